```python
import math
import jax, jax.numpy as jnp
from jax import lax
import numpy as np

D_MODEL = 1024
BATCH = 1
SEQ = 16384
DEPTH = 2
DEC_BATCH = 32
DEC_SEQ = 4
PAST_LEN = 16384
PAGE_SIZE = 128

ATT_GROUPS = ((128, 1), (512, 4), (2048, 16))
ATT_HPG = 4
ATT_HEAD_DIM = 64
ATT_HEADS = ATT_HPG * len(ATT_GROUPS)
ATT_WIDTH = ATT_HEADS * ATT_HEAD_DIM
ATT_OUT = ATT_HPG * ATT_HEAD_DIM
DN_HEADS = 4
DN_HEAD_DIM = 128
DN_WIDTH = DN_HEADS * DN_HEAD_DIM
DN_CONV = 4
DN_CHUNK = 64
RW_HEADS = 8
RW_HEAD_DIM = 64
RW_WIDTH = RW_HEADS * RW_HEAD_DIM
RW_DECAY_LORA = 64
RW_A_LORA = 64
RW_GATE_LORA = 128
RW_GN_EPS = 64e-5
PEER_KEYS = 128
PEER_EXPERTS = PEER_KEYS * PEER_KEYS
PEER_HEADS = 8
PEER_QDIM = 128
PEER_TOPK = 16
PEER_BLOCK = 128

RMS_EPS = 1e-6
L2_EPS = 1e-6

A_COLS = 3 * ATT_WIDTH
B_COLS = 4 * DN_WIDTH + 2 * DN_HEADS
C_COLS = 3 * RW_WIDTH + RW_DECAY_LORA + RW_A_LORA + RW_GATE_LORA
G_COLS = 3 * D_MODEL
IN_COLS = A_COLS + B_COLS + C_COLS + G_COLS

kernel_name = 'hybrid_dilswa_gdn_rwkv7_peer'


def rmsnorm(x, w):
    xf = x.astype(jnp.float32)
    y = xf * lax.rsqrt(jnp.mean(xf * xf, axis=-1, keepdims=True) + RMS_EPS)
    return (y * w.astype(jnp.float32)).astype(x.dtype)


def l2norm(x):
    return x * lax.rsqrt(jnp.sum(x * x, axis=-1, keepdims=True) + L2_EPS)


def alibi_slopes():
    h = np.arange(1, ATT_HEADS + 1, dtype=np.float32)
    return jnp.asarray(np.power(np.float32(2.0), -8.0 * h / ATT_HEADS).astype(np.float32))


def dilated_attn_prompt(q, k, v, slopes, window, dil):
    B, T, H, E = q.shape
    q, k, v = q.astype(jnp.float32), k.astype(jnp.float32), v.astype(jnp.float32)
    n = window // dil
    L = T // dil
    nb = -(-L // n)
    Lp = nb * n

    def to_blocks(z):
        z = z.reshape(B, L, dil, H, E)
        z = jnp.pad(z, ((0, 0), (0, Lp - L), (0, 0), (0, 0), (0, 0)))
        return z.reshape(B, nb, n, dil, H, E)

    def with_prev(z):
        prev = jnp.pad(z[:, :-1], ((0, 0), (1, 0), (0, 0), (0, 0), (0, 0), (0, 0)))
        return jnp.concatenate([prev, z], axis=2)

    qb = to_blocks(q)
    kk = with_prev(to_blocks(k))
    vv = with_prev(to_blocks(v))
    s = jnp.einsum('bnqrhe,bnkrhe->bnrhqk', qb, kk) * (E ** -0.5)
    i = np.arange(n)[:, None]
    j = np.arange(2 * n)[None, :]
    m = n + i - j
    band = (m >= 0) & (m <= n)
    first = (np.arange(nb) == 0)[:, None, None]
    valid = band[None] & ~(first & (j[None] < n))
    bias = -slopes[:, None, None] * jnp.asarray((m * dil).astype(np.float32))[None]
    s = jnp.where(valid[None, :, None, None], s + bias[None, None, None], -jnp.inf)
    mx = jnp.max(s, axis=-1, keepdims=True)
    p = jnp.exp(s - mx)
    den = jnp.sum(p, axis=-1)
    o = jnp.einsum('bnrhqk,bnkrhe->bnqrhe', p, vv)
    o = o / jnp.transpose(den, (0, 1, 4, 2, 3))[..., None]
    lse = jnp.transpose(mx[..., 0] + jnp.log(den), (0, 1, 4, 2, 3))
    o = o.reshape(B, Lp, dil, H, E)[:, :L].reshape(B, T, H, E)
    lse = lse.reshape(B, Lp, dil, H)[:, :L].reshape(B, T, H)
    return o, lse


def dilated_attn_step(q, k_new, v_new, kv_buf, slopes, window, dil):
    Bd, S, H, E = q.shape
    Lb = kv_buf.shape[1]
    n = window // dil
    f32 = jnp.float32
    kall = jnp.concatenate([kv_buf[:, :, 0].astype(f32), k_new.astype(f32)], axis=1)
    vall = jnp.concatenate([kv_buf[:, :, 1].astype(f32), v_new.astype(f32)], axis=1)
    steps = np.arange(n + 1)
    idx = Lb + np.arange(S)[:, None] - dil * steps[None, :]
    valid = idx >= 0
    idx_c = np.clip(idx, 0, None)
    kg = kall[:, idx_c]
    vg = vall[:, idx_c]
    s = jnp.einsum('bshe,bsmhe->bhsm', q.astype(f32), kg) * (E ** -0.5)
    dist = jnp.asarray((dil * steps).astype(np.float32))
    s = s - slopes[:, None, None] * dist[None, None, :]
    s = jnp.where(valid[None, None], s, -jnp.inf)
    mx = jnp.max(s, axis=-1, keepdims=True)
    p = jnp.exp(s - mx)
    den = jnp.sum(p, axis=-1)
    o = jnp.einsum('bhsm,bsmhe->bshe', p, vg) / jnp.transpose(den, (0, 2, 1))[..., None]
    lse = jnp.transpose(mx[..., 0] + jnp.log(den), (0, 2, 1))
    return o, lse


def causal_conv_silu(x, buf, w):
    xx = jnp.concatenate([buf.astype(x.dtype), x], axis=1)
    T = x.shape[1]
    y = xx[:, 0:T] * w[0]
    for j in range(1, DN_CONV):
        y = y + xx[:, j:j + T] * w[j]
    return jax.nn.silu(y), xx[:, T:]


def gated_delta_chunked(q, k, v, beta, g, S0):
    B, T, H, K = q.shape
    V = v.shape[-1]
    C = DN_CHUNK if T % DN_CHUNK == 0 else T
    N = T // C

    def blk(z):
        z = z.reshape((B, N, C, H) + z.shape[3:])
        return jnp.moveaxis(z, 3, 1)

    q, k, v, beta, g = blk(q), blk(k), blk(v), blk(beta), blk(g)
    G = jnp.cumsum(g, axis=-1)
    incl = np.tril(np.ones((C, C), dtype=bool))
    strict = np.tril(np.ones((C, C), dtype=bool), -1)
    decay = jnp.exp(jnp.where(incl, G[..., :, None] - G[..., None, :], -jnp.inf))
    k_beta = k * beta[..., None]
    A = jnp.where(strict, jnp.einsum('bhnck,bhndk->bhncd', k_beta, k) * decay, 0.0)
    rhs = jnp.concatenate([v * beta[..., None], k_beta * jnp.exp(G)[..., None]], axis=-1)
    sol = lax.linalg.triangular_solve(A + jnp.eye(C, dtype=A.dtype), rhs, left_side=True,
                                      lower=True, unit_diagonal=True)
    u, w = sol[..., :V], sol[..., V:]
    attn = jnp.einsum('bhnck,bhndk->bhncd', q, k) * decay
    q_dec = q * jnp.exp(G)[..., None]
    k_dec = k * jnp.exp(G[..., -1:] - G)[..., None]
    g_last = jnp.exp(G[..., -1])

    def step(S, xs):
        u_c, w_c, attn_c, q_c, k_c, gl_c = xs
        v_new = u_c - jnp.einsum('bhck,bhkv->bhcv', w_c, S)
        o = jnp.einsum('bhck,bhkv->bhcv', q_c, S) + jnp.einsum('bhcd,bhdv->bhcv', attn_c, v_new)
        S = S * gl_c[..., None, None] + jnp.einsum('bhck,bhcv->bhkv', k_c, v_new)
        return S, o

    xs = tuple(jnp.moveaxis(z, 2, 0) for z in (u, w, attn, q_dec, k_dec, g_last))
    S, o = lax.scan(step, S0, xs)
    o = jnp.transpose(o, (1, 0, 3, 2, 4)).reshape(B, T, H, V)
    return o, S


def rwkv7_scan(r, w, k, v, kk, a, S0):
    def step(S, xs):
        r_t, w_t, k_t, v_t, kk_t, a_t = xs
        sa = jnp.einsum('bhvk,bhk->bhv', S, -kk_t)
        S = (S * w_t[:, :, None, :] + sa[..., None] * (kk_t * a_t)[:, :, None, :]
             + v_t[..., None] * k_t[:, :, None, :])
        return S, jnp.einsum('bhvk,bhk->bhv', S, r_t)

    xs = tuple(jnp.transpose(z, (1, 0, 2, 3)) for z in (r, w, k, v, kk, a))
    S, o = lax.scan(step, S0, xs)
    return jnp.transpose(o, (1, 0, 2, 3)), S


def token_mixers(xn, kv_bufs, dn_S, dn_conv, rw_S, rw_shift, lp, is_prompt):
    (w_in, dn_conv_w, dn_a_log, dn_dt_bias, dn_norm_w, rw_mu, rw_w0, rw_w2, rw_a0, rw_a2,
     rw_g2, rw_k_k, rw_k_a, rw_r_k, rw_ln_w, rw_ln_b, w_br_a, w_br_b, w_br_c, w_out) = lp
    B, T, _ = xn.shape
    dt = xn.dtype
    f32 = jnp.float32
    proj = xn @ w_in
    pa, pb, pc, pg = jnp.split(proj, [A_COLS, A_COLS + B_COLS, A_COLS + B_COLS + C_COLS], axis=-1)

    qa, ka, va = [z.reshape(B, T, ATT_HEADS, ATT_HEAD_DIM) for z in jnp.split(pa, 3, axis=-1)]
    slopes = alibi_slopes()
    outs, lses, new_kv = [], [], []
    for gi, (win, dil) in enumerate(ATT_GROUPS):
        hs = slice(gi * ATT_HPG, (gi + 1) * ATT_HPG)
        qg, kg, vg = qa[:, :, hs], ka[:, :, hs], va[:, :, hs]
        if is_prompt:
            o, lse = dilated_attn_prompt(qg, kg, vg, slopes[hs], win, dil)
            keep = min(win, T)
            new_kv.append(jnp.stack([kg[:, T - keep:], vg[:, T - keep:]], axis=2))
        else:
            o, lse = dilated_attn_step(qg, kg, vg, kv_bufs[gi], slopes[hs], win, dil)
            new_kv.append(jnp.stack([kg, vg], axis=2).astype(kv_bufs[gi].dtype))
        outs.append(o)
        lses.append(lse)
    wgt = jax.nn.softmax(jnp.stack(lses, 0), axis=0)
    o_a = jnp.sum(wgt[..., None] * jnp.stack(outs, 0), axis=0).reshape(B, T, ATT_OUT).astype(dt)

    qkv_b, z_b, beta_b, alpha_b = jnp.split(
        pb, [3 * DN_WIDTH, 4 * DN_WIDTH, 4 * DN_WIDTH + DN_HEADS], axis=-1)
    qkv_b, new_conv = causal_conv_silu(qkv_b, dn_conv, dn_conv_w)
    qb, kb, vb = [z.reshape(B, T, DN_HEADS, DN_HEAD_DIM).astype(f32) for z in jnp.split(qkv_b, 3, axis=-1)]
    qb = l2norm(qb) * (DN_HEAD_DIM ** -0.5)
    kb = l2norm(kb)
    beta = jax.nn.sigmoid(beta_b.astype(f32))
    g = -jnp.exp(dn_a_log.astype(f32)) * jax.nn.softplus(alpha_b.astype(f32) + dn_dt_bias.astype(f32))
    o_b, new_dn_S = gated_delta_chunked(qb, kb, vb, beta, g, dn_S.astype(f32))
    o_b = o_b * lax.rsqrt(jnp.mean(o_b * o_b, axis=-1, keepdims=True) + RMS_EPS) * dn_norm_w.astype(f32)
    o_b = o_b * jax.nn.silu(z_b.astype(f32)).reshape(B, T, DN_HEADS, DN_HEAD_DIM)
    o_b = o_b.reshape(B, T, DN_WIDTH).astype(dt)

    prev = jnp.concatenate([rw_shift[:, None].astype(dt), pc[:, :-1]], axis=1)
    xc = (pc + (prev - pc) * rw_mu).astype(f32)
    new_shift = pc[:, -1]
    o1 = 3 * RW_WIDTH + RW_DECAY_LORA
    r, k, v, wd, ad, gd = jnp.split(
        xc, [RW_WIDTH, 2 * RW_WIDTH, 3 * RW_WIDTH, o1, o1 + RW_A_LORA], axis=-1)
    w_log = -jax.nn.softplus(-(rw_w0.astype(f32) + jnp.tanh(wd) @ rw_w2.astype(f32))) - 0.5
    decay = jnp.exp(-jnp.exp(w_log))
    a = jax.nn.sigmoid(rw_a0.astype(f32) + ad @ rw_a2.astype(f32))
    gate = jax.nn.sigmoid(gd) @ rw_g2.astype(f32)

    def hd(z):
        return z.reshape(B, T, RW_HEADS, RW_HEAD_DIM)

    kk = l2norm(hd(k * rw_k_k.astype(f32)))
    k = k * (1.0 + (a - 1.0) * rw_k_a.astype(f32))
    r, k, v, a, decay = hd(r), hd(k), hd(v), hd(a), hd(decay)
    o_c, new_rw_S = rwkv7_scan(r, decay, k, v, kk, a, rw_S.astype(f32))
    mu = jnp.mean(o_c, axis=-1, keepdims=True)
    var = jnp.mean(jnp.square(o_c - mu), axis=-1, keepdims=True)
    o_c = ((o_c - mu) * lax.rsqrt(var + RW_GN_EPS)).reshape(B, T, RW_WIDTH)
    o_c = o_c * rw_ln_w.astype(f32) + rw_ln_b.astype(f32)
    bonus = jnp.sum(r * k * rw_r_k.astype(f32), axis=-1, keepdims=True) * v
    o_c = ((o_c + bonus.reshape(B, T, RW_WIDTH)) * gate).astype(dt)

    ga, gb, gc = jnp.split(pg, 3, axis=-1)
    merged = (jax.nn.sigmoid(ga) * (o_a @ w_br_a) + jax.nn.sigmoid(gb) * (o_b @ w_br_b)
              + jax.nn.sigmoid(gc) * (o_c @ w_br_c))
    out = (merged @ w_out).astype(dt)
    return out, new_kv, new_dn_S, new_conv, new_rw_S, new_shift


def peer(xn, wq, subkeys, u, v):
    B, T, D = xn.shape
    M = B * T
    nblk = -(-M // PEER_BLOCK)
    flat = jnp.pad(xn.reshape(M, D), ((0, nblk * PEER_BLOCK - M), (0, 0))).reshape(nblk, PEER_BLOCK, D)
    K = PEER_TOPK

    def block(xb):
        q = (xb @ wq).reshape(PEER_BLOCK, PEER_HEADS, 2, PEER_QDIM // 2).astype(jnp.float32)
        s = jnp.einsum('bhpc,hpnc->bhpn', q, subkeys.astype(jnp.float32))
        ts, ti = lax.top_k(s, K)
        cand = (ts[:, :, 0, :, None] + ts[:, :, 1, None, :]).reshape(PEER_BLOCK, PEER_HEADS, K * K)
        cidx = (ti[:, :, 0, :, None] * PEER_KEYS + ti[:, :, 1, None, :]).reshape(PEER_BLOCK, PEER_HEADS, K * K)
        fs, fi = lax.top_k(cand, K)
        eidx = jnp.take_along_axis(cidx, fi, axis=-1)
        gsm = jax.nn.softmax(fs, axis=-1)
        act = jax.nn.gelu(jnp.einsum('bd,bhkd->bhk', xb.astype(jnp.float32), u[eidx].astype(jnp.float32)))
        return jnp.einsum('bhk,bhkd->bd', (gsm * act).astype(xb.dtype), v[eidx])

    y = lax.map(block, flat).reshape(nblk * PEER_BLOCK, D)[:M]
    return y.reshape(B, T, D).astype(xn.dtype)


def setup_inputs(seed: int = 0) -> dict:
    key = jax.random.key(seed)
    ks = iter(jax.random.split(key, 48))

    def nrm(shape, scale):
        return jax.random.normal(next(ks), shape, jnp.float32) * scale

    def unif(shape, lo, hi):
        return jax.random.uniform(next(ks), shape, jnp.float32, lo, hi)

    D = D_MODEL
    Lw = [min(w, PAST_LEN) for w, _ in ATT_GROUPS]
    dt_init = jnp.exp(unif((DEPTH, DN_HEADS), math.log(1e-3), math.log(1e-1)))
    return {
        'x_prompt': nrm((BATCH, SEQ, D), 1.0),
        'x_sample': nrm((DEC_BATCH, DEC_SEQ, D), 1.0),
        'cache_kv_w128': nrm((DEPTH, DEC_BATCH, Lw[0], 2, ATT_HPG, ATT_HEAD_DIM), 1.0),
        'cache_kv_w512': nrm((DEPTH, DEC_BATCH, Lw[1], 2, ATT_HPG, ATT_HEAD_DIM), 1.0),
        'cache_kv_w2048': nrm((DEPTH, DEC_BATCH, Lw[2], 2, ATT_HPG, ATT_HEAD_DIM), 1.0),
        'state_dn': nrm((DEPTH, DEC_BATCH, DN_HEADS, DN_HEAD_DIM, DN_HEAD_DIM), 0.3),
        'state_dn_conv': nrm((DEPTH, DEC_BATCH, DN_CONV - 1, 3 * DN_WIDTH), 1.0),
        'state_rw': nrm((DEPTH, DEC_BATCH, RW_HEADS, RW_HEAD_DIM, RW_HEAD_DIM), 0.3),
        'state_rw_shift': nrm((DEPTH, DEC_BATCH, C_COLS), 1.0),
        'norm_mix': 1.0 + nrm((DEPTH, D), 0.02),
        'w_in': nrm((DEPTH, D, IN_COLS), D ** -0.5),
        'dn_conv_w': nrm((DEPTH, DN_CONV, 3 * DN_WIDTH), DN_CONV ** -0.5),
        'dn_a_log': jnp.log(unif((DEPTH, DN_HEADS), 1.0, 16.0)),
        'dn_dt_bias': dt_init + jnp.log(-jnp.expm1(-dt_init)),
        'dn_norm_w': 1.0 + nrm((DEPTH, DN_HEAD_DIM), 0.02),
        'rw_mu': unif((DEPTH, C_COLS), 0.0, 1.0),
        'rw_w0': unif((DEPTH, RW_WIDTH), -5.0, 0.5),
        'rw_w2': nrm((DEPTH, RW_DECAY_LORA, RW_WIDTH), 0.1),
        'rw_a0': nrm((DEPTH, RW_WIDTH), 0.1),
        'rw_a2': nrm((DEPTH, RW_A_LORA, RW_WIDTH), RW_A_LORA ** -0.5),
        'rw_g2': nrm((DEPTH, RW_GATE_LORA, RW_WIDTH), RW_GATE_LORA ** -0.5),
        'rw_k_k': 0.85 + nrm((DEPTH, RW_WIDTH), 0.05),
        'rw_k_a': 1.0 + nrm((DEPTH, RW_WIDTH), 0.05),
        'rw_r_k': nrm((DEPTH, RW_HEADS, RW_HEAD_DIM), 0.1),
        'rw_ln_w': 1.0 + nrm((DEPTH, RW_WIDTH), 0.02),
        'rw_ln_b': nrm((DEPTH, RW_WIDTH), 0.02),
        'w_br_a': nrm((DEPTH, ATT_OUT, D), ATT_OUT ** -0.5),
        'w_br_b': nrm((DEPTH, DN_WIDTH, D), DN_WIDTH ** -0.5),
        'w_br_c': nrm((DEPTH, RW_WIDTH, D), RW_WIDTH ** -0.5),
        'w_out': nrm((DEPTH, D, D), D ** -0.5),
        'norm_ffn': 1.0 + nrm((DEPTH, D), 0.02),
        'peer_wq': nrm((DEPTH, D, PEER_HEADS * PEER_QDIM), D ** -0.5),
        'peer_subkeys': nrm((DEPTH, PEER_HEADS, 2, PEER_KEYS, PEER_QDIM // 2), (PEER_QDIM // 2) ** -0.5),
        'peer_u': nrm((DEPTH, PEER_EXPERTS, D), D ** -0.5),
        'peer_v': nrm((DEPTH, PEER_EXPERTS, D), PEER_HEADS ** -0.5),
        'norm_final': 1.0 + nrm((D,), 0.02),
    }


def reference(x_prompt, x_sample, cache_kv_w128, cache_kv_w512, cache_kv_w2048,
              state_dn, state_dn_conv, state_rw, state_rw_shift,
              norm_mix, w_in, dn_conv_w, dn_a_log, dn_dt_bias, dn_norm_w,
              rw_mu, rw_w0, rw_w2, rw_a0, rw_a2, rw_g2, rw_k_k, rw_k_a, rw_r_k,
              rw_ln_w, rw_ln_b, w_br_a, w_br_b, w_br_c, w_out,
              norm_ffn, peer_wq, peer_subkeys, peer_u, peer_v, norm_final):
    hp, hs = x_prompt, x_sample
    Bp = x_prompt.shape[0]
    dt = x_prompt.dtype
    zero_dn = jnp.zeros((Bp, DN_HEADS, DN_HEAD_DIM, DN_HEAD_DIM), jnp.float32)
    zero_conv = jnp.zeros((Bp, DN_CONV - 1, 3 * DN_WIDTH), dt)
    zero_rw = jnp.zeros((Bp, RW_HEADS, RW_HEAD_DIM, RW_HEAD_DIM), jnp.float32)
    zero_shift = jnp.zeros((Bp, C_COLS), dt)
    kv_caches = (cache_kv_w128, cache_kv_w512, cache_kv_w2048)
    kvp = [[] for _ in ATT_GROUPS]
    kvs = [[] for _ in ATT_GROUPS]
    dnp, dns, cvp, cvs, rwp, rws, shp, shs = [], [], [], [], [], [], [], []
    for l in range(DEPTH):
        lp = (w_in[l], dn_conv_w[l], dn_a_log[l], dn_dt_bias[l], dn_norm_w[l], rw_mu[l], rw_w0[l],
              rw_w2[l], rw_a0[l], rw_a2[l], rw_g2[l], rw_k_k[l], rw_k_a[l], rw_r_k[l], rw_ln_w[l],
              rw_ln_b[l], w_br_a[l], w_br_b[l], w_br_c[l], w_out[l])
        mo, nkv, nS, nc, nr, nsh = token_mixers(rmsnorm(hp, norm_mix[l]), None, zero_dn, zero_conv,
                                                zero_rw, zero_shift, lp, True)
        hp = hp + mo
        hp = hp + peer(rmsnorm(hp, norm_ffn[l]), peer_wq[l], peer_subkeys[l], peer_u[l], peer_v[l])
        for gi in range(len(ATT_GROUPS)):
            kvp[gi].append(nkv[gi])
        dnp.append(nS.astype(dt))
        cvp.append(nc)
        rwp.append(nr.astype(dt))
        shp.append(nsh)
        mo, nkv, nS, nc, nr, nsh = token_mixers(rmsnorm(hs, norm_mix[l]), [c[l] for c in kv_caches],
                                                state_dn[l], state_dn_conv[l], state_rw[l],
                                                state_rw_shift[l], lp, False)
        hs = hs + mo
        hs = hs + peer(rmsnorm(hs, norm_ffn[l]), peer_wq[l], peer_subkeys[l], peer_u[l], peer_v[l])
        for gi in range(len(ATT_GROUPS)):
            kvs[gi].append(nkv[gi])
        dns.append(nS.astype(state_dn.dtype))
        cvs.append(nc.astype(state_dn_conv.dtype))
        rws.append(nr.astype(state_rw.dtype))
        shs.append(nsh.astype(state_rw_shift.dtype))
    y_prompt = rmsnorm(hp, norm_final)
    y_sample = rmsnorm(hs, norm_final)
    return (y_prompt, y_sample,
            jnp.stack(kvp[0], 0), jnp.stack(kvs[0], 0),
            jnp.stack(kvp[1], 0), jnp.stack(kvs[1], 0),
            jnp.stack(kvp[2], 0), jnp.stack(kvs[2], 0),
            jnp.stack(dnp, 0), jnp.stack(dns, 0),
            jnp.stack(cvp, 0), jnp.stack(cvs, 0),
            jnp.stack(rwp, 0), jnp.stack(rws, 0),
            jnp.stack(shp, 0), jnp.stack(shs, 0))
```

```python
import functools
import math

import jax
import jax.numpy as jnp
import numpy as np
from jax import lax
from jax.experimental import pallas as pl
from jax.experimental.pallas import tpu as pltpu

f32 = jnp.float32
bf16 = jnp.bfloat16

V7X_LANES = 128
V7X_SUBLANES = 8
V7X_VMEM_BYTES = 64 * 1024 * 1024
VMEM_CEILING = 56 * 1024 * 1024

D_MODEL = 1024
ATT_GROUPS = ((128, 1), (512, 4), (2048, 16))
ATT_HPG = 4
ATT_HD = 64
ATT_HEADS = ATT_HPG * len(ATT_GROUPS)
ATT_WIDTH = ATT_HEADS * ATT_HD
ATT_OUT = ATT_HPG * ATT_HD
ATT_STEPS = 128
DN_HEADS = 4
DN_HD = 128
DN_WIDTH = DN_HEADS * DN_HD
DN_CONV = 4
DN_CHUNK = 64
RW_HEADS = 8
RW_HD = 64
RW_WIDTH = RW_HEADS * RW_HD
RW_DECAY_LORA = 64
RW_A_LORA = 64
RW_GATE_LORA = 128
RW_GN_EPS = 64e-5
RW_CHUNK = 64
PEER_KEYS = 128
PEER_EXPERTS = PEER_KEYS * PEER_KEYS
PEER_HEADS = 8
PEER_QDIM = 128
PEER_TOPK = 16
RMS_EPS = 1e-6
L2_EPS = 1e-6

A_COLS = 3 * ATT_WIDTH
B_QKV = 3 * DN_WIDTH
B_MAIN = 4 * DN_WIDTH
B_COLS = B_MAIN + 2 * DN_HEADS
C_COLS = 3 * RW_WIDTH + RW_DECAY_LORA + RW_A_LORA + RW_GATE_LORA
G_COLS = 3 * D_MODEL

ROW_TILE = 512
ATT_TILE = 2048
SEQ_TILE = 512
SAMPLE_ROWS = 8
PEER_TOKENS = 512
PEER_EBLK = 1024

NEG_INF = float("-inf")


def _alibi_slopes():
    h = np.arange(1, ATT_HEADS + 1, dtype=np.float32)
    return np.power(np.float32(2.0), -8.0 * h / ATT_HEADS).astype(np.float32)


def _cparams(semantics, vmem_bytes):
    return pltpu.CompilerParams(dimension_semantics=semantics,
                                vmem_limit_bytes=int(min(max(vmem_bytes, 16 * 1024 * 1024), VMEM_CEILING)))


def _dot(a, b):
    return jnp.dot(a, b, preferred_element_type=f32)


def _dot_nt(a, b):
    return lax.dot_general(a, b, (((1,), (1,)), ((), ())), preferred_element_type=f32)


def _dot_tn(a, b):
    return lax.dot_general(a, b, (((0,), (0,)), ((), ())), preferred_element_type=f32)


def _dot_exact(a, b):
    return jnp.dot(a, b, precision=lax.Precision.HIGHEST, preferred_element_type=f32)


def _bf(x):
    return x.astype(bf16)


def _sigmoid(x):
    return 1.0 / (1.0 + jnp.exp(-x))


def _segsum(x, bd):
    hi = _bf(x)
    lo = _bf(x - hi.astype(f32))
    return _dot(hi, bd) + _dot(lo, bd)


def _seg_cumsum(x, rowc, seg):
    k = 1
    while k < seg:
        x = x + jnp.where(rowc >= k, pltpu.roll(x, k, 0), 0.0)
        k *= 2
    return x


def _unit_lower_inverse(x, size):
    eye = (lax.broadcasted_iota(jnp.int32, (size, size), 0) ==
           lax.broadcasted_iota(jnp.int32, (size, size), 1)).astype(f32)
    t = eye + x
    p = x
    k = 2
    while k < size:
        p = _dot_exact(p, p)
        t = t + _dot_exact(t, p)
        k *= 2
    return t


def _block_diag_ones(width, seg):
    i = np.arange(width)
    return jnp.asarray((i[:, None] // seg) == (i[None, :] // seg), dtype=bf16)


def _norm_matmul_kernel(h_ref, nw_ref, w_ref, o_ref):
    x = h_ref[...]
    xn = x * lax.rsqrt(jnp.mean(x * x, axis=-1, keepdims=True) + RMS_EPS) * nw_ref[...]
    o_ref[...] = _dot(_bf(xn), w_ref[...])


def _norm_matmul(h, nw, w):
    rows, d = h.shape
    n = w.shape[1]
    tr = min(ROW_TILE, rows)
    vmem = 2 * (tr * d * 4 + d * n * 2 + tr * n * 4) + 4 * tr * d * 4
    return pl.pallas_call(
        _norm_matmul_kernel,
        out_shape=jax.ShapeDtypeStruct((rows, n), f32),
        grid=(rows // tr,),
        in_specs=[pl.BlockSpec((tr, d), lambda i: (i, 0)),
                  pl.BlockSpec((1, d), lambda i: (0, 0)),
                  pl.BlockSpec((d, n), lambda i: (0, 0))],
        out_specs=pl.BlockSpec((tr, n), lambda i: (i, 0)),
        compiler_params=_cparams(("parallel",), vmem),
        name="norm_matmul",
    )(h, nw.reshape(1, d), w)


def _rmsnorm_kernel(h_ref, nw_ref, o_ref):
    x = h_ref[...]
    o_ref[...] = x * lax.rsqrt(jnp.mean(x * x, axis=-1, keepdims=True) + RMS_EPS) * nw_ref[...]


def _rmsnorm(h, nw):
    rows, d = h.shape
    tr = min(ROW_TILE, rows)
    return pl.pallas_call(
        _rmsnorm_kernel,
        out_shape=jax.ShapeDtypeStruct((rows, d), f32),
        grid=(rows // tr,),
        in_specs=[pl.BlockSpec((tr, d), lambda i: (i, 0)), pl.BlockSpec((1, d), lambda i: (0, 0))],
        out_specs=pl.BlockSpec((tr, d), lambda i: (i, 0)),
        compiler_params=_cparams(("parallel",), 6 * tr * d * 4),
        name="final_rmsnorm",
    )(h, nw.reshape(1, d))


def _attn_prompt_kernel(q_ref, kc_ref, vc_ref, kp_ref, vp_ref, o_ref, lse_ref, kk_scr, vv_scr, *, dil, slopes):
    n = ATT_STEPS
    nblk = ATT_TILE // (dil * n)
    tile = pl.program_id(0)
    pair = pl.program_id(1)
    kk_scr[0:ATT_TILE, :] = kp_ref[...]
    kk_scr[ATT_TILE:2 * ATT_TILE, :] = kc_ref[...]
    vv_scr[0:ATT_TILE, :] = vp_ref[...]
    vv_scr[ATT_TILE:2 * ATT_TILE, :] = vc_ref[...]
    ii = lax.broadcasted_iota(jnp.int32, (n, 2 * n), 0)
    jj = lax.broadcasted_iota(jnp.int32, (n, 2 * n), 1)
    steps = n + ii - jj
    band = (steps >= 0) & (steps <= n)
    dist = (steps * dil).astype(f32)
    biases = []
    for hh in range(2):
        slope = jnp.where(pair == 0, slopes[hh], slopes[2 + hh])
        biases.append(jnp.where(band, -slope * dist, NEG_INF))

    def body(c, carry):
        r = c % dil
        b = c // dil
        qs = r + dil * n * b
        rows_q = pl.ds(qs, n, stride=dil)
        rows_k = pl.ds(ATT_TILE + qs - dil * n, 2 * n, stride=dil)
        q = q_ref[rows_q, :] * (ATT_HD ** -0.5)
        k = kk_scr[rows_k, :]
        v = vv_scr[rows_k, :]
        first_key = jnp.where(tile * nblk + b == 0, n, 0)
        outs, lses = [], []
        for hh in range(2):
            cols = slice(hh * ATT_HD, (hh + 1) * ATT_HD)
            s = _dot_nt(_bf(q[:, cols]), _bf(k[:, cols])) + biases[hh]
            s = jnp.where(jj >= first_key, s, NEG_INF)
            mx = jnp.max(s, axis=-1, keepdims=True)
            p = jnp.exp(s - mx)
            den = jnp.sum(p, axis=-1, keepdims=True)
            outs.append(_dot(_bf(p), _bf(v[:, cols])) / den)
            lses.append(jnp.broadcast_to(mx + jnp.log(den), (n, ATT_HD)))
        o_ref[rows_q, :] = jnp.concatenate(outs, axis=1)
        lse_ref[rows_q, :] = jnp.concatenate(lses, axis=1)
        return carry

    lax.fori_loop(0, dil * nblk, body, 0)


def _attn_prompt_group(pa, seq_len, gi):
    _, dil = ATT_GROUPS[gi]
    slopes = tuple(float(s) for s in _alibi_slopes()[gi * ATT_HPG:(gi + 1) * ATT_HPG])
    w = V7X_LANES
    qb, kb, vb = (gi * 2, 6 + gi * 2, 12 + gi * 2)
    blk = (ATT_TILE, w)
    cur = lambda base: pl.BlockSpec(blk, lambda i, j, base=base: (i, base + j))
    prev = lambda base: pl.BlockSpec(blk, lambda i, j, base=base: (jnp.maximum(i - 1, 0), base + j))
    vmem = 2 * 7 * ATT_TILE * w * 4 + 2 * 2 * ATT_TILE * w * 4 + 8 * 1024 * 1024
    return pl.pallas_call(
        functools.partial(_attn_prompt_kernel, dil=dil, slopes=slopes),
        out_shape=(jax.ShapeDtypeStruct((seq_len, ATT_OUT), f32), jax.ShapeDtypeStruct((seq_len, ATT_OUT), f32)),
        grid=(seq_len // ATT_TILE, 2),
        in_specs=[cur(qb), cur(kb), cur(vb), prev(kb), prev(vb)],
        out_specs=(pl.BlockSpec(blk, lambda i, j: (i, j)), pl.BlockSpec(blk, lambda i, j: (i, j))),
        scratch_shapes=[pltpu.VMEM((2 * ATT_TILE, w), f32), pltpu.VMEM((2 * ATT_TILE, w), f32)],
        compiler_params=_cparams(("parallel", "parallel"), vmem),
        name=f"attn_prompt_g{gi}",
    )(pa, pa, pa, pa, pa)


def _attn_sample_kernel(q_ref, k_ref, v_ref, c0_ref, c1_ref, c2_ref, o_ref, *, slopes, seq):
    rows = 2 * seq
    caches = (c0_ref, c1_ref, c2_ref)
    row = lax.broadcasted_iota(jnp.int32, (rows, 1), 0)
    own0 = row < seq
    qpos = row % seq
    npos = lax.broadcasted_iota(jnp.int32, (rows, rows), 1)
    q_all = q_ref[0] * (ATT_HD ** -0.5)
    k_all = k_ref[0]
    v_all = v_ref[0]
    outs, lses = [], []
    for gi, (win, dil) in enumerate(ATT_GROUPS):
        cpos = lax.broadcasted_iota(jnp.int32, (rows, win), 1)
        dist_c = win + qpos - cpos
        ok_c = (dist_c <= win) & ((dist_c & (dil - 1)) == 0)
        dist_n = row - npos
        ok_n = (dist_n >= 0) & (dist_n <= qpos) & ((dist_n & (dil - 1)) == 0)
        dcf = dist_c.astype(f32)
        dnf = dist_n.astype(f32)
        for h in range(ATT_HPG):
            slope = slopes[gi * ATT_HPG + h]
            cols = slice(gi * ATT_OUT + h * ATT_HD, gi * ATT_OUT + (h + 1) * ATT_HD)
            kcols = slice(h * ATT_HD, (h + 1) * ATT_HD)
            vcols = slice(ATT_OUT + h * ATT_HD, ATT_OUT + (h + 1) * ATT_HD)
            qh = _bf(q_all[:, cols])
            sn = jnp.where(ok_n, _dot_nt(qh, _bf(k_all[:, cols])) - slope * dnf, NEG_INF)
            scs = []
            for b in range(2):
                sc = _dot_nt(qh, _bf(caches[gi][0, b, :, kcols])) - slope * dcf
                scs.append(jnp.where(ok_c, sc, NEG_INF))
            sc = jnp.where(own0, scs[0], scs[1])
            mx = jnp.maximum(jnp.max(sc, axis=-1, keepdims=True), jnp.max(sn, axis=-1, keepdims=True))
            pc = jnp.exp(sc - mx)
            pn = jnp.exp(sn - mx)
            den = jnp.sum(pc, axis=-1, keepdims=True) + jnp.sum(pn, axis=-1, keepdims=True)
            pcb = _bf(pc)
            oc = jnp.where(own0, _dot(pcb, _bf(caches[gi][0, 0, :, vcols])), _dot(pcb, _bf(caches[gi][0, 1, :, vcols])))
            o = (oc + _dot(_bf(pn), _bf(v_all[:, cols]))) / den
            outs.append(o)
            lses.append(mx + jnp.log(den))
    merged = []
    for h in range(ATT_HPG):
        ls = [lses[gi * ATT_HPG + h] for gi in range(len(ATT_GROUPS))]
        mx = jnp.maximum(jnp.maximum(ls[0], ls[1]), ls[2])
        es = [jnp.exp(l - mx) for l in ls]
        den = es[0] + es[1] + es[2]
        acc = es[0] * outs[h] + es[1] * outs[ATT_HPG + h] + es[2] * outs[2 * ATT_HPG + h]
        merged.append(acc / den)
    o_ref[0] = jnp.concatenate(merged, axis=1)


def _attn_sample(pa, caches, layer, nseq, seq):
    rows = 2 * seq
    assert rows == SAMPLE_ROWS and nseq % 2 == 0
    slopes = tuple(float(s) for s in _alibi_slopes())
    pa3 = pa.reshape(nseq // 2, rows, A_COLS)
    in_specs = [pl.BlockSpec((1, rows, ATT_WIDTH), lambda i, c=c: (i, 0, c)) for c in range(3)]
    vmem = 0
    for (win, _), c in zip(ATT_GROUPS, caches):
        assert c.shape[2] == win, "cached window must hold exactly `window` rows"
        in_specs.append(pl.BlockSpec((1, 2, win, 2 * ATT_OUT), lambda i: (layer, i, 0, 0)))
        vmem += 2 * 2 * win * 2 * ATT_OUT * 4
    out = pl.pallas_call(
        functools.partial(_attn_sample_kernel, slopes=slopes, seq=seq),
        out_shape=jax.ShapeDtypeStruct((nseq // 2, rows, ATT_OUT), f32),
        grid=(nseq // 2,),
        in_specs=in_specs,
        out_specs=pl.BlockSpec((1, rows, ATT_OUT), lambda i: (i, 0, 0)),
        compiler_params=_cparams(("parallel",), vmem + 16 * 1024 * 1024),
        name="attn_sample",
    )(pa3, pa3, pa3, *caches)
    return out.reshape(nseq * seq, ATT_OUT)


def _dn_kernel(pb_ref, pbg_ref, conv0_ref, s0_ref, cw_ref, prm_ref, nw_ref, bd_ref,
               o_ref, sout_ref,
               xs_scr, q_scr, k_scr, v_scr, g_scr, b_scr, oc_scr, s_scr, *, tt, chunk, n_valid):
    t = pl.program_id(1)

    @pl.when(t == 0)
    def _():
        xs_scr[0:V7X_SUBLANES, :] = conv0_ref[0]
        s_scr[...] = s0_ref[0]

    x = pb_ref[:, 0:B_QKV]
    xs_scr[pl.ds(V7X_SUBLANES, tt), :] = x
    cw = cw_ref[...]
    y = x * cw[3:4, :]
    for j in range(DN_CONV - 1):
        y = y + xs_scr[pl.ds(V7X_SUBLANES - (DN_CONV - 1) + j, tt), :] * cw[j:j + 1, :]
    xs_scr[0:V7X_SUBLANES, :] = xs_scr[pl.ds(tt, V7X_SUBLANES), :]
    y = y * _sigmoid(y)
    bd = bd_ref[...]
    q = y[:, 0:DN_WIDTH]
    k = y[:, DN_WIDTH:2 * DN_WIDTH]
    q_scr[...] = q * lax.rsqrt(_segsum(q * q, bd) + L2_EPS) * (DN_HD ** -0.5)
    k_scr[...] = k * lax.rsqrt(_segsum(k * k, bd) + L2_EPS)
    v_scr[...] = y[:, 2 * DN_WIDTH:3 * DN_WIDTH]
    pg = pbg_ref[...]
    beta = _sigmoid(pg)
    g = -jnp.exp(prm_ref[0:1, :]) * jax.nn.softplus(pg + prm_ref[1:2, :])
    rown = lax.broadcasted_iota(jnp.int32, (tt, V7X_LANES), 0)
    if n_valid < tt:
        beta = jnp.where(rown < n_valid, beta, 0.0)
        g = jnp.where(rown < n_valid, g, 0.0)
    g_scr[...] = _seg_cumsum(g, rown & (chunk - 1), chunk)
    b_scr[...] = beta

    ci = lax.broadcasted_iota(jnp.int32, (chunk, chunk), 0)
    cj = lax.broadcasted_iota(jnp.int32, (chunk, chunk), 1)
    incl = ci >= cj
    strict = ci > cj

    def chunk_body(c, carry):
        rows = pl.ds(pl.multiple_of(c * chunk, chunk), chunk)
        gc = g_scr[rows, :]
        bc = b_scr[rows, :]
        gct = gc.T
        for h in range(DN_HEADS):
            cols = slice(h * DN_HD, (h + 1) * DN_HD)
            qh = q_scr[rows, cols]
            kh = k_scr[rows, cols]
            vh = v_scr[rows, cols]
            gcol = gc[:, DN_HEADS + h:DN_HEADS + h + 1]
            grow = gct[DN_HEADS + h:DN_HEADS + h + 1, :]
            bcol = bc[:, h:h + 1]
            decay = jnp.exp(jnp.where(incl, gcol - grow, NEG_INF))
            kb = kh * bcol
            khb = _bf(kh)
            a = jnp.where(strict, _dot_nt(_bf(kb), khb) * decay, 0.0)
            tm = _unit_lower_inverse(-a, chunk)
            eg = jnp.exp(gcol)
            rhs = jnp.concatenate([vh * bcol, kb * eg], axis=1)
            sol = _dot(_bf(tm), _bf(rhs))
            u = sol[:, 0:DN_HD]
            w = sol[:, DN_HD:2 * DN_HD]
            attn = jnp.where(incl, _dot_nt(_bf(qh), khb) * decay, 0.0)
            glast = gc[chunk - 1:chunk, DN_HEADS + h:DN_HEADS + h + 1]
            k_dec = kh * jnp.exp(glast - gcol)
            s = s_scr[h]
            ws_qs = _dot(_bf(jnp.concatenate([w, qh * eg], axis=0)), _bf(s))
            v_new = u - ws_qs[0:chunk]
            oc_scr[rows, cols] = ws_qs[chunk:2 * chunk] + _dot(_bf(attn), _bf(v_new))
            s_scr[h] = s * jnp.exp(glast) + _dot_tn(_bf(k_dec), _bf(v_new))
        return carry

    lax.fori_loop(0, tt // chunk, chunk_body, 0)

    o = oc_scr[...]
    z = pb_ref[:, B_QKV:B_MAIN]
    o = o * lax.rsqrt(_segsum(o * o, bd) * (1.0 / DN_HD) + RMS_EPS) * nw_ref[...]
    o_ref[...] = o * (z * _sigmoid(z))

    @pl.when(t == pl.num_programs(1) - 1)
    def _():
        sout_ref[0] = s_scr[...]


def _dn_mixer(pb, pbg, conv0, s0, lw, *, nseq, tt, chunk, n_valid):
    rows = pb.shape[0]
    nt = rows // (nseq * tt)
    vmem = (2 * (tt * (B_MAIN + V7X_LANES + DN_WIDTH) * 4 + DN_WIDTH * DN_WIDTH * 2)
            + (tt + 8) * B_QKV * 4 + 4 * tt * DN_WIDTH * 4 + 2 * tt * V7X_LANES * 4
            + 6 * DN_HEADS * DN_HD * DN_HD * 4 + 8 * tt * B_QKV * 4)
    return pl.pallas_call(
        functools.partial(_dn_kernel, tt=tt, chunk=chunk, n_valid=n_valid),
        out_shape=(jax.ShapeDtypeStruct((rows, DN_WIDTH), f32),
                   jax.ShapeDtypeStruct((nseq, DN_HEADS, DN_HD, DN_HD), f32)),
        grid=(nseq, nt),
        in_specs=[pl.BlockSpec((tt, B_MAIN), lambda s, t: (s * nt + t, 0)),
                  pl.BlockSpec((tt, V7X_LANES), lambda s, t: (s * nt + t, 0)),
                  pl.BlockSpec((1, V7X_SUBLANES, B_QKV), lambda s, t: (s, 0, 0)),
                  pl.BlockSpec((1, DN_HEADS, DN_HD, DN_HD), lambda s, t: (s, 0, 0, 0)),
                  pl.BlockSpec((DN_CONV, B_QKV), lambda s, t: (0, 0)),
                  pl.BlockSpec((V7X_SUBLANES, V7X_LANES), lambda s, t: (0, 0)),
                  pl.BlockSpec((1, DN_WIDTH), lambda s, t: (0, 0)),
                  pl.BlockSpec((DN_WIDTH, DN_WIDTH), lambda s, t: (0, 0))],
        out_specs=(pl.BlockSpec((tt, DN_WIDTH), lambda s, t: (s * nt + t, 0)),
                   pl.BlockSpec((1, DN_HEADS, DN_HD, DN_HD), lambda s, t: (s, 0, 0, 0))),
        scratch_shapes=[pltpu.VMEM((tt + V7X_SUBLANES, B_QKV), f32),
                        pltpu.VMEM((tt, DN_WIDTH), f32), pltpu.VMEM((tt, DN_WIDTH), f32),
                        pltpu.VMEM((tt, DN_WIDTH), f32),
                        pltpu.VMEM((tt, V7X_LANES), f32), pltpu.VMEM((tt, V7X_LANES), f32),
                        pltpu.VMEM((tt, DN_WIDTH), f32),
                        pltpu.VMEM((DN_HEADS, DN_HD, DN_HD), f32)],
        compiler_params=_cparams(("parallel", "arbitrary"), vmem),
        name="deltanet",
    )(pb, pbg, conv0, s0, lw["dn_conv_w"], lw["dn_prm"], lw["dn_norm_w"], lw["bd128"])


def _rw_kernel(pc_ref, shift0_ref, s0_ref, mu_ref, vec_ref, w2_ref, a2_ref, g2_ref, bd_ref,
               o_ref, sout_ref,
               xs_scr, ar_scr, bk_scr, v_scr, pe_scr, oc_scr, s_scr, *, tt, chunk, n_valid):
    t = pl.program_id(1)

    @pl.when(t == 0)
    def _():
        xs_scr[V7X_SUBLANES - 1:V7X_SUBLANES, :] = shift0_ref[0]
        s_scr[...] = s0_ref[0]

    pc = pc_ref[...]
    xs_scr[pl.ds(V7X_SUBLANES, tt), :] = pc
    prev = xs_scr[pl.ds(V7X_SUBLANES - 1, tt), :]
    xs_scr[0:V7X_SUBLANES, :] = xs_scr[pl.ds(tt, V7X_SUBLANES), :]
    xc = pc + (prev - pc) * mu_ref[...]
    w3 = 3 * RW_WIDTH
    r = xc[:, 0:RW_WIDTH]
    k = xc[:, RW_WIDTH:2 * RW_WIDTH]
    v = xc[:, 2 * RW_WIDTH:w3]
    wd = xc[:, w3:w3 + RW_DECAY_LORA]
    ad = xc[:, w3 + RW_DECAY_LORA:w3 + RW_DECAY_LORA + RW_A_LORA]
    gd = xc[:, w3 + RW_DECAY_LORA + RW_A_LORA:C_COLS]
    w0, a0, k_k, k_a, r_k, ln_w, ln_b = (vec_ref[i:i + 1, :] for i in range(7))
    w_log = -jax.nn.softplus(-(w0 + _dot(_bf(jnp.tanh(wd)), w2_ref[...]))) - 0.5
    lw = -jnp.exp(w_log)
    a = _sigmoid(a0 + _dot(_bf(ad), a2_ref[...]))
    gate = _dot(_bf(_sigmoid(gd)), g2_ref[...])
    bd = bd_ref[...]
    kkr = k * k_k
    kk = kkr * lax.rsqrt(_segsum(kkr * kkr, bd) + L2_EPS)
    k2 = k * (1.0 + (a - 1.0) * k_a)
    bonus = _segsum(r * k2 * r_k, bd) * v
    rown = lax.broadcasted_iota(jnp.int32, (tt, RW_WIDTH), 0)
    if n_valid < tt:
        pad = rown >= n_valid
        lw = jnp.where(pad, 0.0, lw)
        kk = jnp.where(pad, 0.0, kk)
        k2 = jnp.where(pad, 0.0, k2)
    gcum = _seg_cumsum(lw, rown & (chunk - 1), chunk)
    e_pos = jnp.exp(gcum)
    e_neg = jnp.exp(-gcum)
    ar_scr[:, 0:RW_WIDTH] = -kk * jnp.exp(gcum - lw)
    ar_scr[:, RW_WIDTH:2 * RW_WIDTH] = r * e_pos
    bk_scr[:, 0:RW_WIDTH] = kk * a * e_neg
    bk_scr[:, RW_WIDTH:2 * RW_WIDTH] = k2 * e_neg
    v_scr[...] = v
    pe_scr[...] = e_pos

    ci = lax.broadcasted_iota(jnp.int32, (2 * chunk, 2 * chunk), 0)
    cj = lax.broadcasted_iota(jnp.int32, (2 * chunk, 2 * chunk), 1)
    keep = jnp.where(ci < chunk, ci - 1, ci - chunk) >= (cj & (chunk - 1))

    def chunk_body(c, carry):
        rows = pl.ds(pl.multiple_of(c * chunk, chunk), chunk)
        p_tail = pe_scr[pl.ds(pl.multiple_of((c + 1) * chunk - V7X_SUBLANES, V7X_SUBLANES), V7X_SUBLANES), :]
        for h in range(RW_HEADS):
            cols = slice(h * RW_HD, (h + 1) * RW_HD)
            cols2 = slice(RW_WIDTH + h * RW_HD, RW_WIDTH + (h + 1) * RW_HD)
            at = ar_scr[rows, cols]
            rt = ar_scr[rows, cols2]
            bt = bk_scr[rows, cols]
            kt = bk_scr[rows, cols2]
            vh = v_scr[rows, cols]
            bkb = _bf(jnp.concatenate([bt, kt], axis=0))
            m = jnp.where(keep, _dot_nt(_bf(jnp.concatenate([at, rt], axis=0)), bkb), 0.0)
            tm = _unit_lower_inverse(m[0:chunk, 0:chunk], chunk)
            tmb = _bf(tm)
            wmat = _dot(tmb, _bf(at))
            u0 = _dot(tmb, _bf(_dot(_bf(m[0:chunk, chunk:2 * chunk]), _bf(vh))))
            s = s_scr[h]
            ws_rs = _dot_nt(_bf(jnp.concatenate([wmat, rt], axis=0)), _bf(s))
            u = ws_rs[0:chunk] + u0
            uvb = _bf(jnp.concatenate([u, vh], axis=0))
            oc_scr[rows, cols] = ws_rs[chunk:2 * chunk] + _dot(_bf(m[chunk:2 * chunk, :]), uvb)
            s_scr[h] = (s + _dot_tn(uvb, bkb)) * p_tail[V7X_SUBLANES - 1:V7X_SUBLANES, cols]
        return carry

    lax.fori_loop(0, tt // chunk, chunk_body, 0)

    o = oc_scr[...]
    mean = _segsum(o, bd) * (1.0 / RW_HD)
    d = o - mean
    var = _segsum(d * d, bd) * (1.0 / RW_HD)
    o = d * lax.rsqrt(var + RW_GN_EPS) * ln_w + ln_b
    o_ref[...] = (o + bonus) * gate

    @pl.when(t == pl.num_programs(1) - 1)
    def _():
        sout_ref[0] = s_scr[...]


def _rw_mixer(pc, shift0, s0, lw, *, nseq, tt, chunk, n_valid):
    rows = pc.shape[0]
    nt = rows // (nseq * tt)
    vmem = (2 * (tt * (C_COLS + RW_WIDTH) * 4 + RW_WIDTH * RW_WIDTH * 2 + 4 * RW_WIDTH * V7X_LANES * 2)
            + (tt + 8) * C_COLS * 4 + 7 * tt * RW_WIDTH * 4 + 6 * RW_HEADS * RW_HD * RW_HD * 4
            + 14 * tt * RW_WIDTH * 4)
    const = lambda shape: pl.BlockSpec(shape, lambda s, t: (0,) * len(shape))
    return pl.pallas_call(
        functools.partial(_rw_kernel, tt=tt, chunk=chunk, n_valid=n_valid),
        out_shape=(jax.ShapeDtypeStruct((rows, RW_WIDTH), f32),
                   jax.ShapeDtypeStruct((nseq, RW_HEADS, RW_HD, RW_HD), f32)),
        grid=(nseq, nt),
        in_specs=[pl.BlockSpec((tt, C_COLS), lambda s, t: (s * nt + t, 0)),
                  pl.BlockSpec((1, 1, C_COLS), lambda s, t: (s, 0, 0)),
                  pl.BlockSpec((1, RW_HEADS, RW_HD, RW_HD), lambda s, t: (s, 0, 0, 0)),
                  const((1, C_COLS)), const((V7X_SUBLANES, RW_WIDTH)),
                  const((RW_DECAY_LORA, RW_WIDTH)), const((RW_A_LORA, RW_WIDTH)), const((RW_GATE_LORA, RW_WIDTH)),
                  const((RW_WIDTH, RW_WIDTH))],
        out_specs=(pl.BlockSpec((tt, RW_WIDTH), lambda s, t: (s * nt + t, 0)),
                   pl.BlockSpec((1, RW_HEADS, RW_HD, RW_HD), lambda s, t: (s, 0, 0, 0))),
        scratch_shapes=[pltpu.VMEM((tt + V7X_SUBLANES, C_COLS), f32),
                        pltpu.VMEM((tt, 2 * RW_WIDTH), f32), pltpu.VMEM((tt, 2 * RW_WIDTH), f32),
                        pltpu.VMEM((tt, RW_WIDTH), f32), pltpu.VMEM((tt, RW_WIDTH), f32),
                        pltpu.VMEM((tt, RW_WIDTH), f32),
                        pltpu.VMEM((RW_HEADS, RW_HD, RW_HD), f32)],
        compiler_params=_cparams(("parallel", "arbitrary"), vmem),
        name="rwkv7",
    )(pc, shift0, s0, lw["rw_mu"], lw["rw_vec"], lw["rw_w2"], lw["rw_a2"], lw["rw_g2"], lw["bd64"])


def _merge_kernel(h_ref, o0_ref, o1_ref, o2_ref, l0_ref, l1_ref, l2_ref, ob_ref, oc_ref, pg_ref,
                  wa_ref, wb_ref, wc_ref, wo_ref, out_ref):
    l0, l1, l2 = l0_ref[...], l1_ref[...], l2_ref[...]
    mx = jnp.maximum(jnp.maximum(l0, l1), l2)
    e0, e1, e2 = jnp.exp(l0 - mx), jnp.exp(l1 - mx), jnp.exp(l2 - mx)
    oa = (e0 * o0_ref[...] + e1 * o1_ref[...] + e2 * o2_ref[...]) / (e0 + e1 + e2)
    d = D_MODEL
    merged = (_sigmoid(pg_ref[:, 0:d]) * _dot(_bf(oa), wa_ref[...])
              + _sigmoid(pg_ref[:, d:2 * d]) * _dot(_bf(ob_ref[...]), wb_ref[...])
              + _sigmoid(pg_ref[:, 2 * d:3 * d]) * _dot(_bf(oc_ref[...]), wc_ref[...]))
    out_ref[...] = h_ref[...] + _dot(_bf(merged), wo_ref[...])


def _merge_sample_kernel(h_ref, oa_ref, ob_ref, oc_ref, pg_ref, wa_ref, wb_ref, wc_ref, wo_ref, out_ref):
    d = D_MODEL
    merged = (_sigmoid(pg_ref[:, 0:d]) * _dot(_bf(oa_ref[...]), wa_ref[...])
              + _sigmoid(pg_ref[:, d:2 * d]) * _dot(_bf(ob_ref[...]), wb_ref[...])
              + _sigmoid(pg_ref[:, 2 * d:3 * d]) * _dot(_bf(oc_ref[...]), wc_ref[...]))
    out_ref[...] = h_ref[...] + _dot(_bf(merged), wo_ref[...])


def _merge(h, oas, lses, ob, oc, pg, lw):
    rows, d = h.shape
    tr = min(ROW_TILE, rows)
    row = lambda n: pl.BlockSpec((tr, n), lambda i: (i, 0))
    const = lambda a: pl.BlockSpec(a.shape, lambda i: (0, 0))
    ws = (lw["w_br_a"], lw["w_br_b"], lw["w_br_c"], lw["w_out"])
    if lses is None:
        kern, acts = _merge_sample_kernel, (oas,)
    else:
        kern, acts = _merge_kernel, (*oas, *lses)
    acts = (h, *acts, ob, oc, pg)
    vmem = 2 * sum(tr * a.shape[1] * 4 for a in acts) + 2 * sum(w.size * 2 for w in ws) + 8 * tr * d * 4
    return pl.pallas_call(
        kern,
        out_shape=jax.ShapeDtypeStruct((rows, d), f32),
        grid=(rows // tr,),
        in_specs=[row(a.shape[1]) for a in acts] + [const(w) for w in ws],
        out_specs=row(d),
        compiler_params=_cparams(("parallel",), vmem),
        name="merge",
    )(*acts, *ws)


def _peer_candidate_groups(tv_scr):
    k = PEER_TOPK
    s = V7X_SUBLANES
    a_lo = tv_scr[0:s, :]
    a_hi = tv_scr[s:k, :]
    b = [tv_scr[k + j:k + j + 1, :] for j in range(s)]
    b_hi = tv_scr[k + s:2 * k, :]
    row = lax.broadcasted_iota(jnp.int32, a_lo.shape, 0)
    groups = [a_lo + b[0], a_hi + b[0], a_lo + b[1]]
    for j in range(2, s):
        groups.append(jnp.where(row < k // (j + 1), a_lo + b[j], NEG_INF))
    groups.append(tv_scr[0:1, :] + b_hi)
    return groups


def _peer_kernel(h_ref, nw_ref, wq_ref, sk_ref, u_ref, vt_ref, out_ref,
                 xn_scr, s1_scr, e1_scr, s2_scr, e2_scr, thr_scr, tv_scr, acc_scr, *, tokens, eblk):
    e = pl.program_id(1)
    k = PEER_TOPK
    half = PEER_QDIM // 2

    @pl.when(e == 0)
    def _():
        x = h_ref[...]
        xn = _bf(x * lax.rsqrt(jnp.mean(x * x, axis=-1, keepdims=True) + RMS_EPS) * nw_ref[...])
        xn_scr[...] = xn
        q = _bf(_dot(xn, wq_ref[...]))
        for h in range(PEER_HEADS):
            tops = []
            for p, s_scr in ((0, s1_scr), (1, s2_scr)):
                hp = 2 * h + p
                s = _dot_nt(sk_ref[hp], q[:, hp * half:(hp + 1) * half])
                s_scr[h] = s
                cur = s
                for i in range(k):
                    m = jnp.max(cur, axis=0, keepdims=True)
                    tv_scr[p * k + i:p * k + i + 1, :] = m
                    cur = jnp.where(cur >= m, NEG_INF, cur)
                tops.append(tv_scr[p * k:p * k + 1, :])
            groups = _peer_candidate_groups(tv_scr)
            cur = groups
            thr = None
            for i in range(k):
                m = cur[0]
                for g in cur[1:]:
                    m = jnp.maximum(m, g)
                thr = jnp.max(m, axis=0, keepdims=True)
                cur = [jnp.where(g >= thr, NEG_INF, g) for g in cur]
            top = tops[0] + tops[1]
            z = jnp.zeros_like(thr)
            for g in groups:
                z = z + jnp.sum(jnp.where(g >= thr, jnp.exp(g - top), 0.0), axis=0, keepdims=True)
            thr_scr[h:h + 1, :] = thr
            e1_scr[h] = jnp.exp(s1_scr[h] - tops[0])
            e2_scr[h] = jnp.exp(s2_scr[h] - tops[1]) / z
        acc_scr[...] = jnp.zeros_like(acc_scr)

    act = jax.nn.gelu(_dot_nt(u_ref[...], xn_scr[...]))
    parts = []
    for il in range(eblk // PEER_KEYS):
        i1 = e * (eblk // PEER_KEYS) + il
        g = jnp.zeros((PEER_KEYS, tokens), f32)
        for h in range(PEER_HEADS):
            s1row = s1_scr[h, pl.ds(i1, 1), :]
            e1row = e1_scr[h, pl.ds(i1, 1), :]
            sel = (s2_scr[h] + s1row) >= thr_scr[h:h + 1, :]
            g = g + jnp.where(sel, e2_scr[h] * e1row, 0.0)
        parts.append(_bf(g * act[il * PEER_KEYS:(il + 1) * PEER_KEYS, :]))
    acc_scr[...] += _dot(vt_ref[...], jnp.concatenate(parts, axis=0))

    @pl.when(e == pl.num_programs(1) - 1)
    def _():
        out_ref[...] = h_ref[...] + acc_scr[...].T


def _peer(h, lw):
    rows, d = h.shape
    tokens = min(PEER_TOKENS, rows)
    eblk = PEER_EBLK
    n_half = PEER_HEADS * 2
    vmem = (2 * (2 * tokens * d * 4 + d * d * 2 + 2 * eblk * d * 2)
            + tokens * d * 2 + 4 * PEER_HEADS * PEER_KEYS * tokens * 4 + d * tokens * 4
            + 5 * eblk * tokens * 4)
    return pl.pallas_call(
        functools.partial(_peer_kernel, tokens=tokens, eblk=eblk),
        out_shape=jax.ShapeDtypeStruct((rows, d), f32),
        grid=(rows // tokens, PEER_EXPERTS // eblk),
        in_specs=[pl.BlockSpec((tokens, d), lambda i, e: (i, 0)),
                  pl.BlockSpec((1, d), lambda i, e: (0, 0)),
                  pl.BlockSpec((d, PEER_HEADS * PEER_QDIM), lambda i, e: (0, 0)),
                  pl.BlockSpec((n_half, PEER_KEYS, PEER_QDIM // 2), lambda i, e: (0, 0, 0)),
                  pl.BlockSpec((eblk, d), lambda i, e: (e, 0)),
                  pl.BlockSpec((d, eblk), lambda i, e: (0, e))],
        out_specs=pl.BlockSpec((tokens, d), lambda i, e: (i, 0)),
        scratch_shapes=[pltpu.VMEM((tokens, d), bf16),
                        pltpu.VMEM((PEER_HEADS, PEER_KEYS, tokens), f32),
                        pltpu.VMEM((PEER_HEADS, PEER_KEYS, tokens), f32),
                        pltpu.VMEM((PEER_HEADS, PEER_KEYS, tokens), f32),
                        pltpu.VMEM((PEER_HEADS, PEER_KEYS, tokens), f32),
                        pltpu.VMEM((PEER_HEADS, tokens), f32),
                        pltpu.VMEM((2 * PEER_TOPK, tokens), f32),
                        pltpu.VMEM((d, tokens), f32)],
        compiler_params=_cparams(("parallel", "arbitrary"), vmem),
        name="peer",
    )(h, lw["norm_ffn"], lw["peer_wq"], lw["peer_sk"], lw["peer_u"], lw["peer_vt"])


def _layer_weights(l, p):
    w_in = p["w_in"][l]
    o_b, o_c, o_g = A_COLS, A_COLS + B_COLS, A_COLS + B_COLS + C_COLS
    lane = jnp.arange(V7X_LANES)
    on_alpha = (lane >= DN_HEADS) & (lane < 2 * DN_HEADS)

    def alpha_lanes(vec):
        return jnp.where(on_alpha, jnp.pad(vec, (DN_HEADS, V7X_LANES - 2 * DN_HEADS)), 0.0)

    rw_vec = jnp.stack([p["rw_w0"][l], p["rw_a0"][l], p["rw_k_k"][l], p["rw_k_a"][l],
                        p["rw_r_k"][l].reshape(RW_WIDTH), p["rw_ln_w"][l], p["rw_ln_b"][l],
                        jnp.zeros((RW_WIDTH,), f32)])
    return {
        "norm_mix": p["norm_mix"][l],
        "w_a": _bf(w_in[:, 0:A_COLS]),
        "w_b": _bf(w_in[:, o_b:o_b + B_MAIN]),
        "w_bg": _bf(jnp.pad(w_in[:, o_b + B_MAIN:o_c], ((0, 0), (0, V7X_LANES - 2 * DN_HEADS)))),
        "w_c": _bf(w_in[:, o_c:o_g]),
        "w_g": _bf(w_in[:, o_g:]),
        "dn_conv_w": p["dn_conv_w"][l],
        "dn_prm": jnp.zeros((V7X_SUBLANES, V7X_LANES), f32)
                  .at[0].set(alpha_lanes(p["dn_a_log"][l])).at[1].set(alpha_lanes(p["dn_dt_bias"][l])),
        "dn_norm_w": jnp.tile(p["dn_norm_w"][l], DN_HEADS).reshape(1, DN_WIDTH),
        "bd128": _block_diag_ones(DN_WIDTH, DN_HD),
        "bd64": _block_diag_ones(RW_WIDTH, RW_HD),
        "rw_mu": p["rw_mu"][l].reshape(1, C_COLS),
        "rw_vec": rw_vec,
        "rw_w2": _bf(p["rw_w2"][l]), "rw_a2": _bf(p["rw_a2"][l]), "rw_g2": _bf(p["rw_g2"][l]),
        "w_br_a": _bf(p["w_br_a"][l]), "w_br_b": _bf(p["w_br_b"][l]), "w_br_c": _bf(p["w_br_c"][l]),
        "w_out": _bf(p["w_out"][l]),
        "norm_ffn": p["norm_ffn"][l].reshape(1, D_MODEL),
        "peer_wq": _bf(p["peer_wq"][l]),
        "peer_sk": _bf(p["peer_subkeys"][l].reshape(PEER_HEADS * 2, PEER_KEYS, PEER_QDIM // 2)),
        "peer_u": _bf(p["peer_u"][l]),
        "peer_vt": _bf(p["peer_v"][l]).T,
    }


def _project(h, lw):
    nm = lw["norm_mix"]
    return {s: _norm_matmul(h, nm, lw["w_" + s]) for s in ("a", "b", "bg", "c", "g")}


def _kv_rows(pa, gi, lo, hi):
    k = pa[lo:hi, ATT_WIDTH + gi * ATT_OUT:ATT_WIDTH + (gi + 1) * ATT_OUT]
    v = pa[lo:hi, 2 * ATT_WIDTH + gi * ATT_OUT:2 * ATT_WIDTH + (gi + 1) * ATT_OUT]
    return jnp.stack([k, v], axis=1).reshape(hi - lo, 2, ATT_HPG, ATT_HD)


def _prompt_layer(h, lw):
    t = h.shape[0]
    assert t % ATT_TILE == 0 and t % SEQ_TILE == 0
    pr = _project(h, lw)
    groups = [_attn_prompt_group(pr["a"], t, gi) for gi in range(len(ATT_GROUPS))]
    ob, dn_s = _dn_mixer(pr["b"], pr["bg"], jnp.zeros((1, V7X_SUBLANES, B_QKV), f32),
                         jnp.zeros((1, DN_HEADS, DN_HD, DN_HD), f32), lw,
                         nseq=1, tt=SEQ_TILE, chunk=DN_CHUNK, n_valid=SEQ_TILE)
    oc, rw_s = _rw_mixer(pr["c"], jnp.zeros((1, 1, C_COLS), f32),
                         jnp.zeros((1, RW_HEADS, RW_HD, RW_HD), f32), lw,
                         nseq=1, tt=SEQ_TILE, chunk=RW_CHUNK, n_valid=SEQ_TILE)
    h = _merge(h, [g[0] for g in groups], [g[1] for g in groups], ob, oc, pr["g"], lw)
    h = _peer(h, lw)
    kvs = [_kv_rows(pr["a"], gi, t - min(win, t), t)[None] for gi, (win, _) in enumerate(ATT_GROUPS)]
    conv = pr["b"][t - (DN_CONV - 1):t, 0:B_QKV][None]
    shift = pr["c"][t - 1:t]
    return h, kvs, dn_s, conv, rw_s, shift


def _pad_rows(x, nseq, seq):
    n = x.shape[1]
    return jnp.pad(x.reshape(nseq, seq, n), ((0, 0), (0, SAMPLE_ROWS - seq), (0, 0))).reshape(nseq * SAMPLE_ROWS, n)


def _unpad_rows(x, nseq, seq):
    return x.reshape(nseq, SAMPLE_ROWS, x.shape[1])[:, 0:seq].reshape(nseq * seq, x.shape[1])


def _sample_layer(h, lw, layer, caches, dn_state, dn_conv, rw_state, rw_shift, nseq, seq):
    pr = _project(h, lw)
    oa = _attn_sample(pr["a"], caches, layer, nseq, seq)
    conv0 = jnp.pad(dn_conv, ((0, 0), (V7X_SUBLANES - (DN_CONV - 1), 0), (0, 0)))
    ob, dn_s = _dn_mixer(_pad_rows(pr["b"], nseq, seq), _pad_rows(pr["bg"], nseq, seq), conv0, dn_state, lw,
                         nseq=nseq, tt=SAMPLE_ROWS, chunk=SAMPLE_ROWS, n_valid=seq)
    oc, rw_s = _rw_mixer(_pad_rows(pr["c"], nseq, seq), rw_shift[:, None, :], rw_state, lw,
                         nseq=nseq, tt=SAMPLE_ROWS, chunk=SAMPLE_ROWS, n_valid=seq)
    h = _merge(h, oa, None, _unpad_rows(ob, nseq, seq), _unpad_rows(oc, nseq, seq), pr["g"], lw)
    h = _peer(h, lw)
    kvs = [_kv_rows(pr["a"], gi, 0, nseq * seq).reshape(nseq, seq, 2, ATT_HPG, ATT_HD)
           for gi in range(len(ATT_GROUPS))]
    conv = pr["b"][:, 0:B_QKV].reshape(nseq, seq, B_QKV)[:, seq - (DN_CONV - 1):]
    shift = pr["c"].reshape(nseq, seq, C_COLS)[:, seq - 1]
    return h, kvs, dn_s, conv, rw_s, shift


def kernel(x_prompt, x_sample, cache_kv_w128, cache_kv_w512, cache_kv_w2048, state_dn, state_dn_conv, state_rw, state_rw_shift, norm_mix, w_in, dn_conv_w, dn_a_log, dn_dt_bias, dn_norm_w, rw_mu, rw_w0, rw_w2, rw_a0, rw_a2, rw_g2, rw_k_k, rw_k_a, rw_r_k, rw_ln_w, rw_ln_b, w_br_a, w_br_b, w_br_c, w_out, norm_ffn, peer_wq, peer_subkeys, peer_u, peer_v, norm_final):
    p = dict(norm_mix=norm_mix, w_in=w_in, dn_conv_w=dn_conv_w, dn_a_log=dn_a_log, dn_dt_bias=dn_dt_bias,
             dn_norm_w=dn_norm_w, rw_mu=rw_mu, rw_w0=rw_w0, rw_w2=rw_w2, rw_a0=rw_a0, rw_a2=rw_a2, rw_g2=rw_g2,
             rw_k_k=rw_k_k, rw_k_a=rw_k_a, rw_r_k=rw_r_k, rw_ln_w=rw_ln_w, rw_ln_b=rw_ln_b, w_br_a=w_br_a,
             w_br_b=w_br_b, w_br_c=w_br_c, w_out=w_out, norm_ffn=norm_ffn, peer_wq=peer_wq,
             peer_subkeys=peer_subkeys, peer_u=peer_u, peer_v=peer_v)
    depth = w_in.shape[0]
    bp, t, d = x_prompt.shape
    nseq, seq, _ = x_sample.shape
    assert bp == 1 and d == D_MODEL and seq <= SAMPLE_ROWS // 2 and DN_CONV - 1 <= seq
    caches = [c.reshape(depth, nseq, c.shape[2], 2 * ATT_OUT) for c in (cache_kv_w128, cache_kv_w512, cache_kv_w2048)]
    hp = x_prompt.reshape(t, d)
    hs = x_sample.reshape(nseq * seq, d)
    outs_p, outs_s = [], []
    for l in range(depth):
        lw = _layer_weights(l, p)
        hp, *st_p = _prompt_layer(hp, lw)
        hs, *st_s = _sample_layer(hs, lw, l, caches, state_dn[l], state_dn_conv[l], state_rw[l],
                                  state_rw_shift[l], nseq, seq)
        outs_p.append(st_p)
        outs_s.append(st_s)
    y_p = _rmsnorm(hp, norm_final).reshape(bp, t, d)
    y_s = _rmsnorm(hs, norm_final).reshape(nseq, seq, d)

    def stack(outs, pick):
        return jnp.stack([pick(o) for o in outs], axis=0)

    res = [y_p, y_s]
    for gi in range(len(ATT_GROUPS)):
        res.append(stack(outs_p, lambda o: o[0][gi]))
        res.append(stack(outs_s, lambda o: o[0][gi]))
    for idx in (1, 2, 3, 4):
        res.append(stack(outs_p, lambda o: o[idx]))
        res.append(stack(outs_s, lambda o: o[idx]))
    return tuple(res)
```

```python
import functools
import math

import jax
import jax.numpy as jnp
import numpy as np
from jax import lax
from jax.experimental import pallas as pl
from jax.experimental.pallas import tpu as pltpu

f32 = jnp.float32
bf16 = jnp.bfloat16

V7X_LANES = 128
V7X_SUBLANES = 8
V7X_VMEM_BYTES = 64 * 1024 * 1024
VMEM_CEILING = 56 * 1024 * 1024

D_MODEL = 1024
ATT_GROUPS = ((128, 1), (512, 4), (2048, 16))
ATT_HPG = 4
ATT_HD = 64
ATT_HEADS = ATT_HPG * len(ATT_GROUPS)
ATT_WIDTH = ATT_HEADS * ATT_HD
ATT_OUT = ATT_HPG * ATT_HD
ATT_STEPS = 128
DN_HEADS = 4
DN_HD = 128
DN_WIDTH = DN_HEADS * DN_HD
DN_CONV = 4
DN_CHUNK = 64
DN_GROUP = 4
RW_HEADS = 8
RW_HD = 64
RW_WIDTH = RW_HEADS * RW_HD
RW_DECAY_LORA = 64
RW_A_LORA = 64
RW_GATE_LORA = 128
RW_GN_EPS = 64e-5
RW_CHUNK = 64
RW_GROUP = 2
PEER_KEYS = 128
PEER_EXPERTS = PEER_KEYS * PEER_KEYS
PEER_HEADS = 8
PEER_QDIM = 128
PEER_TOPK = 16
RMS_EPS = 1e-6
L2_EPS = 1e-6

A_COLS = 3 * ATT_WIDTH
B_QKV = 3 * DN_WIDTH
B_MAIN = 4 * DN_WIDTH
B_COLS = B_MAIN + 2 * DN_HEADS
C_COLS = 3 * RW_WIDTH + RW_DECAY_LORA + RW_A_LORA + RW_GATE_LORA
G_COLS = 3 * D_MODEL

ROW_TILE = 512
ATT_TILE = 2048
SEQ_TILE = 512
SAMPLE_ROWS = 8
PEER_TOKENS = 512
PEER_EBLK = 1024

NEG_INF = float("-inf")


def _alibi_slopes():
    h = np.arange(1, ATT_HEADS + 1, dtype=np.float32)
    return np.power(np.float32(2.0), -8.0 * h / ATT_HEADS).astype(np.float32)


def _cparams(semantics, vmem_bytes):
    return pltpu.CompilerParams(dimension_semantics=semantics,
                                vmem_limit_bytes=int(min(max(vmem_bytes, 16 * 1024 * 1024), VMEM_CEILING)))


def _dot(a, b):
    return jnp.dot(a, b, preferred_element_type=f32)


def _dot_nt(a, b):
    return lax.dot_general(a, b, (((1,), (1,)), ((), ())), preferred_element_type=f32)


def _dot_tn(a, b):
    return lax.dot_general(a, b, (((0,), (0,)), ((), ())), preferred_element_type=f32)


def _bf(x):
    return x.astype(bf16)


def _sigmoid(x):
    return 1.0 / (1.0 + jnp.exp(-x))


def _segsum(x, bd):
    hi = _bf(x)
    lo = _bf(x - hi.astype(f32))
    return _dot(hi, bd) + _dot(lo, bd)


def _seg_cumsum(x, rowc, seg):
    k = 1
    while k < seg:
        x = x + jnp.where(rowc >= k, pltpu.roll(x, k, 0), 0.0)
        k *= 2
    return x


def _block_diag_ones(width, seg):
    i = np.arange(width)
    return jnp.asarray((i[:, None] // seg) == (i[None, :] // seg), dtype=bf16)


def _norm_matmul_kernel(h_ref, nw_ref, w_ref, o_ref):
    x = h_ref[...]
    xn = x * lax.rsqrt(jnp.mean(x * x, axis=-1, keepdims=True) + RMS_EPS) * nw_ref[...]
    o_ref[...] = _dot(_bf(xn), w_ref[...])


def _norm_matmul(h, nw, w):
    rows, d = h.shape
    n = w.shape[1]
    tr = min(ROW_TILE, rows)
    vmem = 2 * (tr * d * 4 + d * n * 2 + tr * n * 4) + 4 * tr * d * 4
    return pl.pallas_call(
        _norm_matmul_kernel,
        out_shape=jax.ShapeDtypeStruct((rows, n), f32),
        grid=(rows // tr,),
        in_specs=[pl.BlockSpec((tr, d), lambda i: (i, 0)),
                  pl.BlockSpec((1, d), lambda i: (0, 0)),
                  pl.BlockSpec((d, n), lambda i: (0, 0))],
        out_specs=pl.BlockSpec((tr, n), lambda i: (i, 0)),
        compiler_params=_cparams(("parallel",), vmem),
        name="norm_matmul",
    )(h, nw.reshape(1, d), w)


def _rmsnorm_kernel(h_ref, nw_ref, o_ref):
    x = h_ref[...]
    o_ref[...] = x * lax.rsqrt(jnp.mean(x * x, axis=-1, keepdims=True) + RMS_EPS) * nw_ref[...]


def _rmsnorm(h, nw):
    rows, d = h.shape
    tr = min(ROW_TILE, rows)
    return pl.pallas_call(
        _rmsnorm_kernel,
        out_shape=jax.ShapeDtypeStruct((rows, d), f32),
        grid=(rows // tr,),
        in_specs=[pl.BlockSpec((tr, d), lambda i: (i, 0)), pl.BlockSpec((1, d), lambda i: (0, 0))],
        out_specs=pl.BlockSpec((tr, d), lambda i: (i, 0)),
        compiler_params=_cparams(("parallel",), 6 * tr * d * 4),
        name="final_rmsnorm",
    )(h, nw.reshape(1, d))


def _attn_prompt_kernel(q_ref, kc_ref, vc_ref, kp_ref, vp_ref, o_ref, lse_ref, kk_scr, vv_scr, *, dil, slopes):
    n = ATT_STEPS
    nblk = ATT_TILE // (dil * n)
    tile = pl.program_id(0)
    pair = pl.program_id(1)
    kk_scr[0:ATT_TILE, :] = kp_ref[...]
    kk_scr[ATT_TILE:2 * ATT_TILE, :] = kc_ref[...]
    vv_scr[0:ATT_TILE, :] = vp_ref[...]
    vv_scr[ATT_TILE:2 * ATT_TILE, :] = vc_ref[...]
    ii = lax.broadcasted_iota(jnp.int32, (n, 2 * n), 0)
    jj = lax.broadcasted_iota(jnp.int32, (n, 2 * n), 1)
    steps = n + ii - jj
    band = (steps >= 0) & (steps <= n)
    dist = (steps * dil).astype(f32)
    biases = []
    for hh in range(2):
        slope = jnp.where(pair == 0, slopes[hh], slopes[2 + hh])
        biases.append(jnp.where(band, -slope * dist, NEG_INF))

    def body(c, carry):
        r = c % dil
        b = c // dil
        qs = r + dil * n * b
        rows_q = pl.ds(qs, n, stride=dil)
        rows_k = pl.ds(ATT_TILE + qs - dil * n, 2 * n, stride=dil)
        q = q_ref[rows_q, :] * (ATT_HD ** -0.5)
        k = kk_scr[rows_k, :]
        v = vv_scr[rows_k, :]
        first_key = jnp.where(tile * nblk + b == 0, n, 0)
        outs, lses = [], []
        for hh in range(2):
            cols = slice(hh * ATT_HD, (hh + 1) * ATT_HD)
            s = _dot_nt(_bf(q[:, cols]), _bf(k[:, cols])) + biases[hh]
            s = jnp.where(jj >= first_key, s, NEG_INF)
            mx = jnp.max(s, axis=-1, keepdims=True)
            p = jnp.exp(s - mx)
            den = jnp.sum(p, axis=-1, keepdims=True)
            outs.append(_dot(_bf(p), _bf(v[:, cols])) / den)
            lses.append(jnp.broadcast_to(mx + jnp.log(den), (n, ATT_HD)))
        o_ref[rows_q, :] = jnp.concatenate(outs, axis=1)
        lse_ref[rows_q, :] = jnp.concatenate(lses, axis=1)
        return carry

    lax.fori_loop(0, dil * nblk, body, 0)


def _attn_prompt_group(pa, seq_len, gi):
    _, dil = ATT_GROUPS[gi]
    slopes = tuple(float(s) for s in _alibi_slopes()[gi * ATT_HPG:(gi + 1) * ATT_HPG])
    w = V7X_LANES
    qb, kb, vb = (gi * 2, 6 + gi * 2, 12 + gi * 2)
    blk = (ATT_TILE, w)
    cur = lambda base: pl.BlockSpec(blk, lambda i, j, base=base: (i, base + j))
    prev = lambda base: pl.BlockSpec(blk, lambda i, j, base=base: (jnp.maximum(i - 1, 0), base + j))
    vmem = 2 * 7 * ATT_TILE * w * 4 + 2 * 2 * ATT_TILE * w * 4 + 8 * 1024 * 1024
    return pl.pallas_call(
        functools.partial(_attn_prompt_kernel, dil=dil, slopes=slopes),
        out_shape=(jax.ShapeDtypeStruct((seq_len, ATT_OUT), f32), jax.ShapeDtypeStruct((seq_len, ATT_OUT), f32)),
        grid=(seq_len // ATT_TILE, 2),
        in_specs=[cur(qb), cur(kb), cur(vb), prev(kb), prev(vb)],
        out_specs=(pl.BlockSpec(blk, lambda i, j: (i, j)), pl.BlockSpec(blk, lambda i, j: (i, j))),
        scratch_shapes=[pltpu.VMEM((2 * ATT_TILE, w), f32), pltpu.VMEM((2 * ATT_TILE, w), f32)],
        compiler_params=_cparams(("parallel", "parallel"), vmem),
        name=f"attn_prompt_g{gi}",
    )(pa, pa, pa, pa, pa)


def _attn_sample_kernel(q_ref, k_ref, v_ref, c0_ref, c1_ref, c2_ref, o_ref, *, slopes, seq):
    rows = 2 * seq
    caches = (c0_ref, c1_ref, c2_ref)
    row = lax.broadcasted_iota(jnp.int32, (rows, 1), 0)
    own0 = row < seq
    qpos = row % seq
    npos = lax.broadcasted_iota(jnp.int32, (rows, rows), 1)
    q_all = q_ref[0] * (ATT_HD ** -0.5)
    k_all = k_ref[0]
    v_all = v_ref[0]
    outs, lses = [], []
    for gi, (win, dil) in enumerate(ATT_GROUPS):
        cpos = lax.broadcasted_iota(jnp.int32, (rows, win), 1)
        dist_c = win + qpos - cpos
        ok_c = (dist_c <= win) & ((dist_c & (dil - 1)) == 0)
        dist_n = row - npos
        ok_n = (dist_n >= 0) & (dist_n <= qpos) & ((dist_n & (dil - 1)) == 0)
        dcf = dist_c.astype(f32)
        dnf = dist_n.astype(f32)
        for h in range(ATT_HPG):
            slope = slopes[gi * ATT_HPG + h]
            cols = slice(gi * ATT_OUT + h * ATT_HD, gi * ATT_OUT + (h + 1) * ATT_HD)
            kcols = slice(h * ATT_HD, (h + 1) * ATT_HD)
            vcols = slice(ATT_OUT + h * ATT_HD, ATT_OUT + (h + 1) * ATT_HD)
            qh = _bf(q_all[:, cols])
            sn = jnp.where(ok_n, _dot_nt(qh, _bf(k_all[:, cols])) - slope * dnf, NEG_INF)
            scs = []
            for b in range(2):
                sc = _dot_nt(qh, _bf(caches[gi][0, b, :, kcols])) - slope * dcf
                scs.append(jnp.where(ok_c, sc, NEG_INF))
            sc = jnp.where(own0, scs[0], scs[1])
            mx = jnp.maximum(jnp.max(sc, axis=-1, keepdims=True), jnp.max(sn, axis=-1, keepdims=True))
            pc = jnp.exp(sc - mx)
            pn = jnp.exp(sn - mx)
            den = jnp.sum(pc, axis=-1, keepdims=True) + jnp.sum(pn, axis=-1, keepdims=True)
            pcb = _bf(pc)
            oc = jnp.where(own0, _dot(pcb, _bf(caches[gi][0, 0, :, vcols])), _dot(pcb, _bf(caches[gi][0, 1, :, vcols])))
            o = (oc + _dot(_bf(pn), _bf(v_all[:, cols]))) / den
            outs.append(o)
            lses.append(mx + jnp.log(den))
    merged = []
    for h in range(ATT_HPG):
        ls = [lses[gi * ATT_HPG + h] for gi in range(len(ATT_GROUPS))]
        mx = jnp.maximum(jnp.maximum(ls[0], ls[1]), ls[2])
        es = [jnp.exp(l - mx) for l in ls]
        den = es[0] + es[1] + es[2]
        acc = es[0] * outs[h] + es[1] * outs[ATT_HPG + h] + es[2] * outs[2 * ATT_HPG + h]
        merged.append(acc / den)
    o_ref[0] = jnp.concatenate(merged, axis=1)


def _attn_sample(pa, caches, layer, nseq, seq):
    rows = 2 * seq
    assert rows == SAMPLE_ROWS and nseq % 2 == 0
    slopes = tuple(float(s) for s in _alibi_slopes())
    pa3 = pa.reshape(nseq // 2, rows, A_COLS)
    in_specs = [pl.BlockSpec((1, rows, ATT_WIDTH), lambda i, c=c: (i, 0, c)) for c in range(3)]
    vmem = 0
    for (win, _), c in zip(ATT_GROUPS, caches):
        assert c.shape[2] == win, "cached window must hold exactly `window` rows"
        in_specs.append(pl.BlockSpec((1, 2, win, 2 * ATT_OUT), lambda i: (layer, i, 0, 0)))
        vmem += 2 * 2 * win * 2 * ATT_OUT * 4
    out = pl.pallas_call(
        functools.partial(_attn_sample_kernel, slopes=slopes, seq=seq),
        out_shape=jax.ShapeDtypeStruct((nseq // 2, rows, ATT_OUT), f32),
        grid=(nseq // 2,),
        in_specs=in_specs,
        out_specs=pl.BlockSpec((1, rows, ATT_OUT), lambda i: (i, 0, 0)),
        compiler_params=_cparams(("parallel",), vmem + 16 * 1024 * 1024),
        name="attn_sample",
    )(pa3, pa3, pa3, *caches)
    return out.reshape(nseq * seq, ATT_OUT)


def _dn_kernel(pb_ref, pbg_ref, conv0_ref, s0_ref, cw_ref, prm_ref, nw_ref, bd_ref,
               o_ref, sout_ref,
               xs_scr, q_scr, k_scr, v_scr, g_scr, b_scr, oc_scr, qp_scr, m_scr, n_scr, s_scr,
               *, tt, chunk, n_valid):
    t = pl.program_id(1)

    @pl.when(t == 0)
    def _():
        xs_scr[0:V7X_SUBLANES, :] = conv0_ref[0]
        s_scr[...] = s0_ref[0]

    x = pb_ref[:, 0:B_QKV]
    xs_scr[pl.ds(V7X_SUBLANES, tt), :] = x
    cw = cw_ref[...]
    y = x * cw[3:4, :]
    for j in range(DN_CONV - 1):
        y = y + xs_scr[pl.ds(V7X_SUBLANES - (DN_CONV - 1) + j, tt), :] * cw[j:j + 1, :]
    xs_scr[0:V7X_SUBLANES, :] = xs_scr[pl.ds(tt, V7X_SUBLANES), :]
    y = y * _sigmoid(y)
    bd = bd_ref[...]
    q = y[:, 0:DN_WIDTH]
    k = y[:, DN_WIDTH:2 * DN_WIDTH]
    q_scr[...] = q * lax.rsqrt(_segsum(q * q, bd) + L2_EPS) * (DN_HD ** -0.5)
    k_scr[...] = k * lax.rsqrt(_segsum(k * k, bd) + L2_EPS)
    v_scr[...] = y[:, 2 * DN_WIDTH:3 * DN_WIDTH]
    pg = pbg_ref[...]
    beta = _sigmoid(pg)
    g = -jnp.exp(prm_ref[0:1, :]) * jax.nn.softplus(pg + prm_ref[1:2, :])
    rown = lax.broadcasted_iota(jnp.int32, (tt, V7X_LANES), 0)
    if n_valid < tt:
        beta = jnp.where(rown < n_valid, beta, 0.0)
        g = jnp.where(rown < n_valid, g, 0.0)
    g_scr[...] = _seg_cumsum(g, rown & (chunk - 1), chunk)
    b_scr[...] = beta

    ci = lax.broadcasted_iota(jnp.int32, (chunk, chunk), 0)
    cj = lax.broadcasted_iota(jnp.int32, (chunk, chunk), 1)
    incl = ci >= cj
    strict = ci > cj

    nchunks = tt // chunk
    group = min(DN_GROUP, nchunks)

    def precompute(gi, carry):
        items = []
        for cc in range(group):
            cidx = gi * group + cc
            rows = pl.ds(pl.multiple_of(cidx * chunk, chunk), chunk)
            gc = g_scr[rows, :]
            bc = b_scr[rows, :]
            gct = gc.T
            for h in range(DN_HEADS):
                cols = slice(h * DN_HD, (h + 1) * DN_HD)
                qh, kh, vh = q_scr[rows, cols], k_scr[rows, cols], v_scr[rows, cols]
                gcol = gc[:, DN_HEADS + h:DN_HEADS + h + 1]
                grow = gct[DN_HEADS + h:DN_HEADS + h + 1, :]
                bcol = bc[:, h:h + 1]
                glast = gc[chunk - 1:chunk, DN_HEADS + h:DN_HEADS + h + 1]
                decay = jnp.exp(jnp.where(incl, gcol - grow, NEG_INF))
                kb = kh * bcol
                eg = jnp.exp(gcol)
                both = _dot_nt(_bf(jnp.concatenate([kb, qh], axis=0)), _bf(kh))
                items.append(dict(
                    rows=rows, cols=cols, idx=cidx * DN_HEADS + h,
                    p=jnp.where(strict, -both[0:chunk] * decay, 0.0),
                    attn=jnp.where(incl, both[chunk:2 * chunk] * decay, 0.0),
                    sol=jnp.concatenate([vh * bcol, kb * eg], axis=1),
                    qeg=qh * eg, kdec=_bf(kh * jnp.exp(glast - gcol))))
        k = 1
        while k < chunk:
            for it in items:
                it["sol"] = it["sol"] + _dot(_bf(it["p"]), _bf(it["sol"]))
            if 2 * k < chunk:
                for it in items:
                    pb = _bf(it["p"])
                    it["p"] = _dot(pb, pb)
            k *= 2
        for it in items:
            u = _bf(it["sol"][:, 0:DN_HD])
            w = _bf(it["sol"][:, DN_HD:2 * DN_HD])
            attn = _bf(it["attn"])
            qp_scr[it["rows"], it["cols"]] = it["qeg"] - _dot(attn, w)
            oc_scr[it["rows"], it["cols"]] = _dot(attn, u)
            m_scr[it["idx"]] = _dot_tn(it["kdec"], w)
            n_scr[it["idx"]] = _dot_tn(it["kdec"], u)
        return carry

    lax.fori_loop(0, nchunks // group, precompute, 0)

    def recur(c, carry):
        rows = pl.ds(pl.multiple_of(c * chunk, chunk), chunk)
        g_tail = g_scr[pl.ds(pl.multiple_of((c + 1) * chunk - V7X_SUBLANES, V7X_SUBLANES), V7X_SUBLANES), :]
        states = [s_scr[h] for h in range(DN_HEADS)]
        sb = [_bf(s) for s in states]
        outs = [_dot(_bf(qp_scr[rows, h * DN_HD:(h + 1) * DN_HD]), sb[h]) for h in range(DN_HEADS)]
        upd = [_dot(_bf(m_scr[c * DN_HEADS + h]), sb[h]) for h in range(DN_HEADS)]
        for h in range(DN_HEADS):
            cols = slice(h * DN_HD, (h + 1) * DN_HD)
            oc_scr[rows, cols] = oc_scr[rows, cols] + outs[h]
            g_end = jnp.exp(g_tail[V7X_SUBLANES - 1:V7X_SUBLANES, DN_HEADS + h:DN_HEADS + h + 1])
            s_scr[h] = states[h] * g_end - upd[h] + n_scr[c * DN_HEADS + h]
        return carry

    lax.fori_loop(0, nchunks, recur, 0)

    o = oc_scr[...]
    z = pb_ref[:, B_QKV:B_MAIN]
    o = o * lax.rsqrt(_segsum(o * o, bd) * (1.0 / DN_HD) + RMS_EPS) * nw_ref[...]
    o_ref[...] = o * (z * _sigmoid(z))

    @pl.when(t == pl.num_programs(1) - 1)
    def _():
        sout_ref[0] = s_scr[...]


def _dn_mixer(pb, pbg, conv0, s0, lw, *, nseq, tt, chunk, n_valid):
    rows = pb.shape[0]
    nt = rows // (nseq * tt)
    nmat = (tt // chunk) * DN_HEADS
    vmem = (2 * (tt * (B_MAIN + V7X_LANES + DN_WIDTH) * 4 + DN_WIDTH * DN_WIDTH * 2)
            + (tt + 8) * B_QKV * 4 + 4 * tt * DN_WIDTH * 4 + 2 * tt * V7X_LANES * 4
            + tt * DN_WIDTH * 4 + (2 * nmat + 6 * DN_HEADS) * DN_HD * DN_HD * 4 + 8 * tt * B_QKV * 4)
    return pl.pallas_call(
        functools.partial(_dn_kernel, tt=tt, chunk=chunk, n_valid=n_valid),
        out_shape=(jax.ShapeDtypeStruct((rows, DN_WIDTH), f32),
                   jax.ShapeDtypeStruct((nseq, DN_HEADS, DN_HD, DN_HD), f32)),
        grid=(nseq, nt),
        in_specs=[pl.BlockSpec((tt, B_MAIN), lambda s, t: (s * nt + t, 0)),
                  pl.BlockSpec((tt, V7X_LANES), lambda s, t: (s * nt + t, 0)),
                  pl.BlockSpec((1, V7X_SUBLANES, B_QKV), lambda s, t: (s, 0, 0)),
                  pl.BlockSpec((1, DN_HEADS, DN_HD, DN_HD), lambda s, t: (s, 0, 0, 0)),
                  pl.BlockSpec((DN_CONV, B_QKV), lambda s, t: (0, 0)),
                  pl.BlockSpec((V7X_SUBLANES, V7X_LANES), lambda s, t: (0, 0)),
                  pl.BlockSpec((1, DN_WIDTH), lambda s, t: (0, 0)),
                  pl.BlockSpec((DN_WIDTH, DN_WIDTH), lambda s, t: (0, 0))],
        out_specs=(pl.BlockSpec((tt, DN_WIDTH), lambda s, t: (s * nt + t, 0)),
                   pl.BlockSpec((1, DN_HEADS, DN_HD, DN_HD), lambda s, t: (s, 0, 0, 0))),
        scratch_shapes=[pltpu.VMEM((tt + V7X_SUBLANES, B_QKV), f32),
                        pltpu.VMEM((tt, DN_WIDTH), f32), pltpu.VMEM((tt, DN_WIDTH), f32),
                        pltpu.VMEM((tt, DN_WIDTH), f32),
                        pltpu.VMEM((tt, V7X_LANES), f32), pltpu.VMEM((tt, V7X_LANES), f32),
                        pltpu.VMEM((tt, DN_WIDTH), f32), pltpu.VMEM((tt, DN_WIDTH), f32),
                        pltpu.VMEM((nmat, DN_HD, DN_HD), f32), pltpu.VMEM((nmat, DN_HD, DN_HD), f32),
                        pltpu.VMEM((DN_HEADS, DN_HD, DN_HD), f32)],
        compiler_params=_cparams(("parallel", "arbitrary"), vmem),
        name="deltanet",
    )(pb, pbg, conv0, s0, lw["dn_conv_w"], lw["dn_prm"], lw["dn_norm_w"], lw["bd128"])


def _rw_kernel(pc_ref, shift0_ref, s0_ref, mu_ref, vec_ref, w2_ref, a2_ref, g2_ref, bd_ref,
               o_ref, sout_ref,
               xs_scr, ar_scr, bk_scr, v_scr, pe_scr, oc_scr, rp_scr, m_scr, n_scr, s_scr, *, tt, chunk, n_valid):
    t = pl.program_id(1)

    @pl.when(t == 0)
    def _():
        xs_scr[V7X_SUBLANES - 1:V7X_SUBLANES, :] = shift0_ref[0]
        s_scr[...] = s0_ref[0]

    pc = pc_ref[...]
    xs_scr[pl.ds(V7X_SUBLANES, tt), :] = pc
    prev = xs_scr[pl.ds(V7X_SUBLANES - 1, tt), :]
    xs_scr[0:V7X_SUBLANES, :] = xs_scr[pl.ds(tt, V7X_SUBLANES), :]
    xc = pc + (prev - pc) * mu_ref[...]
    w3 = 3 * RW_WIDTH
    r = xc[:, 0:RW_WIDTH]
    k = xc[:, RW_WIDTH:2 * RW_WIDTH]
    v = xc[:, 2 * RW_WIDTH:w3]
    wd = xc[:, w3:w3 + RW_DECAY_LORA]
    ad = xc[:, w3 + RW_DECAY_LORA:w3 + RW_DECAY_LORA + RW_A_LORA]
    gd = xc[:, w3 + RW_DECAY_LORA + RW_A_LORA:C_COLS]
    w0, a0, k_k, k_a, r_k, ln_w, ln_b = (vec_ref[i:i + 1, :] for i in range(7))
    w_log = -jax.nn.softplus(-(w0 + _dot(_bf(jnp.tanh(wd)), w2_ref[...]))) - 0.5
    lw = -jnp.exp(w_log)
    a = _sigmoid(a0 + _dot(_bf(ad), a2_ref[...]))
    gate = _dot(_bf(_sigmoid(gd)), g2_ref[...])
    bd = bd_ref[...]
    kkr = k * k_k
    kk = kkr * lax.rsqrt(_segsum(kkr * kkr, bd) + L2_EPS)
    k2 = k * (1.0 + (a - 1.0) * k_a)
    bonus = _segsum(r * k2 * r_k, bd) * v
    rown = lax.broadcasted_iota(jnp.int32, (tt, RW_WIDTH), 0)
    if n_valid < tt:
        pad = rown >= n_valid
        lw = jnp.where(pad, 0.0, lw)
        kk = jnp.where(pad, 0.0, kk)
        k2 = jnp.where(pad, 0.0, k2)
    gcum = _seg_cumsum(lw, rown & (chunk - 1), chunk)
    e_pos = jnp.exp(gcum)
    e_neg = jnp.exp(-gcum)
    ar_scr[:, 0:RW_WIDTH] = -kk * jnp.exp(gcum - lw)
    ar_scr[:, RW_WIDTH:2 * RW_WIDTH] = r * e_pos
    bk_scr[:, 0:RW_WIDTH] = kk * a * e_neg
    bk_scr[:, RW_WIDTH:2 * RW_WIDTH] = k2 * e_neg
    v_scr[...] = v
    pe_scr[...] = e_pos

    ci = lax.broadcasted_iota(jnp.int32, (2 * chunk, 2 * chunk), 0)
    cj = lax.broadcasted_iota(jnp.int32, (2 * chunk, 2 * chunk), 1)
    keep = jnp.where(ci < chunk, ci - 1, ci - chunk) >= (cj & (chunk - 1))

    nchunks = tt // chunk
    group = min(RW_GROUP, nchunks)

    def precompute(gi, carry):
        items = []
        for cc in range(group):
            cidx = gi * group + cc
            rows = pl.ds(pl.multiple_of(cidx * chunk, chunk), chunk)
            p_tail = pe_scr[pl.ds(pl.multiple_of((cidx + 1) * chunk - V7X_SUBLANES, V7X_SUBLANES), V7X_SUBLANES), :]
            for h in range(RW_HEADS):
                cols = slice(h * RW_HD, (h + 1) * RW_HD)
                cols2 = slice(RW_WIDTH + h * RW_HD, RW_WIDTH + (h + 1) * RW_HD)
                at, rt = ar_scr[rows, cols], ar_scr[rows, cols2]
                bt, kt = bk_scr[rows, cols], bk_scr[rows, cols2]
                bkb = _bf(jnp.concatenate([bt, kt], axis=0))
                m = jnp.where(keep, _dot_nt(_bf(jnp.concatenate([at, rt], axis=0)), bkb), 0.0)
                items.append(dict(rows=rows, cols=cols, idx=cidx * RW_HEADS + h, at=at, rt=rt, bt=_bf(bt), bkb=bkb,
                                  vh=v_scr[rows, cols], p=m[0:chunk, 0:chunk], aak=_bf(m[0:chunk, chunk:2 * chunk]),
                                  mlow=_bf(m[chunk:2 * chunk, :]), p_end=p_tail[V7X_SUBLANES - 1:V7X_SUBLANES, cols]))
        for it in items:
            it["sol"] = jnp.concatenate([it["at"], _dot(it["aak"], _bf(it["vh"]))], axis=1)
        k = 1
        while k < chunk:
            for it in items:
                it["sol"] = it["sol"] + _dot(_bf(it["p"]), _bf(it["sol"]))
            if 2 * k < chunk:
                for it in items:
                    pb = _bf(it["p"])
                    it["p"] = _dot(pb, pb)
            k *= 2
        for it in items:
            wm = _bf(it["sol"][:, 0:RW_HD])
            uv = _bf(jnp.concatenate([it["sol"][:, RW_HD:2 * RW_HD], it["vh"]], axis=0))
            rp_scr[it["rows"], it["cols"]] = it["rt"] + _dot(it["mlow"][:, 0:chunk], wm)
            oc_scr[it["rows"], it["cols"]] = _dot(it["mlow"], uv)
            m_scr[it["idx"]] = _dot_tn(wm, it["bt"]) * it["p_end"]
            n_scr[it["idx"]] = _dot_tn(uv, it["bkb"]) * it["p_end"]
        return carry

    lax.fori_loop(0, nchunks // group, precompute, 0)

    def recur(c, carry):
        rows = pl.ds(pl.multiple_of(c * chunk, chunk), chunk)
        p_tail = pe_scr[pl.ds(pl.multiple_of((c + 1) * chunk - V7X_SUBLANES, V7X_SUBLANES), V7X_SUBLANES), :]
        states = [s_scr[h] for h in range(RW_HEADS)]
        sb = [_bf(s) for s in states]
        outs = [_dot_nt(_bf(rp_scr[rows, h * RW_HD:(h + 1) * RW_HD]), sb[h]) for h in range(RW_HEADS)]
        upd = [_dot(sb[h], _bf(m_scr[c * RW_HEADS + h])) for h in range(RW_HEADS)]
        for h in range(RW_HEADS):
            cols = slice(h * RW_HD, (h + 1) * RW_HD)
            oc_scr[rows, cols] = oc_scr[rows, cols] + outs[h]
            s_scr[h] = states[h] * p_tail[V7X_SUBLANES - 1:V7X_SUBLANES, cols] + upd[h] + n_scr[c * RW_HEADS + h]
        return carry

    lax.fori_loop(0, nchunks, recur, 0)

    o = oc_scr[...]
    mean = _segsum(o, bd) * (1.0 / RW_HD)
    d = o - mean
    var = _segsum(d * d, bd) * (1.0 / RW_HD)
    o = d * lax.rsqrt(var + RW_GN_EPS) * ln_w + ln_b
    o_ref[...] = (o + bonus) * gate

    @pl.when(t == pl.num_programs(1) - 1)
    def _():
        sout_ref[0] = s_scr[...]


def _rw_mixer(pc, shift0, s0, lw, *, nseq, tt, chunk, n_valid):
    rows = pc.shape[0]
    nt = rows // (nseq * tt)
    nmat = (tt // chunk) * RW_HEADS
    vmem = (2 * (tt * (C_COLS + RW_WIDTH) * 4 + RW_WIDTH * RW_WIDTH * 2 + 4 * RW_WIDTH * V7X_LANES * 2)
            + (tt + 8) * C_COLS * 4 + 8 * tt * RW_WIDTH * 4 + (2 * nmat + 6 * RW_HEADS) * RW_HD * V7X_LANES * 4
            + 14 * tt * RW_WIDTH * 4)
    const = lambda shape: pl.BlockSpec(shape, lambda s, t: (0,) * len(shape))
    return pl.pallas_call(
        functools.partial(_rw_kernel, tt=tt, chunk=chunk, n_valid=n_valid),
        out_shape=(jax.ShapeDtypeStruct((rows, RW_WIDTH), f32),
                   jax.ShapeDtypeStruct((nseq, RW_HEADS, RW_HD, RW_HD), f32)),
        grid=(nseq, nt),
        in_specs=[pl.BlockSpec((tt, C_COLS), lambda s, t: (s * nt + t, 0)),
                  pl.BlockSpec((1, 1, C_COLS), lambda s, t: (s, 0, 0)),
                  pl.BlockSpec((1, RW_HEADS, RW_HD, RW_HD), lambda s, t: (s, 0, 0, 0)),
                  const((1, C_COLS)), const((V7X_SUBLANES, RW_WIDTH)),
                  const((RW_DECAY_LORA, RW_WIDTH)), const((RW_A_LORA, RW_WIDTH)), const((RW_GATE_LORA, RW_WIDTH)),
                  const((RW_WIDTH, RW_WIDTH))],
        out_specs=(pl.BlockSpec((tt, RW_WIDTH), lambda s, t: (s * nt + t, 0)),
                   pl.BlockSpec((1, RW_HEADS, RW_HD, RW_HD), lambda s, t: (s, 0, 0, 0))),
        scratch_shapes=[pltpu.VMEM((tt + V7X_SUBLANES, C_COLS), f32),
                        pltpu.VMEM((tt, 2 * RW_WIDTH), f32), pltpu.VMEM((tt, 2 * RW_WIDTH), f32),
                        pltpu.VMEM((tt, RW_WIDTH), f32), pltpu.VMEM((tt, RW_WIDTH), f32),
                        pltpu.VMEM((tt, RW_WIDTH), f32), pltpu.VMEM((tt, RW_WIDTH), f32),
                        pltpu.VMEM((nmat, RW_HD, RW_HD), f32), pltpu.VMEM((nmat, RW_HD, RW_HD), f32),
                        pltpu.VMEM((RW_HEADS, RW_HD, RW_HD), f32)],
        compiler_params=_cparams(("parallel", "arbitrary"), vmem),
        name="rwkv7",
    )(pc, shift0, s0, lw["rw_mu"], lw["rw_vec"], lw["rw_w2"], lw["rw_a2"], lw["rw_g2"], lw["bd64"])


def _merge_kernel(h_ref, o0_ref, o1_ref, o2_ref, l0_ref, l1_ref, l2_ref, ob_ref, oc_ref, pg_ref,
                  wa_ref, wb_ref, wc_ref, wo_ref, out_ref):
    l0, l1, l2 = l0_ref[...], l1_ref[...], l2_ref[...]
    mx = jnp.maximum(jnp.maximum(l0, l1), l2)
    e0, e1, e2 = jnp.exp(l0 - mx), jnp.exp(l1 - mx), jnp.exp(l2 - mx)
    oa = (e0 * o0_ref[...] + e1 * o1_ref[...] + e2 * o2_ref[...]) / (e0 + e1 + e2)
    d = D_MODEL
    merged = (_sigmoid(pg_ref[:, 0:d]) * _dot(_bf(oa), wa_ref[...])
              + _sigmoid(pg_ref[:, d:2 * d]) * _dot(_bf(ob_ref[...]), wb_ref[...])
              + _sigmoid(pg_ref[:, 2 * d:3 * d]) * _dot(_bf(oc_ref[...]), wc_ref[...]))
    out_ref[...] = h_ref[...] + _dot(_bf(merged), wo_ref[...])


def _merge_sample_kernel(h_ref, oa_ref, ob_ref, oc_ref, pg_ref, wa_ref, wb_ref, wc_ref, wo_ref, out_ref):
    d = D_MODEL
    merged = (_sigmoid(pg_ref[:, 0:d]) * _dot(_bf(oa_ref[...]), wa_ref[...])
              + _sigmoid(pg_ref[:, d:2 * d]) * _dot(_bf(ob_ref[...]), wb_ref[...])
              + _sigmoid(pg_ref[:, 2 * d:3 * d]) * _dot(_bf(oc_ref[...]), wc_ref[...]))
    out_ref[...] = h_ref[...] + _dot(_bf(merged), wo_ref[...])


def _merge(h, oas, lses, ob, oc, pg, lw):
    rows, d = h.shape
    tr = min(ROW_TILE, rows)
    row = lambda n: pl.BlockSpec((tr, n), lambda i: (i, 0))
    const = lambda a: pl.BlockSpec(a.shape, lambda i: (0, 0))
    ws = (lw["w_br_a"], lw["w_br_b"], lw["w_br_c"], lw["w_out"])
    if lses is None:
        kern, acts = _merge_sample_kernel, (oas,)
    else:
        kern, acts = _merge_kernel, (*oas, *lses)
    acts = (h, *acts, ob, oc, pg)
    vmem = 2 * sum(tr * a.shape[1] * 4 for a in acts) + 2 * sum(w.size * 2 for w in ws) + 8 * tr * d * 4
    return pl.pallas_call(
        kern,
        out_shape=jax.ShapeDtypeStruct((rows, d), f32),
        grid=(rows // tr,),
        in_specs=[row(a.shape[1]) for a in acts] + [const(w) for w in ws],
        out_specs=row(d),
        compiler_params=_cparams(("parallel",), vmem),
        name="merge",
    )(*acts, *ws)


def _peer_candidate_groups(tv_scr):
    k = PEER_TOPK
    s = V7X_SUBLANES
    a_lo = tv_scr[0:s, :]
    a_hi = tv_scr[s:k, :]
    b = [tv_scr[k + j:k + j + 1, :] for j in range(s)]
    b_hi = tv_scr[k + s:2 * k, :]
    row = lax.broadcasted_iota(jnp.int32, a_lo.shape, 0)
    groups = [a_lo + b[0], a_hi + b[0], a_lo + b[1]]
    for j in range(2, s):
        groups.append(jnp.where(row < k // (j + 1), a_lo + b[j], NEG_INF))
    groups.append(tv_scr[0:1, :] + b_hi)
    return groups


def _peer_kernel(h_ref, nw_ref, wq_ref, sk_ref, u_ref, vt_ref, out_ref,
                 xn_scr, s1_scr, e1_scr, s2_scr, e2_scr, thr_scr, tv_scr, acc_scr, *, tokens, eblk):
    e = pl.program_id(1)
    k = PEER_TOPK
    half = PEER_QDIM // 2

    @pl.when(e == 0)
    def _():
        x = h_ref[...]
        xn = _bf(x * lax.rsqrt(jnp.mean(x * x, axis=-1, keepdims=True) + RMS_EPS) * nw_ref[...])
        xn_scr[...] = xn
        q = _bf(_dot(xn, wq_ref[...]))
        for h in range(PEER_HEADS):
            tops = []
            for p, s_scr in ((0, s1_scr), (1, s2_scr)):
                hp = 2 * h + p
                s = _dot_nt(sk_ref[hp], q[:, hp * half:(hp + 1) * half])
                s_scr[h] = s
                cur = s
                for i in range(k):
                    m = jnp.max(cur, axis=0, keepdims=True)
                    tv_scr[p * k + i:p * k + i + 1, :] = m
                    cur = jnp.where(cur >= m, NEG_INF, cur)
                tops.append(tv_scr[p * k:p * k + 1, :])
            groups = _peer_candidate_groups(tv_scr)
            cur = groups
            thr = None
            for i in range(k):
                m = cur[0]
                for g in cur[1:]:
                    m = jnp.maximum(m, g)
                thr = jnp.max(m, axis=0, keepdims=True)
                cur = [jnp.where(g >= thr, NEG_INF, g) for g in cur]
            top = tops[0] + tops[1]
            z = jnp.zeros_like(thr)
            for g in groups:
                z = z + jnp.sum(jnp.where(g >= thr, jnp.exp(g - top), 0.0), axis=0, keepdims=True)
            thr_scr[h:h + 1, :] = thr
            e1_scr[h] = jnp.exp(s1_scr[h] - tops[0])
            e2_scr[h] = jnp.exp(s2_scr[h] - tops[1]) / z
        acc_scr[...] = jnp.zeros_like(acc_scr)

    act = jax.nn.gelu(_dot_nt(u_ref[...], xn_scr[...]))
    parts = []
    for il in range(eblk // PEER_KEYS):
        i1 = e * (eblk // PEER_KEYS) + il
        g = jnp.zeros((PEER_KEYS, tokens), f32)
        for h in range(PEER_HEADS):
            s1row = s1_scr[h, pl.ds(i1, 1), :]
            e1row = e1_scr[h, pl.ds(i1, 1), :]
            sel = (s2_scr[h] + s1row) >= thr_scr[h:h + 1, :]
            g = g + jnp.where(sel, e2_scr[h] * e1row, 0.0)
        parts.append(_bf(g * act[il * PEER_KEYS:(il + 1) * PEER_KEYS, :]))
    acc_scr[...] += _dot(vt_ref[...], jnp.concatenate(parts, axis=0))

    @pl.when(e == pl.num_programs(1) - 1)
    def _():
        out_ref[...] = h_ref[...] + acc_scr[...].T


def _peer(h, lw):
    rows, d = h.shape
    tokens = min(PEER_TOKENS, rows)
    eblk = PEER_EBLK
    n_half = PEER_HEADS * 2
    vmem = (2 * (2 * tokens * d * 4 + d * d * 2 + 2 * eblk * d * 2)
            + tokens * d * 2 + 4 * PEER_HEADS * PEER_KEYS * tokens * 4 + d * tokens * 4
            + 5 * eblk * tokens * 4)
    return pl.pallas_call(
        functools.partial(_peer_kernel, tokens=tokens, eblk=eblk),
        out_shape=jax.ShapeDtypeStruct((rows, d), f32),
        grid=(rows // tokens, PEER_EXPERTS // eblk),
        in_specs=[pl.BlockSpec((tokens, d), lambda i, e: (i, 0)),
                  pl.BlockSpec((1, d), lambda i, e: (0, 0)),
                  pl.BlockSpec((d, PEER_HEADS * PEER_QDIM), lambda i, e: (0, 0)),
                  pl.BlockSpec((n_half, PEER_KEYS, PEER_QDIM // 2), lambda i, e: (0, 0, 0)),
                  pl.BlockSpec((eblk, d), lambda i, e: (e, 0)),
                  pl.BlockSpec((d, eblk), lambda i, e: (0, e))],
        out_specs=pl.BlockSpec((tokens, d), lambda i, e: (i, 0)),
        scratch_shapes=[pltpu.VMEM((tokens, d), bf16),
                        pltpu.VMEM((PEER_HEADS, PEER_KEYS, tokens), f32),
                        pltpu.VMEM((PEER_HEADS, PEER_KEYS, tokens), f32),
                        pltpu.VMEM((PEER_HEADS, PEER_KEYS, tokens), f32),
                        pltpu.VMEM((PEER_HEADS, PEER_KEYS, tokens), f32),
                        pltpu.VMEM((PEER_HEADS, tokens), f32),
                        pltpu.VMEM((2 * PEER_TOPK, tokens), f32),
                        pltpu.VMEM((d, tokens), f32)],
        compiler_params=_cparams(("parallel", "arbitrary"), vmem),
        name="peer",
    )(h, lw["norm_ffn"], lw["peer_wq"], lw["peer_sk"], lw["peer_u"], lw["peer_vt"])


def _layer_weights(l, p):
    w_in = p["w_in"][l]
    o_b, o_c, o_g = A_COLS, A_COLS + B_COLS, A_COLS + B_COLS + C_COLS
    lane = jnp.arange(V7X_LANES)
    on_alpha = (lane >= DN_HEADS) & (lane < 2 * DN_HEADS)

    def alpha_lanes(vec):
        return jnp.where(on_alpha, jnp.pad(vec, (DN_HEADS, V7X_LANES - 2 * DN_HEADS)), 0.0)

    rw_vec = jnp.stack([p["rw_w0"][l], p["rw_a0"][l], p["rw_k_k"][l], p["rw_k_a"][l],
                        p["rw_r_k"][l].reshape(RW_WIDTH), p["rw_ln_w"][l], p["rw_ln_b"][l],
                        jnp.zeros((RW_WIDTH,), f32)])
    return {
        "norm_mix": p["norm_mix"][l],
        "w_a": _bf(w_in[:, 0:A_COLS]),
        "w_b": _bf(w_in[:, o_b:o_b + B_MAIN]),
        "w_bg": _bf(jnp.pad(w_in[:, o_b + B_MAIN:o_c], ((0, 0), (0, V7X_LANES - 2 * DN_HEADS)))),
        "w_c": _bf(w_in[:, o_c:o_g]),
        "w_g": _bf(w_in[:, o_g:]),
        "dn_conv_w": p["dn_conv_w"][l],
        "dn_prm": jnp.zeros((V7X_SUBLANES, V7X_LANES), f32)
                  .at[0].set(alpha_lanes(p["dn_a_log"][l])).at[1].set(alpha_lanes(p["dn_dt_bias"][l])),
        "dn_norm_w": jnp.tile(p["dn_norm_w"][l], DN_HEADS).reshape(1, DN_WIDTH),
        "bd128": _block_diag_ones(DN_WIDTH, DN_HD),
        "bd64": _block_diag_ones(RW_WIDTH, RW_HD),
        "rw_mu": p["rw_mu"][l].reshape(1, C_COLS),
        "rw_vec": rw_vec,
        "rw_w2": _bf(p["rw_w2"][l]), "rw_a2": _bf(p["rw_a2"][l]), "rw_g2": _bf(p["rw_g2"][l]),
        "w_br_a": _bf(p["w_br_a"][l]), "w_br_b": _bf(p["w_br_b"][l]), "w_br_c": _bf(p["w_br_c"][l]),
        "w_out": _bf(p["w_out"][l]),
        "norm_ffn": p["norm_ffn"][l].reshape(1, D_MODEL),
        "peer_wq": _bf(p["peer_wq"][l]),
        "peer_sk": _bf(p["peer_subkeys"][l].reshape(PEER_HEADS * 2, PEER_KEYS, PEER_QDIM // 2)),
        "peer_u": _bf(p["peer_u"][l]),
        "peer_vt": _bf(p["peer_v"][l]).T,
    }


def _project(h, lw):
    nm = lw["norm_mix"]
    return {s: _norm_matmul(h, nm, lw["w_" + s]) for s in ("a", "b", "bg", "c", "g")}


def _kv_rows(pa, gi, lo, hi):
    k = pa[lo:hi, ATT_WIDTH + gi * ATT_OUT:ATT_WIDTH + (gi + 1) * ATT_OUT]
    v = pa[lo:hi, 2 * ATT_WIDTH + gi * ATT_OUT:2 * ATT_WIDTH + (gi + 1) * ATT_OUT]
    return jnp.stack([k, v], axis=1).reshape(hi - lo, 2, ATT_HPG, ATT_HD)


def _prompt_layer(h, lw):
    t = h.shape[0]
    assert t % ATT_TILE == 0 and t % SEQ_TILE == 0
    pr = _project(h, lw)
    groups = [_attn_prompt_group(pr["a"], t, gi) for gi in range(len(ATT_GROUPS))]
    ob, dn_s = _dn_mixer(pr["b"], pr["bg"], jnp.zeros((1, V7X_SUBLANES, B_QKV), f32),
                         jnp.zeros((1, DN_HEADS, DN_HD, DN_HD), f32), lw,
                         nseq=1, tt=SEQ_TILE, chunk=DN_CHUNK, n_valid=SEQ_TILE)
    oc, rw_s = _rw_mixer(pr["c"], jnp.zeros((1, 1, C_COLS), f32),
                         jnp.zeros((1, RW_HEADS, RW_HD, RW_HD), f32), lw,
                         nseq=1, tt=SEQ_TILE, chunk=RW_CHUNK, n_valid=SEQ_TILE)
    h = _merge(h, [g[0] for g in groups], [g[1] for g in groups], ob, oc, pr["g"], lw)
    h = _peer(h, lw)
    kvs = [_kv_rows(pr["a"], gi, t - min(win, t), t)[None] for gi, (win, _) in enumerate(ATT_GROUPS)]
    conv = pr["b"][t - (DN_CONV - 1):t, 0:B_QKV][None]
    shift = pr["c"][t - 1:t]
    return h, kvs, dn_s, conv, rw_s, shift


def _pad_rows(x, nseq, seq):
    n = x.shape[1]
    return jnp.pad(x.reshape(nseq, seq, n), ((0, 0), (0, SAMPLE_ROWS - seq), (0, 0))).reshape(nseq * SAMPLE_ROWS, n)


def _unpad_rows(x, nseq, seq):
    return x.reshape(nseq, SAMPLE_ROWS, x.shape[1])[:, 0:seq].reshape(nseq * seq, x.shape[1])


def _sample_layer(h, lw, layer, caches, dn_state, dn_conv, rw_state, rw_shift, nseq, seq):
    pr = _project(h, lw)
    oa = _attn_sample(pr["a"], caches, layer, nseq, seq)
    conv0 = jnp.pad(dn_conv, ((0, 0), (V7X_SUBLANES - (DN_CONV - 1), 0), (0, 0)))
    ob, dn_s = _dn_mixer(_pad_rows(pr["b"], nseq, seq), _pad_rows(pr["bg"], nseq, seq), conv0, dn_state, lw,
                         nseq=nseq, tt=SAMPLE_ROWS, chunk=SAMPLE_ROWS, n_valid=seq)
    oc, rw_s = _rw_mixer(_pad_rows(pr["c"], nseq, seq), rw_shift[:, None, :], rw_state, lw,
                         nseq=nseq, tt=SAMPLE_ROWS, chunk=SAMPLE_ROWS, n_valid=seq)
    h = _merge(h, oa, None, _unpad_rows(ob, nseq, seq), _unpad_rows(oc, nseq, seq), pr["g"], lw)
    h = _peer(h, lw)
    kvs = [_kv_rows(pr["a"], gi, 0, nseq * seq).reshape(nseq, seq, 2, ATT_HPG, ATT_HD)
           for gi in range(len(ATT_GROUPS))]
    conv = pr["b"][:, 0:B_QKV].reshape(nseq, seq, B_QKV)[:, seq - (DN_CONV - 1):]
    shift = pr["c"].reshape(nseq, seq, C_COLS)[:, seq - 1]
    return h, kvs, dn_s, conv, rw_s, shift


def kernel(x_prompt, x_sample, cache_kv_w128, cache_kv_w512, cache_kv_w2048, state_dn, state_dn_conv, state_rw, state_rw_shift, norm_mix, w_in, dn_conv_w, dn_a_log, dn_dt_bias, dn_norm_w, rw_mu, rw_w0, rw_w2, rw_a0, rw_a2, rw_g2, rw_k_k, rw_k_a, rw_r_k, rw_ln_w, rw_ln_b, w_br_a, w_br_b, w_br_c, w_out, norm_ffn, peer_wq, peer_subkeys, peer_u, peer_v, norm_final):
    p = dict(norm_mix=norm_mix, w_in=w_in, dn_conv_w=dn_conv_w, dn_a_log=dn_a_log, dn_dt_bias=dn_dt_bias,
             dn_norm_w=dn_norm_w, rw_mu=rw_mu, rw_w0=rw_w0, rw_w2=rw_w2, rw_a0=rw_a0, rw_a2=rw_a2, rw_g2=rw_g2,
             rw_k_k=rw_k_k, rw_k_a=rw_k_a, rw_r_k=rw_r_k, rw_ln_w=rw_ln_w, rw_ln_b=rw_ln_b, w_br_a=w_br_a,
             w_br_b=w_br_b, w_br_c=w_br_c, w_out=w_out, norm_ffn=norm_ffn, peer_wq=peer_wq,
             peer_subkeys=peer_subkeys, peer_u=peer_u, peer_v=peer_v)
    depth = w_in.shape[0]
    bp, t, d = x_prompt.shape
    nseq, seq, _ = x_sample.shape
    assert bp == 1 and d == D_MODEL and seq <= SAMPLE_ROWS // 2 and DN_CONV - 1 <= seq
    caches = [c.reshape(depth, nseq, c.shape[2], 2 * ATT_OUT) for c in (cache_kv_w128, cache_kv_w512, cache_kv_w2048)]
    hp = x_prompt.reshape(t, d)
    hs = x_sample.reshape(nseq * seq, d)
    outs_p, outs_s = [], []
    for l in range(depth):
        lw = _layer_weights(l, p)
        hp, *st_p = _prompt_layer(hp, lw)
        hs, *st_s = _sample_layer(hs, lw, l, caches, state_dn[l], state_dn_conv[l], state_rw[l],
                                  state_rw_shift[l], nseq, seq)
        outs_p.append(st_p)
        outs_s.append(st_s)
    y_p = _rmsnorm(hp, norm_final).reshape(bp, t, d)
    y_s = _rmsnorm(hs, norm_final).reshape(nseq, seq, d)

    def stack(outs, pick):
        return jnp.stack([pick(o) for o in outs], axis=0)

    res = [y_p, y_s]
    for gi in range(len(ATT_GROUPS)):
        res.append(stack(outs_p, lambda o: o[0][gi]))
        res.append(stack(outs_s, lambda o: o[0][gi]))
    for idx in (1, 2, 3, 4):
        res.append(stack(outs_p, lambda o: o[idx]))
        res.append(stack(outs_s, lambda o: o[idx]))
    return tuple(res)
```

```python
import functools
import math

import jax
import jax.numpy as jnp
import numpy as np
from jax import lax
from jax.experimental import pallas as pl
from jax.experimental.pallas import tpu as pltpu

f32 = jnp.float32
bf16 = jnp.bfloat16

V7X_LANES = 128
V7X_SUBLANES = 8
V7X_VMEM_BYTES = 64 * 1024 * 1024
VMEM_CEILING = 56 * 1024 * 1024

D_MODEL = 1024
ATT_GROUPS = ((128, 1), (512, 4), (2048, 16))
ATT_HPG = 4
ATT_HD = 64
ATT_HEADS = ATT_HPG * len(ATT_GROUPS)
ATT_WIDTH = ATT_HEADS * ATT_HD
ATT_OUT = ATT_HPG * ATT_HD
ATT_STEPS = 128
DN_HEADS = 4
DN_HD = 128
DN_WIDTH = DN_HEADS * DN_HD
DN_CONV = 4
DN_CHUNK = 64
DN_GROUP = 4
RW_HEADS = 8
RW_HD = 64
RW_WIDTH = RW_HEADS * RW_HD
RW_DECAY_LORA = 64
RW_A_LORA = 64
RW_GATE_LORA = 128
RW_GN_EPS = 64e-5
RW_CHUNK = 64
RW_GROUP = 2
PEER_KEYS = 128
PEER_EXPERTS = PEER_KEYS * PEER_KEYS
PEER_HEADS = 8
PEER_QDIM = 128
PEER_TOPK = 16
RMS_EPS = 1e-6
L2_EPS = 1e-6

A_COLS = 3 * ATT_WIDTH
B_QKV = 3 * DN_WIDTH
B_MAIN = 4 * DN_WIDTH
B_COLS = B_MAIN + 2 * DN_HEADS
C_COLS = 3 * RW_WIDTH + RW_DECAY_LORA + RW_A_LORA + RW_GATE_LORA
G_COLS = 3 * D_MODEL

ROW_TILE = 512
ATT_TILE = 2048
ATT_GROUP = 4
SEQ_TILE = 512
SAMPLE_ROWS = 8
PEER_TOKENS = 512
PEER_EBLK = 1024

NEG_INF = float("-inf")
GELU_C1 = math.sqrt(2.0 / math.pi)
GELU_C3 = 0.044715 * GELU_C1


def _alibi_slopes():
    h = np.arange(1, ATT_HEADS + 1, dtype=np.float32)
    return np.power(np.float32(2.0), -8.0 * h / ATT_HEADS).astype(np.float32)


def _cparams(semantics, vmem_bytes):
    return pltpu.CompilerParams(dimension_semantics=semantics,
                                vmem_limit_bytes=int(min(max(vmem_bytes, 16 * 1024 * 1024), VMEM_CEILING)))


def _dot(a, b):
    return jnp.dot(a, b, preferred_element_type=f32)


def _dot_nt(a, b):
    return lax.dot_general(a, b, (((1,), (1,)), ((), ())), preferred_element_type=f32)


def _dot_tn(a, b):
    return lax.dot_general(a, b, (((0,), (0,)), ((), ())), preferred_element_type=f32)


def _bf(x):
    return x.astype(bf16)


def _sigmoid(x):
    return 1.0 / (1.0 + jnp.exp(-x))


def _segsum(x, bd):
    hi = _bf(x)
    lo = _bf(x - hi.astype(f32))
    return _dot(hi, bd) + _dot(lo, bd)


def _seg_cumsum(x, rowc, seg):
    k = 1
    while k < seg:
        x = x + jnp.where(rowc >= k, pltpu.roll(x, k, 0), 0.0)
        k *= 2
    return x


def _block_diag_ones(width, seg):
    i = np.arange(width)
    return jnp.asarray((i[:, None] // seg) == (i[None, :] // seg), dtype=bf16)


def _norm_matmul_kernel(h_ref, nw_ref, w_ref, o_ref):
    x = h_ref[...]
    xn = x * lax.rsqrt(jnp.mean(x * x, axis=-1, keepdims=True) + RMS_EPS) * nw_ref[...]
    o_ref[...] = _dot(_bf(xn), w_ref[...])


def _norm_matmul(h, nw, w):
    rows, d = h.shape
    n = w.shape[1]
    tr = min(ROW_TILE, rows)
    vmem = 2 * (tr * d * 4 + d * n * 2 + tr * n * 4) + 4 * tr * d * 4
    return pl.pallas_call(
        _norm_matmul_kernel,
        out_shape=jax.ShapeDtypeStruct((rows, n), f32),
        grid=(rows // tr,),
        in_specs=[pl.BlockSpec((tr, d), lambda i: (i, 0)),
                  pl.BlockSpec((1, d), lambda i: (0, 0)),
                  pl.BlockSpec((d, n), lambda i: (0, 0))],
        out_specs=pl.BlockSpec((tr, n), lambda i: (i, 0)),
        compiler_params=_cparams(("parallel",), vmem),
        name="norm_matmul",
    )(h, nw.reshape(1, d), w)


def _rmsnorm_kernel(h_ref, nw_ref, o_ref):
    x = h_ref[...]
    o_ref[...] = x * lax.rsqrt(jnp.mean(x * x, axis=-1, keepdims=True) + RMS_EPS) * nw_ref[...]


def _rmsnorm(h, nw):
    rows, d = h.shape
    tr = min(ROW_TILE, rows)
    return pl.pallas_call(
        _rmsnorm_kernel,
        out_shape=jax.ShapeDtypeStruct((rows, d), f32),
        grid=(rows // tr,),
        in_specs=[pl.BlockSpec((tr, d), lambda i: (i, 0)), pl.BlockSpec((1, d), lambda i: (0, 0))],
        out_specs=pl.BlockSpec((tr, d), lambda i: (i, 0)),
        compiler_params=_cparams(("parallel",), 6 * tr * d * 4),
        name="final_rmsnorm",
    )(h, nw.reshape(1, d))


def _attn_prompt_kernel(q_ref, kc_ref, vc_ref, kp_ref, vp_ref, o_ref, lse_ref, kk_scr, vv_scr, *, dil, slopes):
    n = ATT_STEPS
    nblk = ATT_TILE // (dil * n)
    tile = pl.program_id(0)
    pair = pl.program_id(1)
    kk_scr[0:ATT_TILE, :] = kp_ref[...]
    kk_scr[ATT_TILE:2 * ATT_TILE, :] = kc_ref[...]
    vv_scr[0:ATT_TILE, :] = vp_ref[...]
    vv_scr[ATT_TILE:2 * ATT_TILE, :] = vc_ref[...]
    ii = lax.broadcasted_iota(jnp.int32, (n, 2 * n), 0)
    jj = lax.broadcasted_iota(jnp.int32, (n, 2 * n), 1)
    steps = n + ii - jj
    band = (steps >= 0) & (steps <= n)
    dist = (steps * dil).astype(f32)
    biases = []
    for hh in range(2):
        slope = jnp.where(pair == 0, slopes[hh], slopes[2 + hh])
        biases.append(jnp.where(band, -slope * dist, NEG_INF))

    def body(gidx, carry):
        items = []
        for cc in range(ATT_GROUP):
            c = gidx * ATT_GROUP + cc
            r = c % dil
            b = c // dil
            qs = r + dil * n * b
            rows_q = pl.ds(qs, n, stride=dil)
            rows_k = pl.ds(ATT_TILE + qs - dil * n, 2 * n, stride=dil)
            q = q_ref[rows_q, :] * (ATT_HD ** -0.5)
            k = kk_scr[rows_k, :]
            first_key = jnp.where(tile * nblk + b == 0, n, 0)
            ss = [_dot_nt(_bf(q[:, hh * ATT_HD:(hh + 1) * ATT_HD]), _bf(k[:, hh * ATT_HD:(hh + 1) * ATT_HD]))
                  for hh in range(2)]
            items.append(dict(rows_q=rows_q, rows_k=rows_k, first_key=first_key, s=ss))
        for it in items:
            it["p"], it["den"], it["lse"] = [], [], []
            for hh in range(2):
                s = jnp.where(jj >= it["first_key"], it["s"][hh] + biases[hh], NEG_INF)
                mx = jnp.max(s, axis=-1, keepdims=True)
                p = jnp.exp(s - mx)
                den = jnp.sum(p, axis=-1, keepdims=True)
                it["p"].append(_bf(p))
                it["den"].append(den)
                it["lse"].append(jnp.broadcast_to(mx + jnp.log(den), (n, ATT_HD)))
        for it in items:
            v = vv_scr[it["rows_k"], :]
            outs = [_dot(it["p"][hh], _bf(v[:, hh * ATT_HD:(hh + 1) * ATT_HD])) / it["den"][hh] for hh in range(2)]
            o_ref[it["rows_q"], :] = jnp.concatenate(outs, axis=1)
            lse_ref[it["rows_q"], :] = jnp.concatenate(it["lse"], axis=1)
        return carry

    lax.fori_loop(0, dil * nblk // ATT_GROUP, body, 0)


def _attn_prompt_group(pa, seq_len, gi):
    _, dil = ATT_GROUPS[gi]
    slopes = tuple(float(s) for s in _alibi_slopes()[gi * ATT_HPG:(gi + 1) * ATT_HPG])
    w = V7X_LANES
    qb, kb, vb = (gi * 2, 6 + gi * 2, 12 + gi * 2)
    blk = (ATT_TILE, w)
    cur = lambda base: pl.BlockSpec(blk, lambda i, j, base=base: (i, base + j))
    prev = lambda base: pl.BlockSpec(blk, lambda i, j, base=base: (jnp.maximum(i - 1, 0), base + j))
    vmem = 2 * 7 * ATT_TILE * w * 4 + 2 * 2 * ATT_TILE * w * 4 + 8 * 1024 * 1024
    return pl.pallas_call(
        functools.partial(_attn_prompt_kernel, dil=dil, slopes=slopes),
        out_shape=(jax.ShapeDtypeStruct((seq_len, ATT_OUT), f32), jax.ShapeDtypeStruct((seq_len, ATT_OUT), f32)),
        grid=(seq_len // ATT_TILE, 2),
        in_specs=[cur(qb), cur(kb), cur(vb), prev(kb), prev(vb)],
        out_specs=(pl.BlockSpec(blk, lambda i, j: (i, j)), pl.BlockSpec(blk, lambda i, j: (i, j))),
        scratch_shapes=[pltpu.VMEM((2 * ATT_TILE, w), f32), pltpu.VMEM((2 * ATT_TILE, w), f32)],
        compiler_params=_cparams(("parallel", "parallel"), vmem),
        name=f"attn_prompt_g{gi}",
    )(pa, pa, pa, pa, pa)


def _attn_sample_kernel(q_ref, k_ref, v_ref, c0_ref, c1_ref, c2_ref, o_ref, *, slopes, seq):
    rows = 2 * seq
    caches = (c0_ref, c1_ref, c2_ref)
    row = lax.broadcasted_iota(jnp.int32, (rows, 1), 0)
    own0 = row < seq
    qpos = row % seq
    npos = lax.broadcasted_iota(jnp.int32, (rows, rows), 1)
    q_all = q_ref[0] * (ATT_HD ** -0.5)
    k_all = k_ref[0]
    v_all = v_ref[0]
    outs, lses = [], []
    for gi, (win, dil) in enumerate(ATT_GROUPS):
        cpos = lax.broadcasted_iota(jnp.int32, (rows, win), 1)
        dist_c = win + qpos - cpos
        ok_c = (dist_c <= win) & ((dist_c & (dil - 1)) == 0)
        dist_n = row - npos
        ok_n = (dist_n >= 0) & (dist_n <= qpos) & ((dist_n & (dil - 1)) == 0)
        dcf = dist_c.astype(f32)
        dnf = dist_n.astype(f32)
        for h in range(ATT_HPG):
            slope = slopes[gi * ATT_HPG + h]
            cols = slice(gi * ATT_OUT + h * ATT_HD, gi * ATT_OUT + (h + 1) * ATT_HD)
            kcols = slice(h * ATT_HD, (h + 1) * ATT_HD)
            vcols = slice(ATT_OUT + h * ATT_HD, ATT_OUT + (h + 1) * ATT_HD)
            qh = _bf(q_all[:, cols])
            sn = jnp.where(ok_n, _dot_nt(qh, _bf(k_all[:, cols])) - slope * dnf, NEG_INF)
            scs = []
            for b in range(2):
                sc = _dot_nt(qh, _bf(caches[gi][0, b, :, kcols])) - slope * dcf
                scs.append(jnp.where(ok_c, sc, NEG_INF))
            sc = jnp.where(own0, scs[0], scs[1])
            mx = jnp.maximum(jnp.max(sc, axis=-1, keepdims=True), jnp.max(sn, axis=-1, keepdims=True))
            pc = jnp.exp(sc - mx)
            pn = jnp.exp(sn - mx)
            den = jnp.sum(pc, axis=-1, keepdims=True) + jnp.sum(pn, axis=-1, keepdims=True)
            pcb = _bf(pc)
            oc = jnp.where(own0, _dot(pcb, _bf(caches[gi][0, 0, :, vcols])), _dot(pcb, _bf(caches[gi][0, 1, :, vcols])))
            o = (oc + _dot(_bf(pn), _bf(v_all[:, cols]))) / den
            outs.append(o)
            lses.append(mx + jnp.log(den))
    merged = []
    for h in range(ATT_HPG):
        ls = [lses[gi * ATT_HPG + h] for gi in range(len(ATT_GROUPS))]
        mx = jnp.maximum(jnp.maximum(ls[0], ls[1]), ls[2])
        es = [jnp.exp(l - mx) for l in ls]
        den = es[0] + es[1] + es[2]
        acc = es[0] * outs[h] + es[1] * outs[ATT_HPG + h] + es[2] * outs[2 * ATT_HPG + h]
        merged.append(acc / den)
    o_ref[0] = jnp.concatenate(merged, axis=1)


def _attn_sample(pa, caches, layer, nseq, seq):
    rows = 2 * seq
    assert rows == SAMPLE_ROWS and nseq % 2 == 0
    slopes = tuple(float(s) for s in _alibi_slopes())
    pa3 = pa.reshape(nseq // 2, rows, A_COLS)
    in_specs = [pl.BlockSpec((1, rows, ATT_WIDTH), lambda i, c=c: (i, 0, c)) for c in range(3)]
    vmem = 0
    for (win, _), c in zip(ATT_GROUPS, caches):
        assert c.shape[2] == win, "cached window must hold exactly `window` rows"
        in_specs.append(pl.BlockSpec((1, 2, win, 2 * ATT_OUT), lambda i: (layer, i, 0, 0)))
        vmem += 2 * 2 * win * 2 * ATT_OUT * 4
    out = pl.pallas_call(
        functools.partial(_attn_sample_kernel, slopes=slopes, seq=seq),
        out_shape=jax.ShapeDtypeStruct((nseq // 2, rows, ATT_OUT), f32),
        grid=(nseq // 2,),
        in_specs=in_specs,
        out_specs=pl.BlockSpec((1, rows, ATT_OUT), lambda i: (i, 0, 0)),
        compiler_params=_cparams(("parallel",), vmem + 16 * 1024 * 1024),
        name="attn_sample",
    )(pa3, pa3, pa3, *caches)
    return out.reshape(nseq * seq, ATT_OUT)


def _dn_kernel(pb_ref, pbg_ref, conv0_ref, s0_ref, cw_ref, prm_ref, nw_ref, bd_ref,
               o_ref, sout_ref,
               xs_scr, q_scr, k_scr, v_scr, g_scr, b_scr, oc_scr, qp_scr, m_scr, n_scr, s_scr,
               *, tt, chunk, n_valid):
    t = pl.program_id(1)

    @pl.when(t == 0)
    def _():
        xs_scr[0:V7X_SUBLANES, :] = conv0_ref[0]
        s_scr[...] = s0_ref[0]

    x = pb_ref[:, 0:B_QKV]
    xs_scr[pl.ds(V7X_SUBLANES, tt), :] = x
    cw = cw_ref[...]
    y = x * cw[3:4, :]
    for j in range(DN_CONV - 1):
        y = y + xs_scr[pl.ds(V7X_SUBLANES - (DN_CONV - 1) + j, tt), :] * cw[j:j + 1, :]
    xs_scr[0:V7X_SUBLANES, :] = xs_scr[pl.ds(tt, V7X_SUBLANES), :]
    y = y * _sigmoid(y)
    bd = bd_ref[...]
    q = y[:, 0:DN_WIDTH]
    k = y[:, DN_WIDTH:2 * DN_WIDTH]
    q_scr[...] = q * lax.rsqrt(_segsum(q * q, bd) + L2_EPS) * (DN_HD ** -0.5)
    k_scr[...] = k * lax.rsqrt(_segsum(k * k, bd) + L2_EPS)
    v_scr[...] = y[:, 2 * DN_WIDTH:3 * DN_WIDTH]
    pg = pbg_ref[...]
    beta = _sigmoid(pg)
    g = -jnp.exp(prm_ref[0:1, :]) * jax.nn.softplus(pg + prm_ref[1:2, :])
    rown = lax.broadcasted_iota(jnp.int32, (tt, V7X_LANES), 0)
    if n_valid < tt:
        beta = jnp.where(rown < n_valid, beta, 0.0)
        g = jnp.where(rown < n_valid, g, 0.0)
    g_scr[...] = _seg_cumsum(g, rown & (chunk - 1), chunk)
    b_scr[...] = beta

    ci = lax.broadcasted_iota(jnp.int32, (chunk, chunk), 0)
    cj = lax.broadcasted_iota(jnp.int32, (chunk, chunk), 1)
    incl = ci >= cj
    strict = ci > cj

    nchunks = tt // chunk
    group = min(DN_GROUP, nchunks)

    def precompute(gi, carry):
        items = []
        for cc in range(group):
            cidx = gi * group + cc
            rows = pl.ds(pl.multiple_of(cidx * chunk, chunk), chunk)
            gc = g_scr[rows, :]
            bc = b_scr[rows, :]
            gct = gc.T
            for h in range(DN_HEADS):
                cols = slice(h * DN_HD, (h + 1) * DN_HD)
                qh, kh, vh = q_scr[rows, cols], k_scr[rows, cols], v_scr[rows, cols]
                gcol = gc[:, DN_HEADS + h:DN_HEADS + h + 1]
                grow = gct[DN_HEADS + h:DN_HEADS + h + 1, :]
                bcol = bc[:, h:h + 1]
                glast = gc[chunk - 1:chunk, DN_HEADS + h:DN_HEADS + h + 1]
                decay = jnp.exp(jnp.where(incl, gcol - grow, NEG_INF))
                kb = kh * bcol
                eg = jnp.exp(gcol)
                both = _dot_nt(_bf(jnp.concatenate([kb, qh], axis=0)), _bf(kh))
                items.append(dict(
                    rows=rows, cols=cols, idx=cidx * DN_HEADS + h,
                    p=jnp.where(strict, -both[0:chunk] * decay, 0.0),
                    attn=jnp.where(incl, both[chunk:2 * chunk] * decay, 0.0),
                    sol=jnp.concatenate([vh * bcol, kb * eg], axis=1),
                    qeg=qh * eg, kdec=_bf(kh * jnp.exp(glast - gcol))))
        k = 1
        while k < chunk:
            for it in items:
                it["sol"] = it["sol"] + _dot(_bf(it["p"]), _bf(it["sol"]))
            if 2 * k < chunk:
                for it in items:
                    pb = _bf(it["p"])
                    it["p"] = _dot(pb, pb)
            k *= 2
        for it in items:
            u = _bf(it["sol"][:, 0:DN_HD])
            w = _bf(it["sol"][:, DN_HD:2 * DN_HD])
            attn = _bf(it["attn"])
            qp_scr[it["rows"], it["cols"]] = it["qeg"] - _dot(attn, w)
            oc_scr[it["rows"], it["cols"]] = _dot(attn, u)
            m_scr[it["idx"]] = _dot_tn(it["kdec"], w)
            n_scr[it["idx"]] = _dot_tn(it["kdec"], u)
        return carry

    lax.fori_loop(0, nchunks // group, precompute, 0)

    def recur(c, carry):
        rows = pl.ds(pl.multiple_of(c * chunk, chunk), chunk)
        g_tail = g_scr[pl.ds(pl.multiple_of((c + 1) * chunk - V7X_SUBLANES, V7X_SUBLANES), V7X_SUBLANES), :]
        states = [s_scr[h] for h in range(DN_HEADS)]
        sb = [_bf(s) for s in states]
        outs = [_dot(_bf(qp_scr[rows, h * DN_HD:(h + 1) * DN_HD]), sb[h]) for h in range(DN_HEADS)]
        upd = [_dot(_bf(m_scr[c * DN_HEADS + h]), sb[h]) for h in range(DN_HEADS)]
        for h in range(DN_HEADS):
            cols = slice(h * DN_HD, (h + 1) * DN_HD)
            oc_scr[rows, cols] = oc_scr[rows, cols] + outs[h]
            g_end = jnp.exp(g_tail[V7X_SUBLANES - 1:V7X_SUBLANES, DN_HEADS + h:DN_HEADS + h + 1])
            s_scr[h] = states[h] * g_end - upd[h] + n_scr[c * DN_HEADS + h]
        return carry

    lax.fori_loop(0, nchunks, recur, 0)

    o = oc_scr[...]
    z = pb_ref[:, B_QKV:B_MAIN]
    o = o * lax.rsqrt(_segsum(o * o, bd) * (1.0 / DN_HD) + RMS_EPS) * nw_ref[...]
    o_ref[...] = o * (z * _sigmoid(z))

    @pl.when(t == pl.num_programs(1) - 1)
    def _():
        sout_ref[0] = s_scr[...]


def _dn_mixer(pb, pbg, conv0, s0, lw, *, nseq, tt, chunk, n_valid):
    rows = pb.shape[0]
    nt = rows // (nseq * tt)
    nmat = (tt // chunk) * DN_HEADS
    vmem = (2 * (tt * (B_MAIN + V7X_LANES + DN_WIDTH) * 4 + DN_WIDTH * DN_WIDTH * 2)
            + (tt + 8) * B_QKV * 4 + 4 * tt * DN_WIDTH * 4 + 2 * tt * V7X_LANES * 4
            + tt * DN_WIDTH * 4 + (2 * nmat + 6 * DN_HEADS) * DN_HD * DN_HD * 4 + 8 * tt * B_QKV * 4)
    return pl.pallas_call(
        functools.partial(_dn_kernel, tt=tt, chunk=chunk, n_valid=n_valid),
        out_shape=(jax.ShapeDtypeStruct((rows, DN_WIDTH), f32),
                   jax.ShapeDtypeStruct((nseq, DN_HEADS, DN_HD, DN_HD), f32)),
        grid=(nseq, nt),
        in_specs=[pl.BlockSpec((tt, B_MAIN), lambda s, t: (s * nt + t, 0)),
                  pl.BlockSpec((tt, V7X_LANES), lambda s, t: (s * nt + t, 0)),
                  pl.BlockSpec((1, V7X_SUBLANES, B_QKV), lambda s, t: (s, 0, 0)),
                  pl.BlockSpec((1, DN_HEADS, DN_HD, DN_HD), lambda s, t: (s, 0, 0, 0)),
                  pl.BlockSpec((DN_CONV, B_QKV), lambda s, t: (0, 0)),
                  pl.BlockSpec((V7X_SUBLANES, V7X_LANES), lambda s, t: (0, 0)),
                  pl.BlockSpec((1, DN_WIDTH), lambda s, t: (0, 0)),
                  pl.BlockSpec((DN_WIDTH, DN_WIDTH), lambda s, t: (0, 0))],
        out_specs=(pl.BlockSpec((tt, DN_WIDTH), lambda s, t: (s * nt + t, 0)),
                   pl.BlockSpec((1, DN_HEADS, DN_HD, DN_HD), lambda s, t: (s, 0, 0, 0))),
        scratch_shapes=[pltpu.VMEM((tt + V7X_SUBLANES, B_QKV), f32),
                        pltpu.VMEM((tt, DN_WIDTH), f32), pltpu.VMEM((tt, DN_WIDTH), f32),
                        pltpu.VMEM((tt, DN_WIDTH), f32),
                        pltpu.VMEM((tt, V7X_LANES), f32), pltpu.VMEM((tt, V7X_LANES), f32),
                        pltpu.VMEM((tt, DN_WIDTH), f32), pltpu.VMEM((tt, DN_WIDTH), f32),
                        pltpu.VMEM((nmat, DN_HD, DN_HD), f32), pltpu.VMEM((nmat, DN_HD, DN_HD), f32),
                        pltpu.VMEM((DN_HEADS, DN_HD, DN_HD), f32)],
        compiler_params=_cparams(("parallel", "arbitrary"), vmem),
        name="deltanet",
    )(pb, pbg, conv0, s0, lw["dn_conv_w"], lw["dn_prm"], lw["dn_norm_w"], lw["bd128"])


def _rw_kernel(pc_ref, shift0_ref, s0_ref, mu_ref, vec_ref, w2_ref, a2_ref, g2_ref, bd_ref,
               o_ref, sout_ref,
               xs_scr, ar_scr, bk_scr, v_scr, pe_scr, oc_scr, rp_scr, m_scr, n_scr, s_scr, *, tt, chunk, n_valid):
    t = pl.program_id(1)

    @pl.when(t == 0)
    def _():
        xs_scr[V7X_SUBLANES - 1:V7X_SUBLANES, :] = shift0_ref[0]
        s_scr[...] = s0_ref[0]

    pc = pc_ref[...]
    xs_scr[pl.ds(V7X_SUBLANES, tt), :] = pc
    prev = xs_scr[pl.ds(V7X_SUBLANES - 1, tt), :]
    xs_scr[0:V7X_SUBLANES, :] = xs_scr[pl.ds(tt, V7X_SUBLANES), :]
    xc = pc + (prev - pc) * mu_ref[...]
    w3 = 3 * RW_WIDTH
    r = xc[:, 0:RW_WIDTH]
    k = xc[:, RW_WIDTH:2 * RW_WIDTH]
    v = xc[:, 2 * RW_WIDTH:w3]
    wd = xc[:, w3:w3 + RW_DECAY_LORA]
    ad = xc[:, w3 + RW_DECAY_LORA:w3 + RW_DECAY_LORA + RW_A_LORA]
    gd = xc[:, w3 + RW_DECAY_LORA + RW_A_LORA:C_COLS]
    w0, a0, k_k, k_a, r_k, ln_w, ln_b = (vec_ref[i:i + 1, :] for i in range(7))
    w_log = -jax.nn.softplus(-(w0 + _dot(_bf(jnp.tanh(wd)), w2_ref[...]))) - 0.5
    lw = -jnp.exp(w_log)
    a = _sigmoid(a0 + _dot(_bf(ad), a2_ref[...]))
    gate = _dot(_bf(_sigmoid(gd)), g2_ref[...])
    bd = bd_ref[...]
    kkr = k * k_k
    kk = kkr * lax.rsqrt(_segsum(kkr * kkr, bd) + L2_EPS)
    k2 = k * (1.0 + (a - 1.0) * k_a)
    bonus = _segsum(r * k2 * r_k, bd) * v
    rown = lax.broadcasted_iota(jnp.int32, (tt, RW_WIDTH), 0)
    if n_valid < tt:
        pad = rown >= n_valid
        lw = jnp.where(pad, 0.0, lw)
        kk = jnp.where(pad, 0.0, kk)
        k2 = jnp.where(pad, 0.0, k2)
    gcum = _seg_cumsum(lw, rown & (chunk - 1), chunk)
    e_pos = jnp.exp(gcum)
    e_neg = jnp.exp(-gcum)
    ar_scr[:, 0:RW_WIDTH] = -kk * jnp.exp(gcum - lw)
    ar_scr[:, RW_WIDTH:2 * RW_WIDTH] = r * e_pos
    bk_scr[:, 0:RW_WIDTH] = kk * a * e_neg
    bk_scr[:, RW_WIDTH:2 * RW_WIDTH] = k2 * e_neg
    v_scr[...] = v
    pe_scr[...] = e_pos

    ci = lax.broadcasted_iota(jnp.int32, (2 * chunk, 2 * chunk), 0)
    cj = lax.broadcasted_iota(jnp.int32, (2 * chunk, 2 * chunk), 1)
    keep = jnp.where(ci < chunk, ci - 1, ci - chunk) >= (cj & (chunk - 1))

    nchunks = tt // chunk
    group = min(RW_GROUP, nchunks)

    def precompute(gi, carry):
        items = []
        for cc in range(group):
            cidx = gi * group + cc
            rows = pl.ds(pl.multiple_of(cidx * chunk, chunk), chunk)
            p_tail = pe_scr[pl.ds(pl.multiple_of((cidx + 1) * chunk - V7X_SUBLANES, V7X_SUBLANES), V7X_SUBLANES), :]
            for h in range(RW_HEADS):
                cols = slice(h * RW_HD, (h + 1) * RW_HD)
                cols2 = slice(RW_WIDTH + h * RW_HD, RW_WIDTH + (h + 1) * RW_HD)
                at, rt = ar_scr[rows, cols], ar_scr[rows, cols2]
                bt, kt = bk_scr[rows, cols], bk_scr[rows, cols2]
                bkb = _bf(jnp.concatenate([bt, kt], axis=0))
                m = jnp.where(keep, _dot_nt(_bf(jnp.concatenate([at, rt], axis=0)), bkb), 0.0)
                items.append(dict(rows=rows, cols=cols, idx=cidx * RW_HEADS + h, at=at, rt=rt, bt=_bf(bt), bkb=bkb,
                                  vh=v_scr[rows, cols], p=m[0:chunk, 0:chunk], aak=_bf(m[0:chunk, chunk:2 * chunk]),
                                  mlow=_bf(m[chunk:2 * chunk, :]), p_end=p_tail[V7X_SUBLANES - 1:V7X_SUBLANES, cols]))
        for it in items:
            it["sol"] = jnp.concatenate([it["at"], _dot(it["aak"], _bf(it["vh"]))], axis=1)
        k = 1
        while k < chunk:
            for it in items:
                it["sol"] = it["sol"] + _dot(_bf(it["p"]), _bf(it["sol"]))
            if 2 * k < chunk:
                for it in items:
                    pb = _bf(it["p"])
                    it["p"] = _dot(pb, pb)
            k *= 2
        for it in items:
            wm = _bf(it["sol"][:, 0:RW_HD])
            uv = _bf(jnp.concatenate([it["sol"][:, RW_HD:2 * RW_HD], it["vh"]], axis=0))
            rp_scr[it["rows"], it["cols"]] = it["rt"] + _dot(it["mlow"][:, 0:chunk], wm)
            oc_scr[it["rows"], it["cols"]] = _dot(it["mlow"], uv)
            m_scr[it["idx"]] = _dot_tn(wm, it["bt"]) * it["p_end"]
            n_scr[it["idx"]] = _dot_tn(uv, it["bkb"]) * it["p_end"]
        return carry

    lax.fori_loop(0, nchunks // group, precompute, 0)

    def recur(c, carry):
        rows = pl.ds(pl.multiple_of(c * chunk, chunk), chunk)
        p_tail = pe_scr[pl.ds(pl.multiple_of((c + 1) * chunk - V7X_SUBLANES, V7X_SUBLANES), V7X_SUBLANES), :]
        states = [s_scr[h] for h in range(RW_HEADS)]
        sb = [_bf(s) for s in states]
        outs = [_dot_nt(_bf(rp_scr[rows, h * RW_HD:(h + 1) * RW_HD]), sb[h]) for h in range(RW_HEADS)]
        upd = [_dot(sb[h], _bf(m_scr[c * RW_HEADS + h])) for h in range(RW_HEADS)]
        for h in range(RW_HEADS):
            cols = slice(h * RW_HD, (h + 1) * RW_HD)
            oc_scr[rows, cols] = oc_scr[rows, cols] + outs[h]
            s_scr[h] = states[h] * p_tail[V7X_SUBLANES - 1:V7X_SUBLANES, cols] + upd[h] + n_scr[c * RW_HEADS + h]
        return carry

    lax.fori_loop(0, nchunks, recur, 0)

    o = oc_scr[...]
    mean = _segsum(o, bd) * (1.0 / RW_HD)
    d = o - mean
    var = _segsum(d * d, bd) * (1.0 / RW_HD)
    o = d * lax.rsqrt(var + RW_GN_EPS) * ln_w + ln_b
    o_ref[...] = (o + bonus) * gate

    @pl.when(t == pl.num_programs(1) - 1)
    def _():
        sout_ref[0] = s_scr[...]


def _rw_mixer(pc, shift0, s0, lw, *, nseq, tt, chunk, n_valid):
    rows = pc.shape[0]
    nt = rows // (nseq * tt)
    nmat = (tt // chunk) * RW_HEADS
    vmem = (2 * (tt * (C_COLS + RW_WIDTH) * 4 + RW_WIDTH * RW_WIDTH * 2 + 4 * RW_WIDTH * V7X_LANES * 2)
            + (tt + 8) * C_COLS * 4 + 8 * tt * RW_WIDTH * 4 + (2 * nmat + 6 * RW_HEADS) * RW_HD * V7X_LANES * 4
            + 14 * tt * RW_WIDTH * 4)
    const = lambda shape: pl.BlockSpec(shape, lambda s, t: (0,) * len(shape))
    return pl.pallas_call(
        functools.partial(_rw_kernel, tt=tt, chunk=chunk, n_valid=n_valid),
        out_shape=(jax.ShapeDtypeStruct((rows, RW_WIDTH), f32),
                   jax.ShapeDtypeStruct((nseq, RW_HEADS, RW_HD, RW_HD), f32)),
        grid=(nseq, nt),
        in_specs=[pl.BlockSpec((tt, C_COLS), lambda s, t: (s * nt + t, 0)),
                  pl.BlockSpec((1, 1, C_COLS), lambda s, t: (s, 0, 0)),
                  pl.BlockSpec((1, RW_HEADS, RW_HD, RW_HD), lambda s, t: (s, 0, 0, 0)),
                  const((1, C_COLS)), const((V7X_SUBLANES, RW_WIDTH)),
                  const((RW_DECAY_LORA, RW_WIDTH)), const((RW_A_LORA, RW_WIDTH)), const((RW_GATE_LORA, RW_WIDTH)),
                  const((RW_WIDTH, RW_WIDTH))],
        out_specs=(pl.BlockSpec((tt, RW_WIDTH), lambda s, t: (s * nt + t, 0)),
                   pl.BlockSpec((1, RW_HEADS, RW_HD, RW_HD), lambda s, t: (s, 0, 0, 0))),
        scratch_shapes=[pltpu.VMEM((tt + V7X_SUBLANES, C_COLS), f32),
                        pltpu.VMEM((tt, 2 * RW_WIDTH), f32), pltpu.VMEM((tt, 2 * RW_WIDTH), f32),
                        pltpu.VMEM((tt, RW_WIDTH), f32), pltpu.VMEM((tt, RW_WIDTH), f32),
                        pltpu.VMEM((tt, RW_WIDTH), f32), pltpu.VMEM((tt, RW_WIDTH), f32),
                        pltpu.VMEM((nmat, RW_HD, RW_HD), f32), pltpu.VMEM((nmat, RW_HD, RW_HD), f32),
                        pltpu.VMEM((RW_HEADS, RW_HD, RW_HD), f32)],
        compiler_params=_cparams(("parallel", "arbitrary"), vmem),
        name="rwkv7",
    )(pc, shift0, s0, lw["rw_mu"], lw["rw_vec"], lw["rw_w2"], lw["rw_a2"], lw["rw_g2"], lw["bd64"])


def _merge_kernel(h_ref, o0_ref, o1_ref, o2_ref, l0_ref, l1_ref, l2_ref, ob_ref, oc_ref, pg_ref,
                  wa_ref, wb_ref, wc_ref, wo_ref, out_ref):
    l0, l1, l2 = l0_ref[...], l1_ref[...], l2_ref[...]
    mx = jnp.maximum(jnp.maximum(l0, l1), l2)
    e0, e1, e2 = jnp.exp(l0 - mx), jnp.exp(l1 - mx), jnp.exp(l2 - mx)
    oa = (e0 * o0_ref[...] + e1 * o1_ref[...] + e2 * o2_ref[...]) / (e0 + e1 + e2)
    d = D_MODEL
    merged = (_sigmoid(pg_ref[:, 0:d]) * _dot(_bf(oa), wa_ref[...])
              + _sigmoid(pg_ref[:, d:2 * d]) * _dot(_bf(ob_ref[...]), wb_ref[...])
              + _sigmoid(pg_ref[:, 2 * d:3 * d]) * _dot(_bf(oc_ref[...]), wc_ref[...]))
    out_ref[...] = h_ref[...] + _dot(_bf(merged), wo_ref[...])


def _merge_sample_kernel(h_ref, oa_ref, ob_ref, oc_ref, pg_ref, wa_ref, wb_ref, wc_ref, wo_ref, out_ref):
    d = D_MODEL
    merged = (_sigmoid(pg_ref[:, 0:d]) * _dot(_bf(oa_ref[...]), wa_ref[...])
              + _sigmoid(pg_ref[:, d:2 * d]) * _dot(_bf(ob_ref[...]), wb_ref[...])
              + _sigmoid(pg_ref[:, 2 * d:3 * d]) * _dot(_bf(oc_ref[...]), wc_ref[...]))
    out_ref[...] = h_ref[...] + _dot(_bf(merged), wo_ref[...])


def _merge(h, oas, lses, ob, oc, pg, lw):
    rows, d = h.shape
    tr = min(ROW_TILE, rows)
    row = lambda n: pl.BlockSpec((tr, n), lambda i: (i, 0))
    const = lambda a: pl.BlockSpec(a.shape, lambda i: (0, 0))
    ws = (lw["w_br_a"], lw["w_br_b"], lw["w_br_c"], lw["w_out"])
    if lses is None:
        kern, acts = _merge_sample_kernel, (oas,)
    else:
        kern, acts = _merge_kernel, (*oas, *lses)
    acts = (h, *acts, ob, oc, pg)
    vmem = 2 * sum(tr * a.shape[1] * 4 for a in acts) + 2 * sum(w.size * 2 for w in ws) + 8 * tr * d * 4
    return pl.pallas_call(
        kern,
        out_shape=jax.ShapeDtypeStruct((rows, d), f32),
        grid=(rows // tr,),
        in_specs=[row(a.shape[1]) for a in acts] + [const(w) for w in ws],
        out_specs=row(d),
        compiler_params=_cparams(("parallel",), vmem),
        name="merge",
    )(*acts, *ws)


def _peer_candidate_groups(tv_scr):
    k = PEER_TOPK
    s = V7X_SUBLANES
    a_lo = tv_scr[0:s, :]
    a_hi = tv_scr[s:k, :]
    b = [tv_scr[k + j:k + j + 1, :] for j in range(s)]
    b_hi = tv_scr[k + s:2 * k, :]
    row = lax.broadcasted_iota(jnp.int32, a_lo.shape, 0)
    groups = [a_lo + b[0], a_hi + b[0], a_lo + b[1]]
    for j in range(2, s):
        groups.append(jnp.where(row < k // (j + 1), a_lo + b[j], NEG_INF))
    groups.append(tv_scr[0:1, :] + b_hi)
    return groups


def _peer_kernel(h_ref, nw_ref, wq_ref, sk_ref, u_ref, vt_ref, out_ref,
                 xn_scr, n1_scr, e1_scr, r2_scr, e2_scr, sc_scr, rk_scr, tv_scr, acc_scr, *, tokens, eblk):
    e = pl.program_id(1)
    k = PEER_TOPK
    half = PEER_QDIM // 2
    unranked = float(k + 1)

    @pl.when(e == 0)
    def _():
        x = h_ref[...]
        xn = _bf(x * lax.rsqrt(jnp.mean(x * x, axis=-1, keepdims=True) + RMS_EPS) * nw_ref[...])
        xn_scr[...] = xn
        q = _bf(_dot(xn, wq_ref[...]))
        lane_tiles = [slice(lt * V7X_LANES, (lt + 1) * V7X_LANES) for lt in range(tokens // V7X_LANES)]
        for h in range(PEER_HEADS):
            for p in range(2):
                hp = 2 * h + p
                s = _dot_nt(sk_ref[hp], q[:, hp * half:(hp + 1) * half])
                if p == 0:
                    e1_scr[h] = s
                else:
                    sc_scr[...] = s
                for lanes in lane_tiles:
                    cur = s[:, lanes]
                    rank = jnp.full(cur.shape, unranked, f32)
                    for i in range(k):
                        m = jnp.max(cur, axis=0, keepdims=True)
                        tv_scr[p * k + i:p * k + i + 1, lanes] = m
                        hit = cur >= m
                        rank = jnp.where(hit, float(i + 1), rank)
                        cur = jnp.where(hit, NEG_INF, cur)
                    if p == 0:
                        rk_scr[:, lanes] = rank
                    else:
                        r2_scr[h, :, lanes] = _bf(rank)
            cur = _peer_candidate_groups(tv_scr)
            thr = None
            for i in range(k):
                m = cur[0]
                for g in cur[1:]:
                    m = jnp.maximum(m, g)
                thr = jnp.max(m, axis=0, keepdims=True)
                cur = [jnp.where(g >= thr, NEG_INF, g) for g in cur]
            top1 = tv_scr[0:1, :]
            top2 = tv_scr[k:k + 1, :]
            tops1 = tv_scr[0:k, :]
            cnt = jnp.zeros(tops1.shape, f32)
            z = jnp.zeros_like(thr)
            for j in range(k):
                sums = tops1 + tv_scr[k + j:k + j + 1, :]
                ok = sums >= thr
                cnt = cnt + jnp.where(ok, 1.0, 0.0)
                z = z + jnp.sum(jnp.where(ok, jnp.exp(sums - (top1 + top2)), 0.0), axis=0, keepdims=True)
            scale2 = 0.5 / z
            for lanes in lane_tiles:
                rank1 = rk_scr[:, lanes]
                n1 = jnp.zeros(rank1.shape, f32)
                for i in range(k):
                    n1 = jnp.where(rank1 == float(i + 1), cnt[i:i + 1, lanes], n1)
                n1_scr[h, :, lanes] = n1
                e1_scr[h, :, lanes] = jnp.exp(e1_scr[h, :, lanes] - top1[:, lanes])
                e2_scr[h, :, lanes] = _bf(jnp.exp(sc_scr[:, lanes] - top2[:, lanes]) * scale2[:, lanes])
        acc_scr[...] = jnp.zeros_like(acc_scr)

    nb1 = eblk // PEER_KEYS
    nlt = tokens // V7X_LANES
    pack = 2 * V7X_SUBLANES
    xu = _dot_nt(u_ref[...], xn_scr[...])
    act2 = xu + xu * jnp.tanh(xu * (GELU_C1 + GELU_C3 * (xu * xu)))
    zero = jnp.zeros((), bf16)
    parts = []
    for ip in range(nb1 // 2):
        i1s = [e * nb1 + 2 * ip + j for j in range(2)]
        nrows = [[n1_scr[h, pl.ds(i1, 1), :] for h in range(PEER_HEADS)] for i1 in i1s]
        erows = [[e1_scr[h, pl.ds(i1, 1), :] for h in range(PEER_HEADS)] for i1 in i1s]
        tiles = [[None] * nlt for _ in range(2)]
        for lt in range(nlt):
            lanes = slice(lt * V7X_LANES, (lt + 1) * V7X_LANES)
            g = [None, None]
            for h in range(PEER_HEADS):
                r2t = r2_scr[h, :, lanes]
                e2t = e2_scr[h, :, lanes]
                for j in range(2):
                    nb = _bf(jnp.broadcast_to(nrows[j][h][:, lanes], (pack, V7X_LANES)))
                    eb = _bf(jnp.broadcast_to(erows[j][h][:, lanes], (pack, V7X_LANES)))
                    nb = jnp.concatenate([nb] * (PEER_KEYS // pack), axis=0)
                    eb = jnp.concatenate([eb] * (PEER_KEYS // pack), axis=0)
                    hit = jnp.where(r2t <= nb, e2t * eb, zero)
                    g[j] = hit if g[j] is None else g[j] + hit
            for j in range(2):
                tiles[j][lt] = g[j]
        for j in range(2):
            il = 2 * ip + j
            gate = tiles[j][0] if nlt == 1 else jnp.concatenate(tiles[j], axis=1)
            parts.append(gate * _bf(act2[il * PEER_KEYS:(il + 1) * PEER_KEYS, :]))
    acc_scr[...] += _dot(vt_ref[...], jnp.concatenate(parts, axis=0))

    @pl.when(e == pl.num_programs(1) - 1)
    def _():
        out_ref[...] = h_ref[...] + acc_scr[...].T


def _peer(h, lw):
    rows, d = h.shape
    tokens = min(PEER_TOKENS, rows)
    assert tokens % V7X_LANES == 0 and rows % tokens == 0
    eblk = PEER_EBLK
    n_half = PEER_HEADS * 2
    vmem = (2 * (2 * tokens * d * 4 + d * d * 2 + 2 * eblk * d * 2)
            + tokens * d * 2 + 4 * PEER_HEADS * PEER_KEYS * tokens * 4 + d * tokens * 4
            + 5 * eblk * tokens * 4)
    return pl.pallas_call(
        functools.partial(_peer_kernel, tokens=tokens, eblk=eblk),
        out_shape=jax.ShapeDtypeStruct((rows, d), f32),
        grid=(rows // tokens, PEER_EXPERTS // eblk),
        in_specs=[pl.BlockSpec((tokens, d), lambda i, e: (i, 0)),
                  pl.BlockSpec((1, d), lambda i, e: (0, 0)),
                  pl.BlockSpec((d, PEER_HEADS * PEER_QDIM), lambda i, e: (0, 0)),
                  pl.BlockSpec((n_half, PEER_KEYS, PEER_QDIM // 2), lambda i, e: (0, 0, 0)),
                  pl.BlockSpec((eblk, d), lambda i, e: (e, 0)),
                  pl.BlockSpec((d, eblk), lambda i, e: (0, e))],
        out_specs=pl.BlockSpec((tokens, d), lambda i, e: (i, 0)),
        scratch_shapes=[pltpu.VMEM((tokens, d), bf16),
                        pltpu.VMEM((PEER_HEADS, PEER_KEYS, tokens), f32),
                        pltpu.VMEM((PEER_HEADS, PEER_KEYS, tokens), f32),
                        pltpu.VMEM((PEER_HEADS, PEER_KEYS, tokens), bf16),
                        pltpu.VMEM((PEER_HEADS, PEER_KEYS, tokens), bf16),
                        pltpu.VMEM((PEER_KEYS, tokens), f32), pltpu.VMEM((PEER_KEYS, tokens), f32),
                        pltpu.VMEM((2 * PEER_TOPK, tokens), f32),
                        pltpu.VMEM((d, tokens), f32)],
        compiler_params=_cparams(("parallel", "arbitrary"), vmem),
        name="peer",
    )(h, lw["norm_ffn"], lw["peer_wq"], lw["peer_sk"], lw["peer_u"], lw["peer_vt"])


def _layer_weights(l, p):
    w_in = p["w_in"][l]
    o_b, o_c, o_g = A_COLS, A_COLS + B_COLS, A_COLS + B_COLS + C_COLS
    lane = jnp.arange(V7X_LANES)
    on_alpha = (lane >= DN_HEADS) & (lane < 2 * DN_HEADS)

    def alpha_lanes(vec):
        return jnp.where(on_alpha, jnp.pad(vec, (DN_HEADS, V7X_LANES - 2 * DN_HEADS)), 0.0)

    rw_vec = jnp.stack([p["rw_w0"][l], p["rw_a0"][l], p["rw_k_k"][l], p["rw_k_a"][l],
                        p["rw_r_k"][l].reshape(RW_WIDTH), p["rw_ln_w"][l], p["rw_ln_b"][l],
                        jnp.zeros((RW_WIDTH,), f32)])
    return {
        "norm_mix": p["norm_mix"][l],
        "w_a": _bf(w_in[:, 0:A_COLS]),
        "w_b": _bf(w_in[:, o_b:o_b + B_MAIN]),
        "w_bg": _bf(jnp.pad(w_in[:, o_b + B_MAIN:o_c], ((0, 0), (0, V7X_LANES - 2 * DN_HEADS)))),
        "w_c": _bf(w_in[:, o_c:o_g]),
        "w_g": _bf(w_in[:, o_g:]),
        "dn_conv_w": p["dn_conv_w"][l],
        "dn_prm": jnp.zeros((V7X_SUBLANES, V7X_LANES), f32)
                  .at[0].set(alpha_lanes(p["dn_a_log"][l])).at[1].set(alpha_lanes(p["dn_dt_bias"][l])),
        "dn_norm_w": jnp.tile(p["dn_norm_w"][l], DN_HEADS).reshape(1, DN_WIDTH),
        "bd128": _block_diag_ones(DN_WIDTH, DN_HD),
        "bd64": _block_diag_ones(RW_WIDTH, RW_HD),
        "rw_mu": p["rw_mu"][l].reshape(1, C_COLS),
        "rw_vec": rw_vec,
        "rw_w2": _bf(p["rw_w2"][l]), "rw_a2": _bf(p["rw_a2"][l]), "rw_g2": _bf(p["rw_g2"][l]),
        "w_br_a": _bf(p["w_br_a"][l]), "w_br_b": _bf(p["w_br_b"][l]), "w_br_c": _bf(p["w_br_c"][l]),
        "w_out": _bf(p["w_out"][l]),
        "norm_ffn": p["norm_ffn"][l].reshape(1, D_MODEL),
        "peer_wq": _bf(p["peer_wq"][l]),
        "peer_sk": _bf(p["peer_subkeys"][l].reshape(PEER_HEADS * 2, PEER_KEYS, PEER_QDIM // 2)),
        "peer_u": _bf(p["peer_u"][l]),
        "peer_vt": _bf(p["peer_v"][l]).T,
    }


def _project(h, lw):
    nm = lw["norm_mix"]
    return {s: _norm_matmul(h, nm, lw["w_" + s]) for s in ("a", "b", "bg", "c", "g")}


def _kv_rows(pa, gi, lo, hi):
    k = pa[lo:hi, ATT_WIDTH + gi * ATT_OUT:ATT_WIDTH + (gi + 1) * ATT_OUT]
    v = pa[lo:hi, 2 * ATT_WIDTH + gi * ATT_OUT:2 * ATT_WIDTH + (gi + 1) * ATT_OUT]
    return jnp.stack([k, v], axis=1).reshape(hi - lo, 2, ATT_HPG, ATT_HD)


def _prompt_layer(h, lw):
    t = h.shape[0]
    assert t % ATT_TILE == 0 and t % SEQ_TILE == 0
    pr = _project(h, lw)
    groups = [_attn_prompt_group(pr["a"], t, gi) for gi in range(len(ATT_GROUPS))]
    ob, dn_s = _dn_mixer(pr["b"], pr["bg"], jnp.zeros((1, V7X_SUBLANES, B_QKV), f32),
                         jnp.zeros((1, DN_HEADS, DN_HD, DN_HD), f32), lw,
                         nseq=1, tt=SEQ_TILE, chunk=DN_CHUNK, n_valid=SEQ_TILE)
    oc, rw_s = _rw_mixer(pr["c"], jnp.zeros((1, 1, C_COLS), f32),
                         jnp.zeros((1, RW_HEADS, RW_HD, RW_HD), f32), lw,
                         nseq=1, tt=SEQ_TILE, chunk=RW_CHUNK, n_valid=SEQ_TILE)
    h = _merge(h, [g[0] for g in groups], [g[1] for g in groups], ob, oc, pr["g"], lw)
    h = _peer(h, lw)
    kvs = [_kv_rows(pr["a"], gi, t - min(win, t), t)[None] for gi, (win, _) in enumerate(ATT_GROUPS)]
    conv = pr["b"][t - (DN_CONV - 1):t, 0:B_QKV][None]
    shift = pr["c"][t - 1:t]
    return h, kvs, dn_s, conv, rw_s, shift


def _pad_rows(x, nseq, seq):
    n = x.shape[1]
    return jnp.pad(x.reshape(nseq, seq, n), ((0, 0), (0, SAMPLE_ROWS - seq), (0, 0))).reshape(nseq * SAMPLE_ROWS, n)


def _unpad_rows(x, nseq, seq):
    return x.reshape(nseq, SAMPLE_ROWS, x.shape[1])[:, 0:seq].reshape(nseq * seq, x.shape[1])


def _sample_layer(h, lw, layer, caches, dn_state, dn_conv, rw_state, rw_shift, nseq, seq):
    pr = _project(h, lw)
    oa = _attn_sample(pr["a"], caches, layer, nseq, seq)
    conv0 = jnp.pad(dn_conv, ((0, 0), (V7X_SUBLANES - (DN_CONV - 1), 0), (0, 0)))
    ob, dn_s = _dn_mixer(_pad_rows(pr["b"], nseq, seq), _pad_rows(pr["bg"], nseq, seq), conv0, dn_state, lw,
                         nseq=nseq, tt=SAMPLE_ROWS, chunk=SAMPLE_ROWS, n_valid=seq)
    oc, rw_s = _rw_mixer(_pad_rows(pr["c"], nseq, seq), rw_shift[:, None, :], rw_state, lw,
                         nseq=nseq, tt=SAMPLE_ROWS, chunk=SAMPLE_ROWS, n_valid=seq)
    h = _merge(h, oa, None, _unpad_rows(ob, nseq, seq), _unpad_rows(oc, nseq, seq), pr["g"], lw)
    h = _peer(h, lw)
    kvs = [_kv_rows(pr["a"], gi, 0, nseq * seq).reshape(nseq, seq, 2, ATT_HPG, ATT_HD)
           for gi in range(len(ATT_GROUPS))]
    conv = pr["b"][:, 0:B_QKV].reshape(nseq, seq, B_QKV)[:, seq - (DN_CONV - 1):]
    shift = pr["c"].reshape(nseq, seq, C_COLS)[:, seq - 1]
    return h, kvs, dn_s, conv, rw_s, shift


def kernel(x_prompt, x_sample, cache_kv_w128, cache_kv_w512, cache_kv_w2048, state_dn, state_dn_conv, state_rw, state_rw_shift, norm_mix, w_in, dn_conv_w, dn_a_log, dn_dt_bias, dn_norm_w, rw_mu, rw_w0, rw_w2, rw_a0, rw_a2, rw_g2, rw_k_k, rw_k_a, rw_r_k, rw_ln_w, rw_ln_b, w_br_a, w_br_b, w_br_c, w_out, norm_ffn, peer_wq, peer_subkeys, peer_u, peer_v, norm_final):
    p = dict(norm_mix=norm_mix, w_in=w_in, dn_conv_w=dn_conv_w, dn_a_log=dn_a_log, dn_dt_bias=dn_dt_bias,
             dn_norm_w=dn_norm_w, rw_mu=rw_mu, rw_w0=rw_w0, rw_w2=rw_w2, rw_a0=rw_a0, rw_a2=rw_a2, rw_g2=rw_g2,
             rw_k_k=rw_k_k, rw_k_a=rw_k_a, rw_r_k=rw_r_k, rw_ln_w=rw_ln_w, rw_ln_b=rw_ln_b, w_br_a=w_br_a,
             w_br_b=w_br_b, w_br_c=w_br_c, w_out=w_out, norm_ffn=norm_ffn, peer_wq=peer_wq,
             peer_subkeys=peer_subkeys, peer_u=peer_u, peer_v=peer_v)
    depth = w_in.shape[0]
    bp, t, d = x_prompt.shape
    nseq, seq, _ = x_sample.shape
    assert bp == 1 and d == D_MODEL and seq <= SAMPLE_ROWS // 2 and DN_CONV - 1 <= seq
    caches = [c.reshape(depth, nseq, c.shape[2], 2 * ATT_OUT) for c in (cache_kv_w128, cache_kv_w512, cache_kv_w2048)]
    hp = x_prompt.reshape(t, d)
    hs = x_sample.reshape(nseq * seq, d)
    outs_p, outs_s = [], []
    for l in range(depth):
        lw = _layer_weights(l, p)
        hp, *st_p = _prompt_layer(hp, lw)
        hs, *st_s = _sample_layer(hs, lw, l, caches, state_dn[l], state_dn_conv[l], state_rw[l],
                                  state_rw_shift[l], nseq, seq)
        outs_p.append(st_p)
        outs_s.append(st_s)
    y_p = _rmsnorm(hp, norm_final).reshape(bp, t, d)
    y_s = _rmsnorm(hs, norm_final).reshape(nseq, seq, d)

    def stack(outs, pick):
        return jnp.stack([pick(o) for o in outs], axis=0)

    res = [y_p, y_s]
    for gi in range(len(ATT_GROUPS)):
        res.append(stack(outs_p, lambda o: o[0][gi]))
        res.append(stack(outs_s, lambda o: o[0][gi]))
    for idx in (1, 2, 3, 4):
        res.append(stack(outs_p, lambda o: o[idx]))
        res.append(stack(outs_s, lambda o: o[idx]))
    return tuple(res)
```

```python
import functools
import math

import jax
import jax.numpy as jnp
import numpy as np
from jax import lax
from jax.experimental import pallas as pl
from jax.experimental.pallas import tpu as pltpu

f32 = jnp.float32
bf16 = jnp.bfloat16

V7X_LANES = 128
V7X_SUBLANES = 8
V7X_VMEM_BYTES = 64 * 1024 * 1024
VMEM_CEILING = 56 * 1024 * 1024

D_MODEL = 1024
ATT_GROUPS = ((128, 1), (512, 4), (2048, 16))
ATT_HPG = 4
ATT_HD = 64
ATT_HEADS = ATT_HPG * len(ATT_GROUPS)
ATT_WIDTH = ATT_HEADS * ATT_HD
ATT_OUT = ATT_HPG * ATT_HD
ATT_STEPS = 128
DN_HEADS = 4
DN_HD = 128
DN_WIDTH = DN_HEADS * DN_HD
DN_CONV = 4
DN_CHUNK = 64
DN_GROUP = 4
RW_HEADS = 8
RW_HD = 64
RW_WIDTH = RW_HEADS * RW_HD
RW_DECAY_LORA = 64
RW_A_LORA = 64
RW_GATE_LORA = 128
RW_GN_EPS = 64e-5
RW_CHUNK = 64
RW_GROUP = 2
PEER_KEYS = 128
PEER_EXPERTS = PEER_KEYS * PEER_KEYS
PEER_HEADS = 8
PEER_QDIM = 128
PEER_TOPK = 16
RMS_EPS = 1e-6
L2_EPS = 1e-6

A_COLS = 3 * ATT_WIDTH
B_QKV = 3 * DN_WIDTH
B_MAIN = 4 * DN_WIDTH
B_COLS = B_MAIN + 2 * DN_HEADS
C_COLS = 3 * RW_WIDTH + RW_DECAY_LORA + RW_A_LORA + RW_GATE_LORA
G_COLS = 3 * D_MODEL

ROW_TILE = 512
ATT_TILE = 2048
ATT_GROUP = 4
SEQ_TILE = 512
SAMPLE_ROWS = 8
SAMPLE_STEP = 8
PEER_TOKENS = 512
PEER_EBLK = 1024

NEG_INF = float("-inf")
GELU_C1 = math.sqrt(2.0 / math.pi)
GELU_C3 = 0.044715 * GELU_C1


def _alibi_slopes():
    h = np.arange(1, ATT_HEADS + 1, dtype=np.float32)
    return np.power(np.float32(2.0), -8.0 * h / ATT_HEADS).astype(np.float32)


def _cparams(semantics, vmem_bytes):
    return pltpu.CompilerParams(dimension_semantics=semantics,
                                vmem_limit_bytes=int(min(max(vmem_bytes, 16 * 1024 * 1024), VMEM_CEILING)))


def _dot(a, b):
    return jnp.dot(a, b, preferred_element_type=f32)


def _dot_nt(a, b):
    return lax.dot_general(a, b, (((1,), (1,)), ((), ())), preferred_element_type=f32)


def _dot_tn(a, b):
    return lax.dot_general(a, b, (((0,), (0,)), ((), ())), preferred_element_type=f32)


def _bf(x):
    return x.astype(bf16)


def _sigmoid(x):
    return 1.0 / (1.0 + jnp.exp(-x))


def _segsum(x, bd):
    hi = _bf(x)
    lo = _bf(x - hi.astype(f32))
    return _dot(hi, bd) + _dot(lo, bd)


def _seg_cumsum(x, rowc, seg):
    k = 1
    while k < seg:
        x = x + jnp.where(rowc >= k, pltpu.roll(x, k, 0), 0.0)
        k *= 2
    return x


def _block_diag_ones(width, seg):
    i = np.arange(width)
    return jnp.asarray((i[:, None] // seg) == (i[None, :] // seg), dtype=bf16)


def _norm_matmul_kernel(h_ref, nw_ref, w_ref, o_ref):
    x = h_ref[...]
    xn = x * lax.rsqrt(jnp.mean(x * x, axis=-1, keepdims=True) + RMS_EPS) * nw_ref[...]
    o_ref[...] = _dot(_bf(xn), w_ref[...])


def _norm_matmul(h, nw, w):
    rows, d = h.shape
    n = w.shape[1]
    tr = min(ROW_TILE, rows)
    vmem = 2 * (tr * d * 4 + d * n * 2 + tr * n * 4) + 4 * tr * d * 4
    return pl.pallas_call(
        _norm_matmul_kernel,
        out_shape=jax.ShapeDtypeStruct((rows, n), f32),
        grid=(rows // tr,),
        in_specs=[pl.BlockSpec((tr, d), lambda i: (i, 0)),
                  pl.BlockSpec((1, d), lambda i: (0, 0)),
                  pl.BlockSpec((d, n), lambda i: (0, 0))],
        out_specs=pl.BlockSpec((tr, n), lambda i: (i, 0)),
        compiler_params=_cparams(("parallel",), vmem),
        name="norm_matmul",
    )(h, nw.reshape(1, d), w)


def _rmsnorm_kernel(h_ref, nw_ref, o_ref):
    x = h_ref[...]
    o_ref[...] = x * lax.rsqrt(jnp.mean(x * x, axis=-1, keepdims=True) + RMS_EPS) * nw_ref[...]


def _rmsnorm(h, nw):
    rows, d = h.shape
    tr = min(ROW_TILE, rows)
    return pl.pallas_call(
        _rmsnorm_kernel,
        out_shape=jax.ShapeDtypeStruct((rows, d), f32),
        grid=(rows // tr,),
        in_specs=[pl.BlockSpec((tr, d), lambda i: (i, 0)), pl.BlockSpec((1, d), lambda i: (0, 0))],
        out_specs=pl.BlockSpec((tr, d), lambda i: (i, 0)),
        compiler_params=_cparams(("parallel",), 6 * tr * d * 4),
        name="final_rmsnorm",
    )(h, nw.reshape(1, d))


def _attn_prompt_kernel(q_ref, kc_ref, vc_ref, kp_ref, vp_ref, o_ref, lse_ref, kk_scr, vv_scr, *, dil, slopes):
    n = ATT_STEPS
    nblk = ATT_TILE // (dil * n)
    tile = pl.program_id(0)
    pair = pl.program_id(1)
    kk_scr[0:ATT_TILE, :] = kp_ref[...]
    kk_scr[ATT_TILE:2 * ATT_TILE, :] = kc_ref[...]
    vv_scr[0:ATT_TILE, :] = vp_ref[...]
    vv_scr[ATT_TILE:2 * ATT_TILE, :] = vc_ref[...]
    ii = lax.broadcasted_iota(jnp.int32, (n, 2 * n), 0)
    jj = lax.broadcasted_iota(jnp.int32, (n, 2 * n), 1)
    steps = n + ii - jj
    band = (steps >= 0) & (steps <= n)
    dist = (steps * dil).astype(f32)
    biases = []
    for hh in range(2):
        slope = jnp.where(pair == 0, slopes[hh], slopes[2 + hh])
        biases.append(jnp.where(band, -slope * dist, NEG_INF))

    def body(gidx, carry):
        items = []
        for cc in range(ATT_GROUP):
            c = gidx * ATT_GROUP + cc
            r = c % dil
            b = c // dil
            qs = r + dil * n * b
            rows_q = pl.ds(qs, n, stride=dil)
            rows_k = pl.ds(ATT_TILE + qs - dil * n, 2 * n, stride=dil)
            q = q_ref[rows_q, :] * (ATT_HD ** -0.5)
            k = kk_scr[rows_k, :]
            first_key = jnp.where(tile * nblk + b == 0, n, 0)
            ss = [_dot_nt(_bf(q[:, hh * ATT_HD:(hh + 1) * ATT_HD]), _bf(k[:, hh * ATT_HD:(hh + 1) * ATT_HD]))
                  for hh in range(2)]
            items.append(dict(rows_q=rows_q, rows_k=rows_k, first_key=first_key, s=ss))
        for it in items:
            it["p"], it["den"], it["lse"] = [], [], []
            for hh in range(2):
                s = jnp.where(jj >= it["first_key"], it["s"][hh] + biases[hh], NEG_INF)
                mx = jnp.max(s, axis=-1, keepdims=True)
                p = jnp.exp(s - mx)
                den = jnp.sum(p, axis=-1, keepdims=True)
                it["p"].append(_bf(p))
                it["den"].append(den)
                it["lse"].append(jnp.broadcast_to(mx + jnp.log(den), (n, ATT_HD)))
        for it in items:
            v = vv_scr[it["rows_k"], :]
            outs = [_dot(it["p"][hh], _bf(v[:, hh * ATT_HD:(hh + 1) * ATT_HD])) / it["den"][hh] for hh in range(2)]
            o_ref[it["rows_q"], :] = jnp.concatenate(outs, axis=1)
            lse_ref[it["rows_q"], :] = jnp.concatenate(it["lse"], axis=1)
        return carry

    lax.fori_loop(0, dil * nblk // ATT_GROUP, body, 0)


def _attn_prompt_group(pa, seq_len, gi):
    _, dil = ATT_GROUPS[gi]
    slopes = tuple(float(s) for s in _alibi_slopes()[gi * ATT_HPG:(gi + 1) * ATT_HPG])
    w = V7X_LANES
    qb, kb, vb = (gi * 2, 6 + gi * 2, 12 + gi * 2)
    blk = (ATT_TILE, w)
    cur = lambda base: pl.BlockSpec(blk, lambda i, j, base=base: (i, base + j))
    prev = lambda base: pl.BlockSpec(blk, lambda i, j, base=base: (jnp.maximum(i - 1, 0), base + j))
    vmem = 2 * 7 * ATT_TILE * w * 4 + 2 * 2 * ATT_TILE * w * 4 + 8 * 1024 * 1024
    return pl.pallas_call(
        functools.partial(_attn_prompt_kernel, dil=dil, slopes=slopes),
        out_shape=(jax.ShapeDtypeStruct((seq_len, ATT_OUT), f32), jax.ShapeDtypeStruct((seq_len, ATT_OUT), f32)),
        grid=(seq_len // ATT_TILE, 2),
        in_specs=[cur(qb), cur(kb), cur(vb), prev(kb), prev(vb)],
        out_specs=(pl.BlockSpec(blk, lambda i, j: (i, j)), pl.BlockSpec(blk, lambda i, j: (i, j))),
        scratch_shapes=[pltpu.VMEM((2 * ATT_TILE, w), f32), pltpu.VMEM((2 * ATT_TILE, w), f32)],
        compiler_params=_cparams(("parallel", "parallel"), vmem),
        name=f"attn_prompt_g{gi}",
    )(pa, pa, pa, pa, pa)


def _attn_sample_kernel(q_ref, k_ref, v_ref, c0_ref, c1_ref, c2_ref, o_ref, *, slopes, seq, cached):
    rows = 2 * seq
    caches = (c0_ref, c1_ref, c2_ref)
    row = lax.broadcasted_iota(jnp.int32, (rows, 1), 0)
    own0 = row < seq
    qpos = row % seq
    npos = lax.broadcasted_iota(jnp.int32, (rows, rows), 1)
    q_all = q_ref[0] * (ATT_HD ** -0.5)
    k_all = k_ref[0]
    v_all = v_ref[0]
    outs, lses = [], []
    for gi, (_, dil) in enumerate(ATT_GROUPS):
        cwin, cdil, cscale = cached[gi]
        cpos = lax.broadcasted_iota(jnp.int32, (rows, cwin), 1)
        dist_c = cwin + qpos - cpos
        ok_c = (dist_c <= cwin) & ((dist_c & (cdil - 1)) == 0)
        dist_n = row - npos
        ok_n = (dist_n >= 0) & (dist_n <= qpos) & ((dist_n & (dil - 1)) == 0)
        dcf = (dist_c * cscale).astype(f32)
        dnf = dist_n.astype(f32)
        for h in range(ATT_HPG):
            slope = slopes[gi * ATT_HPG + h]
            cols = slice(gi * ATT_OUT + h * ATT_HD, gi * ATT_OUT + (h + 1) * ATT_HD)
            kcols = slice(h * ATT_HD, (h + 1) * ATT_HD)
            vcols = slice(ATT_OUT + h * ATT_HD, ATT_OUT + (h + 1) * ATT_HD)
            qh = _bf(q_all[:, cols])
            sn = jnp.where(ok_n, _dot_nt(qh, _bf(k_all[:, cols])) - slope * dnf, NEG_INF)
            scs = []
            for b in range(2):
                sc = _dot_nt(qh, _bf(caches[gi][0, b, :, kcols])) - slope * dcf
                scs.append(jnp.where(ok_c, sc, NEG_INF))
            sc = jnp.where(own0, scs[0], scs[1])
            mx = jnp.maximum(jnp.max(sc, axis=-1, keepdims=True), jnp.max(sn, axis=-1, keepdims=True))
            pc = jnp.exp(sc - mx)
            pn = jnp.exp(sn - mx)
            den = jnp.sum(pc, axis=-1, keepdims=True) + jnp.sum(pn, axis=-1, keepdims=True)
            pcb = _bf(pc)
            oc = jnp.where(own0, _dot(pcb, _bf(caches[gi][0, 0, :, vcols])), _dot(pcb, _bf(caches[gi][0, 1, :, vcols])))
            o = (oc + _dot(_bf(pn), _bf(v_all[:, cols]))) / den
            outs.append(o)
            lses.append(mx + jnp.log(den))
    merged = []
    for h in range(ATT_HPG):
        ls = [lses[gi * ATT_HPG + h] for gi in range(len(ATT_GROUPS))]
        mx = jnp.maximum(jnp.maximum(ls[0], ls[1]), ls[2])
        es = [jnp.exp(l - mx) for l in ls]
        den = es[0] + es[1] + es[2]
        acc = es[0] * outs[h] + es[1] * outs[ATT_HPG + h] + es[2] * outs[2 * ATT_HPG + h]
        merged.append(acc / den)
    o_ref[0] = jnp.concatenate(merged, axis=1)


def _compact_cache(cache, win, dil, seq):
    depth, nseq = cache.shape[:2]
    c = cache.reshape(depth, nseq, win, 2 * ATT_OUT)
    if dil <= seq:
        return c, (win, dil, 1)
    c = c.reshape(depth, nseq, win // dil, dil, 2 * ATT_OUT)[:, :, :, 0:seq]
    return c.reshape(depth, nseq, (win // dil) * seq, 2 * ATT_OUT), ((win // dil) * seq, seq, dil // seq)


def _attn_sample(pa, caches, cached, layer, nseq, seq):
    rows = 2 * seq
    assert rows == SAMPLE_ROWS and nseq % 2 == 0
    slopes = tuple(float(s) for s in _alibi_slopes())
    pa3 = pa.reshape(nseq // 2, rows, A_COLS)
    in_specs = [pl.BlockSpec((1, rows, ATT_WIDTH), lambda i, c=c: (i, 0, c)) for c in range(3)]
    vmem = 0
    for (cwin, _, _), c in zip(cached, caches):
        in_specs.append(pl.BlockSpec((1, 2, cwin, 2 * ATT_OUT), lambda i: (layer, i, 0, 0)))
        vmem += 2 * 2 * cwin * 2 * ATT_OUT * 4
    out = pl.pallas_call(
        functools.partial(_attn_sample_kernel, slopes=slopes, seq=seq, cached=tuple(cached)),
        out_shape=jax.ShapeDtypeStruct((nseq // 2, rows, ATT_OUT), f32),
        grid=(nseq // 2,),
        in_specs=in_specs,
        out_specs=pl.BlockSpec((1, rows, ATT_OUT), lambda i: (i, 0, 0)),
        compiler_params=_cparams(("parallel",), vmem + 16 * 1024 * 1024),
        name="attn_sample",
    )(pa3, pa3, pa3, *caches)
    return out.reshape(nseq * seq, ATT_OUT)


def _dn_kernel(pb_ref, pbg_ref, conv0_ref, s0_ref, cw_ref, prm_ref, nw_ref, bd_ref,
               o_ref, sout_ref,
               xs_scr, q_scr, k_scr, v_scr, g_scr, b_scr, oc_scr, qp_scr, m_scr, n_scr, s_scr,
               *, tt, chunk, valid, nstate):
    t = pl.program_id(1)

    @pl.when(t == 0)
    def _():
        xs_scr[0:V7X_SUBLANES, :] = conv0_ref[0]
        if nstate == 1:
            s_scr[...] = s0_ref[0]

    x = pb_ref[:, 0:B_QKV]
    xs_scr[pl.ds(V7X_SUBLANES, tt), :] = x
    cw = cw_ref[...]
    y = x * cw[3:4, :]
    for j in range(DN_CONV - 1):
        y = y + xs_scr[pl.ds(V7X_SUBLANES - (DN_CONV - 1) + j, tt), :] * cw[j:j + 1, :]
    xs_scr[0:V7X_SUBLANES, :] = xs_scr[pl.ds(tt, V7X_SUBLANES), :]
    y = y * _sigmoid(y)
    bd = bd_ref[...]
    q = y[:, 0:DN_WIDTH]
    k = y[:, DN_WIDTH:2 * DN_WIDTH]
    q_scr[...] = q * lax.rsqrt(_segsum(q * q, bd) + L2_EPS) * (DN_HD ** -0.5)
    k_scr[...] = k * lax.rsqrt(_segsum(k * k, bd) + L2_EPS)
    v_scr[...] = y[:, 2 * DN_WIDTH:3 * DN_WIDTH]
    pg = pbg_ref[...]
    beta = _sigmoid(pg)
    g = -jnp.exp(prm_ref[0:1, :]) * jax.nn.softplus(pg + prm_ref[1:2, :])
    rown = lax.broadcasted_iota(jnp.int32, (tt, V7X_LANES), 0)
    if valid != (0, chunk):
        rc = rown & (chunk - 1)
        beta = jnp.where(rc >= valid[0], jnp.where(rc < valid[1], beta, 0.0), 0.0)
        g = jnp.where(rc >= valid[0], jnp.where(rc < valid[1], g, 0.0), 0.0)
    g_scr[...] = _seg_cumsum(g, rown & (chunk - 1), chunk)
    b_scr[...] = beta

    ci = lax.broadcasted_iota(jnp.int32, (chunk, chunk), 0)
    cj = lax.broadcasted_iota(jnp.int32, (chunk, chunk), 1)
    incl = ci >= cj
    strict = ci > cj

    nchunks = tt // chunk
    group = min(DN_GROUP, nchunks)

    def precompute(gi, carry):
        items = []
        for cc in range(group):
            cidx = gi * group + cc
            rows = pl.ds(pl.multiple_of(cidx * chunk, chunk), chunk)
            gc = g_scr[rows, :]
            bc = b_scr[rows, :]
            gct = gc.T
            for h in range(DN_HEADS):
                cols = slice(h * DN_HD, (h + 1) * DN_HD)
                qh, kh, vh = q_scr[rows, cols], k_scr[rows, cols], v_scr[rows, cols]
                gcol = gc[:, DN_HEADS + h:DN_HEADS + h + 1]
                grow = gct[DN_HEADS + h:DN_HEADS + h + 1, :]
                bcol = bc[:, h:h + 1]
                glast = gc[chunk - 1:chunk, DN_HEADS + h:DN_HEADS + h + 1]
                decay = jnp.exp(jnp.where(incl, gcol - grow, NEG_INF))
                kb = kh * bcol
                eg = jnp.exp(gcol)
                both = _dot_nt(_bf(jnp.concatenate([kb, qh], axis=0)), _bf(kh))
                items.append(dict(
                    rows=rows, cols=cols, idx=cidx * DN_HEADS + h,
                    p=jnp.where(strict, -both[0:chunk] * decay, 0.0),
                    attn=jnp.where(incl, both[chunk:2 * chunk] * decay, 0.0),
                    sol=jnp.concatenate([vh * bcol, kb * eg], axis=1),
                    qeg=qh * eg, kdec=_bf(kh * jnp.exp(glast - gcol))))
        k = 1
        while k < chunk:
            for it in items:
                it["sol"] = it["sol"] + _dot(_bf(it["p"]), _bf(it["sol"]))
            if 2 * k < chunk:
                for it in items:
                    pb = _bf(it["p"])
                    it["p"] = _dot(pb, pb)
            k *= 2
        for it in items:
            u = _bf(it["sol"][:, 0:DN_HD])
            w = _bf(it["sol"][:, DN_HD:2 * DN_HD])
            attn = _bf(it["attn"])
            qp_scr[it["rows"], it["cols"]] = it["qeg"] - _dot(attn, w)
            oc_scr[it["rows"], it["cols"]] = _dot(attn, u)
            m_scr[it["idx"]] = _dot_tn(it["kdec"], w)
            n_scr[it["idx"]] = _dot_tn(it["kdec"], u)
        return carry

    lax.fori_loop(0, nchunks // group, precompute, 0)

    def recur(c, carry):
        rows = pl.ds(pl.multiple_of(c * chunk, chunk), chunk)
        g_tail = g_scr[pl.ds(pl.multiple_of((c + 1) * chunk - V7X_SUBLANES, V7X_SUBLANES), V7X_SUBLANES), :]
        states = [s_scr[h] if nstate == 1 else s0_ref[c, h] for h in range(DN_HEADS)]
        sb = [_bf(s) for s in states]
        outs = [_dot(_bf(qp_scr[rows, h * DN_HD:(h + 1) * DN_HD]), sb[h]) for h in range(DN_HEADS)]
        upd = [_dot(_bf(m_scr[c * DN_HEADS + h]), sb[h]) for h in range(DN_HEADS)]
        for h in range(DN_HEADS):
            cols = slice(h * DN_HD, (h + 1) * DN_HD)
            oc_scr[rows, cols] = oc_scr[rows, cols] + outs[h]
            g_end = jnp.exp(g_tail[V7X_SUBLANES - 1:V7X_SUBLANES, DN_HEADS + h:DN_HEADS + h + 1])
            new = states[h] * g_end - upd[h] + n_scr[c * DN_HEADS + h]
            if nstate == 1:
                s_scr[h] = new
            else:
                sout_ref[c, h] = new
        return carry

    lax.fori_loop(0, nchunks, recur, 0)

    o = oc_scr[...]
    z = pb_ref[:, B_QKV:B_MAIN]
    o = o * lax.rsqrt(_segsum(o * o, bd) * (1.0 / DN_HD) + RMS_EPS) * nw_ref[...]
    o_ref[...] = o * (z * _sigmoid(z))

    if nstate == 1:
        @pl.when(t == pl.num_programs(1) - 1)
        def _():
            sout_ref[0] = s_scr[...]


def _dn_mixer(pb, pbg, conv0, s0, lw, *, nseq, tt, chunk, valid, nstate):
    rows = pb.shape[0]
    nt = rows // (nseq * tt)
    nmat = (tt // chunk) * DN_HEADS
    vmem = (2 * (tt * (B_MAIN + V7X_LANES + DN_WIDTH) * 4 + DN_WIDTH * DN_WIDTH * 2)
            + (tt + 8) * B_QKV * 4 + 4 * tt * DN_WIDTH * 4 + 2 * tt * V7X_LANES * 4
            + tt * DN_WIDTH * 4 + (2 * nmat + (6 + 4 * nstate) * DN_HEADS) * DN_HD * DN_HD * 4 + 8 * tt * B_QKV * 4)
    return pl.pallas_call(
        functools.partial(_dn_kernel, tt=tt, chunk=chunk, valid=valid, nstate=nstate),
        out_shape=(jax.ShapeDtypeStruct((rows, DN_WIDTH), f32),
                   jax.ShapeDtypeStruct((nseq * nstate, DN_HEADS, DN_HD, DN_HD), f32)),
        grid=(nseq, nt),
        in_specs=[pl.BlockSpec((tt, B_MAIN), lambda s, t: (s * nt + t, 0)),
                  pl.BlockSpec((tt, V7X_LANES), lambda s, t: (s * nt + t, 0)),
                  pl.BlockSpec((1, V7X_SUBLANES, B_QKV), lambda s, t: (s, 0, 0)),
                  pl.BlockSpec((nstate, DN_HEADS, DN_HD, DN_HD), lambda s, t: (s, 0, 0, 0)),
                  pl.BlockSpec((DN_CONV, B_QKV), lambda s, t: (0, 0)),
                  pl.BlockSpec((V7X_SUBLANES, V7X_LANES), lambda s, t: (0, 0)),
                  pl.BlockSpec((1, DN_WIDTH), lambda s, t: (0, 0)),
                  pl.BlockSpec((DN_WIDTH, DN_WIDTH), lambda s, t: (0, 0))],
        out_specs=(pl.BlockSpec((tt, DN_WIDTH), lambda s, t: (s * nt + t, 0)),
                   pl.BlockSpec((nstate, DN_HEADS, DN_HD, DN_HD), lambda s, t: (s, 0, 0, 0))),
        scratch_shapes=[pltpu.VMEM((tt + V7X_SUBLANES, B_QKV), f32),
                        pltpu.VMEM((tt, DN_WIDTH), f32), pltpu.VMEM((tt, DN_WIDTH), f32),
                        pltpu.VMEM((tt, DN_WIDTH), f32),
                        pltpu.VMEM((tt, V7X_LANES), f32), pltpu.VMEM((tt, V7X_LANES), f32),
                        pltpu.VMEM((tt, DN_WIDTH), f32), pltpu.VMEM((tt, DN_WIDTH), f32),
                        pltpu.VMEM((nmat, DN_HD, DN_HD), f32), pltpu.VMEM((nmat, DN_HD, DN_HD), f32),
                        pltpu.VMEM((DN_HEADS, DN_HD, DN_HD), f32)],
        compiler_params=_cparams(("parallel", "arbitrary"), vmem),
        name="deltanet",
    )(pb, pbg, conv0, s0, lw["dn_conv_w"], lw["dn_prm"], lw["dn_norm_w"], lw["bd128"])


def _rw_kernel(pc_ref, shift0_ref, s0_ref, mu_ref, vec_ref, w2_ref, a2_ref, g2_ref, bd_ref,
               o_ref, sout_ref,
               xs_scr, ar_scr, bk_scr, v_scr, pe_scr, oc_scr, rp_scr, m_scr, n_scr, s_scr, *, tt, chunk, valid, nstate):
    t = pl.program_id(1)

    @pl.when(t == 0)
    def _():
        xs_scr[V7X_SUBLANES - 1:V7X_SUBLANES, :] = shift0_ref[0]
        if nstate == 1:
            s_scr[...] = s0_ref[0]

    pc = pc_ref[...]
    xs_scr[pl.ds(V7X_SUBLANES, tt), :] = pc
    prev = xs_scr[pl.ds(V7X_SUBLANES - 1, tt), :]
    xs_scr[0:V7X_SUBLANES, :] = xs_scr[pl.ds(tt, V7X_SUBLANES), :]
    xc = pc + (prev - pc) * mu_ref[...]
    w3 = 3 * RW_WIDTH
    r = xc[:, 0:RW_WIDTH]
    k = xc[:, RW_WIDTH:2 * RW_WIDTH]
    v = xc[:, 2 * RW_WIDTH:w3]
    wd = xc[:, w3:w3 + RW_DECAY_LORA]
    ad = xc[:, w3 + RW_DECAY_LORA:w3 + RW_DECAY_LORA + RW_A_LORA]
    gd = xc[:, w3 + RW_DECAY_LORA + RW_A_LORA:C_COLS]
    w0, a0, k_k, k_a, r_k, ln_w, ln_b = (vec_ref[i:i + 1, :] for i in range(7))
    w_log = -jax.nn.softplus(-(w0 + _dot(_bf(jnp.tanh(wd)), w2_ref[...]))) - 0.5
    lw = -jnp.exp(w_log)
    a = _sigmoid(a0 + _dot(_bf(ad), a2_ref[...]))
    gate = _dot(_bf(_sigmoid(gd)), g2_ref[...])
    bd = bd_ref[...]
    kkr = k * k_k
    kk = kkr * lax.rsqrt(_segsum(kkr * kkr, bd) + L2_EPS)
    k2 = k * (1.0 + (a - 1.0) * k_a)
    bonus = _segsum(r * k2 * r_k, bd) * v
    rown = lax.broadcasted_iota(jnp.int32, (tt, RW_WIDTH), 0)
    if valid != (0, chunk):
        rc = rown & (chunk - 1)
        pad = jnp.where(rc >= valid[0], jnp.where(rc < valid[1], 0, 1), 1) == 1
        lw = jnp.where(pad, 0.0, lw)
        kk = jnp.where(pad, 0.0, kk)
        k2 = jnp.where(pad, 0.0, k2)
    gcum = _seg_cumsum(lw, rown & (chunk - 1), chunk)
    e_pos = jnp.exp(gcum)
    e_neg = jnp.exp(-gcum)
    ar_scr[:, 0:RW_WIDTH] = -kk * jnp.exp(gcum - lw)
    ar_scr[:, RW_WIDTH:2 * RW_WIDTH] = r * e_pos
    bk_scr[:, 0:RW_WIDTH] = kk * a * e_neg
    bk_scr[:, RW_WIDTH:2 * RW_WIDTH] = k2 * e_neg
    v_scr[...] = v
    pe_scr[...] = e_pos

    ci = lax.broadcasted_iota(jnp.int32, (2 * chunk, 2 * chunk), 0)
    cj = lax.broadcasted_iota(jnp.int32, (2 * chunk, 2 * chunk), 1)
    keep = jnp.where(ci < chunk, ci - 1, ci - chunk) >= (cj & (chunk - 1))

    nchunks = tt // chunk
    group = min(RW_GROUP, nchunks)

    def precompute(gi, carry):
        items = []
        for cc in range(group):
            cidx = gi * group + cc
            rows = pl.ds(pl.multiple_of(cidx * chunk, chunk), chunk)
            p_tail = pe_scr[pl.ds(pl.multiple_of((cidx + 1) * chunk - V7X_SUBLANES, V7X_SUBLANES), V7X_SUBLANES), :]
            for h in range(RW_HEADS):
                cols = slice(h * RW_HD, (h + 1) * RW_HD)
                cols2 = slice(RW_WIDTH + h * RW_HD, RW_WIDTH + (h + 1) * RW_HD)
                at, rt = ar_scr[rows, cols], ar_scr[rows, cols2]
                bt, kt = bk_scr[rows, cols], bk_scr[rows, cols2]
                bkb = _bf(jnp.concatenate([bt, kt], axis=0))
                m = jnp.where(keep, _dot_nt(_bf(jnp.concatenate([at, rt], axis=0)), bkb), 0.0)
                items.append(dict(rows=rows, cols=cols, idx=cidx * RW_HEADS + h, at=at, rt=rt, bt=_bf(bt), bkb=bkb,
                                  vh=v_scr[rows, cols], p=m[0:chunk, 0:chunk], aak=_bf(m[0:chunk, chunk:2 * chunk]),
                                  mlow=_bf(m[chunk:2 * chunk, :]), p_end=p_tail[V7X_SUBLANES - 1:V7X_SUBLANES, cols]))
        for it in items:
            it["sol"] = jnp.concatenate([it["at"], _dot(it["aak"], _bf(it["vh"]))], axis=1)
        k = 1
        while k < chunk:
            for it in items:
                it["sol"] = it["sol"] + _dot(_bf(it["p"]), _bf(it["sol"]))
            if 2 * k < chunk:
                for it in items:
                    pb = _bf(it["p"])
                    it["p"] = _dot(pb, pb)
            k *= 2
        for it in items:
            wm = _bf(it["sol"][:, 0:RW_HD])
            uv = _bf(jnp.concatenate([it["sol"][:, RW_HD:2 * RW_HD], it["vh"]], axis=0))
            rp_scr[it["rows"], it["cols"]] = it["rt"] + _dot(it["mlow"][:, 0:chunk], wm)
            oc_scr[it["rows"], it["cols"]] = _dot(it["mlow"], uv)
            m_scr[it["idx"]] = _dot_tn(wm, it["bt"]) * it["p_end"]
            n_scr[it["idx"]] = _dot_tn(uv, it["bkb"]) * it["p_end"]
        return carry

    lax.fori_loop(0, nchunks // group, precompute, 0)

    def recur(c, carry):
        rows = pl.ds(pl.multiple_of(c * chunk, chunk), chunk)
        p_tail = pe_scr[pl.ds(pl.multiple_of((c + 1) * chunk - V7X_SUBLANES, V7X_SUBLANES), V7X_SUBLANES), :]
        states = [s_scr[h] if nstate == 1 else s0_ref[c, h] for h in range(RW_HEADS)]
        sb = [_bf(s) for s in states]
        outs = [_dot_nt(_bf(rp_scr[rows, h * RW_HD:(h + 1) * RW_HD]), sb[h]) for h in range(RW_HEADS)]
        upd = [_dot(sb[h], _bf(m_scr[c * RW_HEADS + h])) for h in range(RW_HEADS)]
        for h in range(RW_HEADS):
            cols = slice(h * RW_HD, (h + 1) * RW_HD)
            oc_scr[rows, cols] = oc_scr[rows, cols] + outs[h]
            new = states[h] * p_tail[V7X_SUBLANES - 1:V7X_SUBLANES, cols] + upd[h] + n_scr[c * RW_HEADS + h]
            if nstate == 1:
                s_scr[h] = new
            else:
                sout_ref[c, h] = new
        return carry

    lax.fori_loop(0, nchunks, recur, 0)

    o = oc_scr[...]
    mean = _segsum(o, bd) * (1.0 / RW_HD)
    d = o - mean
    var = _segsum(d * d, bd) * (1.0 / RW_HD)
    o = d * lax.rsqrt(var + RW_GN_EPS) * ln_w + ln_b
    o_ref[...] = (o + bonus) * gate

    if nstate == 1:
        @pl.when(t == pl.num_programs(1) - 1)
        def _():
            sout_ref[0] = s_scr[...]


def _rw_mixer(pc, shift0, s0, lw, *, nseq, tt, chunk, valid, nstate):
    rows = pc.shape[0]
    nt = rows // (nseq * tt)
    nmat = (tt // chunk) * RW_HEADS
    vmem = (2 * (tt * (C_COLS + RW_WIDTH) * 4 + RW_WIDTH * RW_WIDTH * 2 + 4 * RW_WIDTH * V7X_LANES * 2)
            + (tt + 8) * C_COLS * 4 + 8 * tt * RW_WIDTH * 4 + (2 * nmat + (6 + 4 * nstate) * RW_HEADS) * RW_HD * V7X_LANES * 4
            + 14 * tt * RW_WIDTH * 4)
    const = lambda shape: pl.BlockSpec(shape, lambda s, t: (0,) * len(shape))
    return pl.pallas_call(
        functools.partial(_rw_kernel, tt=tt, chunk=chunk, valid=valid, nstate=nstate),
        out_shape=(jax.ShapeDtypeStruct((rows, RW_WIDTH), f32),
                   jax.ShapeDtypeStruct((nseq * nstate, RW_HEADS, RW_HD, RW_HD), f32)),
        grid=(nseq, nt),
        in_specs=[pl.BlockSpec((tt, C_COLS), lambda s, t: (s * nt + t, 0)),
                  pl.BlockSpec((1, 1, C_COLS), lambda s, t: (s, 0, 0)),
                  pl.BlockSpec((nstate, RW_HEADS, RW_HD, RW_HD), lambda s, t: (s, 0, 0, 0)),
                  const((1, C_COLS)), const((V7X_SUBLANES, RW_WIDTH)),
                  const((RW_DECAY_LORA, RW_WIDTH)), const((RW_A_LORA, RW_WIDTH)), const((RW_GATE_LORA, RW_WIDTH)),
                  const((RW_WIDTH, RW_WIDTH))],
        out_specs=(pl.BlockSpec((tt, RW_WIDTH), lambda s, t: (s * nt + t, 0)),
                   pl.BlockSpec((nstate, RW_HEADS, RW_HD, RW_HD), lambda s, t: (s, 0, 0, 0))),
        scratch_shapes=[pltpu.VMEM((tt + V7X_SUBLANES, C_COLS), f32),
                        pltpu.VMEM((tt, 2 * RW_WIDTH), f32), pltpu.VMEM((tt, 2 * RW_WIDTH), f32),
                        pltpu.VMEM((tt, RW_WIDTH), f32), pltpu.VMEM((tt, RW_WIDTH), f32),
                        pltpu.VMEM((tt, RW_WIDTH), f32), pltpu.VMEM((tt, RW_WIDTH), f32),
                        pltpu.VMEM((nmat, RW_HD, RW_HD), f32), pltpu.VMEM((nmat, RW_HD, RW_HD), f32),
                        pltpu.VMEM((RW_HEADS, RW_HD, RW_HD), f32)],
        compiler_params=_cparams(("parallel", "arbitrary"), vmem),
        name="rwkv7",
    )(pc, shift0, s0, lw["rw_mu"], lw["rw_vec"], lw["rw_w2"], lw["rw_a2"], lw["rw_g2"], lw["bd64"])


def _merge_kernel(h_ref, o0_ref, o1_ref, o2_ref, l0_ref, l1_ref, l2_ref, ob_ref, oc_ref, pg_ref,
                  wa_ref, wb_ref, wc_ref, wo_ref, out_ref):
    l0, l1, l2 = l0_ref[...], l1_ref[...], l2_ref[...]
    mx = jnp.maximum(jnp.maximum(l0, l1), l2)
    e0, e1, e2 = jnp.exp(l0 - mx), jnp.exp(l1 - mx), jnp.exp(l2 - mx)
    oa = (e0 * o0_ref[...] + e1 * o1_ref[...] + e2 * o2_ref[...]) / (e0 + e1 + e2)
    d = D_MODEL
    merged = (_sigmoid(pg_ref[:, 0:d]) * _dot(_bf(oa), wa_ref[...])
              + _sigmoid(pg_ref[:, d:2 * d]) * _dot(_bf(ob_ref[...]), wb_ref[...])
              + _sigmoid(pg_ref[:, 2 * d:3 * d]) * _dot(_bf(oc_ref[...]), wc_ref[...]))
    out_ref[...] = h_ref[...] + _dot(_bf(merged), wo_ref[...])


def _merge_sample_kernel(h_ref, oa_ref, ob_ref, oc_ref, pg_ref, wa_ref, wb_ref, wc_ref, wo_ref, out_ref):
    d = D_MODEL
    merged = (_sigmoid(pg_ref[:, 0:d]) * _dot(_bf(oa_ref[...]), wa_ref[...])
              + _sigmoid(pg_ref[:, d:2 * d]) * _dot(_bf(ob_ref[...]), wb_ref[...])
              + _sigmoid(pg_ref[:, 2 * d:3 * d]) * _dot(_bf(oc_ref[...]), wc_ref[...]))
    out_ref[...] = h_ref[...] + _dot(_bf(merged), wo_ref[...])


def _merge(h, oas, lses, ob, oc, pg, lw):
    rows, d = h.shape
    tr = min(ROW_TILE, rows)
    row = lambda n: pl.BlockSpec((tr, n), lambda i: (i, 0))
    const = lambda a: pl.BlockSpec(a.shape, lambda i: (0, 0))
    ws = (lw["w_br_a"], lw["w_br_b"], lw["w_br_c"], lw["w_out"])
    if lses is None:
        kern, acts = _merge_sample_kernel, (oas,)
    else:
        kern, acts = _merge_kernel, (*oas, *lses)
    acts = (h, *acts, ob, oc, pg)
    vmem = 2 * sum(tr * a.shape[1] * 4 for a in acts) + 2 * sum(w.size * 2 for w in ws) + 8 * tr * d * 4
    return pl.pallas_call(
        kern,
        out_shape=jax.ShapeDtypeStruct((rows, d), f32),
        grid=(rows // tr,),
        in_specs=[row(a.shape[1]) for a in acts] + [const(w) for w in ws],
        out_specs=row(d),
        compiler_params=_cparams(("parallel",), vmem),
        name="merge",
    )(*acts, *ws)


def _peer_candidate_groups(tv_scr):
    k = PEER_TOPK
    s = V7X_SUBLANES
    a_lo = tv_scr[0:s, :]
    a_hi = tv_scr[s:k, :]
    b = [tv_scr[k + j:k + j + 1, :] for j in range(s)]
    b_hi = tv_scr[k + s:2 * k, :]
    row = lax.broadcasted_iota(jnp.int32, a_lo.shape, 0)
    groups = [a_lo + b[0], a_hi + b[0], a_lo + b[1]]
    for j in range(2, s):
        groups.append(jnp.where(row < k // (j + 1), a_lo + b[j], NEG_INF))
    groups.append(tv_scr[0:1, :] + b_hi)
    return groups


def _peer_kernel(h_ref, nw_ref, wq_ref, sk_ref, u_ref, vt_ref, out_ref,
                 xn_scr, n1_scr, e1_scr, r2_scr, e2_scr, sc_scr, rk_scr, tv_scr, acc_scr, *, tokens, eblk):
    e = pl.program_id(1)
    k = PEER_TOPK
    half = PEER_QDIM // 2
    unranked = float(k + 1)

    @pl.when(e == 0)
    def _():
        x = h_ref[...]
        xn = _bf(x * lax.rsqrt(jnp.mean(x * x, axis=-1, keepdims=True) + RMS_EPS) * nw_ref[...])
        xn_scr[...] = xn
        q = _bf(_dot(xn, wq_ref[...]))
        lane_tiles = [slice(lt * V7X_LANES, (lt + 1) * V7X_LANES) for lt in range(tokens // V7X_LANES)]
        for h in range(PEER_HEADS):
            for p in range(2):
                hp = 2 * h + p
                s = _dot_nt(sk_ref[hp], q[:, hp * half:(hp + 1) * half])
                if p == 0:
                    e1_scr[h] = s
                else:
                    sc_scr[...] = s
                for lanes in lane_tiles:
                    cur = s[:, lanes]
                    rank = jnp.full(cur.shape, unranked, f32)
                    for i in range(k):
                        m = jnp.max(cur, axis=0, keepdims=True)
                        tv_scr[p * k + i:p * k + i + 1, lanes] = m
                        hit = cur >= m
                        rank = jnp.where(hit, float(i + 1), rank)
                        cur = jnp.where(hit, NEG_INF, cur)
                    if p == 0:
                        rk_scr[:, lanes] = rank
                    else:
                        r2_scr[h, :, lanes] = _bf(rank)
            cur = _peer_candidate_groups(tv_scr)
            thr = None
            for i in range(k):
                m = cur[0]
                for g in cur[1:]:
                    m = jnp.maximum(m, g)
                thr = jnp.max(m, axis=0, keepdims=True)
                cur = [jnp.where(g >= thr, NEG_INF, g) for g in cur]
            top1 = tv_scr[0:1, :]
            top2 = tv_scr[k:k + 1, :]
            tops1 = tv_scr[0:k, :]
            cnt = jnp.zeros(tops1.shape, f32)
            z = jnp.zeros_like(thr)
            for j in range(k):
                sums = tops1 + tv_scr[k + j:k + j + 1, :]
                ok = sums >= thr
                cnt = cnt + jnp.where(ok, 1.0, 0.0)
                z = z + jnp.sum(jnp.where(ok, jnp.exp(sums - (top1 + top2)), 0.0), axis=0, keepdims=True)
            scale2 = 0.5 / z
            for lanes in lane_tiles:
                rank1 = rk_scr[:, lanes]
                n1 = jnp.zeros(rank1.shape, f32)
                for i in range(k):
                    n1 = jnp.where(rank1 == float(i + 1), cnt[i:i + 1, lanes], n1)
                n1_scr[h, :, lanes] = n1
                e1_scr[h, :, lanes] = jnp.exp(e1_scr[h, :, lanes] - top1[:, lanes])
                e2_scr[h, :, lanes] = _bf(jnp.exp(sc_scr[:, lanes] - top2[:, lanes]) * scale2[:, lanes])
        acc_scr[...] = jnp.zeros_like(acc_scr)

    nb1 = eblk // PEER_KEYS
    nlt = tokens // V7X_LANES
    pack = 2 * V7X_SUBLANES
    xu = _dot_nt(u_ref[0], xn_scr[...])
    act2 = xu + xu * jnp.tanh(xu * (GELU_C1 + GELU_C3 * (xu * xu)))
    zero = jnp.zeros((), bf16)
    parts = []
    for ip in range(nb1 // 2):
        i1s = [e * nb1 + 2 * ip + j for j in range(2)]
        nrows = [[n1_scr[h, pl.ds(i1, 1), :] for h in range(PEER_HEADS)] for i1 in i1s]
        erows = [[e1_scr[h, pl.ds(i1, 1), :] for h in range(PEER_HEADS)] for i1 in i1s]
        tiles = [[None] * nlt for _ in range(2)]
        for lt in range(nlt):
            lanes = slice(lt * V7X_LANES, (lt + 1) * V7X_LANES)
            g = [None, None]
            for h in range(PEER_HEADS):
                r2t = r2_scr[h, :, lanes]
                e2t = e2_scr[h, :, lanes]
                for j in range(2):
                    nb = _bf(jnp.broadcast_to(nrows[j][h][:, lanes], (pack, V7X_LANES)))
                    eb = _bf(jnp.broadcast_to(erows[j][h][:, lanes], (pack, V7X_LANES)))
                    nb = jnp.concatenate([nb] * (PEER_KEYS // pack), axis=0)
                    eb = jnp.concatenate([eb] * (PEER_KEYS // pack), axis=0)
                    hit = jnp.where(r2t <= nb, e2t * eb, zero)
                    g[j] = hit if g[j] is None else g[j] + hit
            for j in range(2):
                tiles[j][lt] = g[j]
        for j in range(2):
            il = 2 * ip + j
            gate = tiles[j][0] if nlt == 1 else jnp.concatenate(tiles[j], axis=1)
            parts.append(gate * _bf(act2[il * PEER_KEYS:(il + 1) * PEER_KEYS, :]))
    acc_scr[...] += _dot(vt_ref[0, 0], jnp.concatenate(parts, axis=0))

    @pl.when(e == pl.num_programs(1) - 1)
    def _():
        out_ref[...] = h_ref[...] + acc_scr[...].T


def _peer(h, lw, layer):
    rows, d = h.shape
    tokens = min(PEER_TOKENS, rows)
    assert tokens % V7X_LANES == 0 and rows % tokens == 0
    eblk = PEER_EBLK
    n_half = PEER_HEADS * 2
    vmem = (2 * (2 * tokens * d * 4 + d * d * 2 + 2 * eblk * d * 2)
            + tokens * d * 2 + 4 * PEER_HEADS * PEER_KEYS * tokens * 4 + d * tokens * 4
            + 5 * eblk * tokens * 4)
    return pl.pallas_call(
        functools.partial(_peer_kernel, tokens=tokens, eblk=eblk),
        out_shape=jax.ShapeDtypeStruct((rows, d), f32),
        grid=(rows // tokens, PEER_EXPERTS // eblk),
        in_specs=[pl.BlockSpec((tokens, d), lambda i, e: (i, 0)),
                  pl.BlockSpec((1, d), lambda i, e: (0, 0)),
                  pl.BlockSpec((d, PEER_HEADS * PEER_QDIM), lambda i, e: (0, 0)),
                  pl.BlockSpec((n_half, PEER_KEYS, PEER_QDIM // 2), lambda i, e: (0, 0, 0)),
                  pl.BlockSpec((1, eblk, d), lambda i, e: (layer, e, 0)),
                  pl.BlockSpec((1, 1, d, eblk), lambda i, e: (layer, e, 0, 0))],
        out_specs=pl.BlockSpec((tokens, d), lambda i, e: (i, 0)),
        scratch_shapes=[pltpu.VMEM((tokens, d), bf16),
                        pltpu.VMEM((PEER_HEADS, PEER_KEYS, tokens), f32),
                        pltpu.VMEM((PEER_HEADS, PEER_KEYS, tokens), f32),
                        pltpu.VMEM((PEER_HEADS, PEER_KEYS, tokens), bf16),
                        pltpu.VMEM((PEER_HEADS, PEER_KEYS, tokens), bf16),
                        pltpu.VMEM((PEER_KEYS, tokens), f32), pltpu.VMEM((PEER_KEYS, tokens), f32),
                        pltpu.VMEM((2 * PEER_TOPK, tokens), f32),
                        pltpu.VMEM((d, tokens), f32)],
        compiler_params=_cparams(("parallel", "arbitrary"), vmem),
        name="peer",
    )(h, lw["norm_ffn"], lw["peer_wq"], lw["peer_sk"], lw["peer_u"], lw["peer_vt"])


def _layer_weights(l, p):
    w_in = p["w_in"][l]
    o_b, o_c, o_g = A_COLS, A_COLS + B_COLS, A_COLS + B_COLS + C_COLS
    lane = jnp.arange(V7X_LANES)
    on_alpha = (lane >= DN_HEADS) & (lane < 2 * DN_HEADS)

    def alpha_lanes(vec):
        return jnp.where(on_alpha, jnp.pad(vec, (DN_HEADS, V7X_LANES - 2 * DN_HEADS)), 0.0)

    rw_vec = jnp.stack([p["rw_w0"][l], p["rw_a0"][l], p["rw_k_k"][l], p["rw_k_a"][l],
                        p["rw_r_k"][l].reshape(RW_WIDTH), p["rw_ln_w"][l], p["rw_ln_b"][l],
                        jnp.zeros((RW_WIDTH,), f32)])
    return {
        "norm_mix": p["norm_mix"][l],
        "w_a": _bf(w_in[:, 0:A_COLS]),
        "w_b": _bf(w_in[:, o_b:o_b + B_MAIN]),
        "w_bg": _bf(jnp.pad(w_in[:, o_b + B_MAIN:o_c], ((0, 0), (0, V7X_LANES - 2 * DN_HEADS)))),
        "w_c": _bf(w_in[:, o_c:o_g]),
        "w_g": _bf(w_in[:, o_g:]),
        "dn_conv_w": p["dn_conv_w"][l],
        "dn_prm": jnp.zeros((V7X_SUBLANES, V7X_LANES), f32)
                  .at[0].set(alpha_lanes(p["dn_a_log"][l])).at[1].set(alpha_lanes(p["dn_dt_bias"][l])),
        "dn_norm_w": jnp.tile(p["dn_norm_w"][l], DN_HEADS).reshape(1, DN_WIDTH),
        "bd128": _block_diag_ones(DN_WIDTH, DN_HD),
        "bd64": _block_diag_ones(RW_WIDTH, RW_HD),
        "rw_mu": p["rw_mu"][l].reshape(1, C_COLS),
        "rw_vec": rw_vec,
        "rw_w2": _bf(p["rw_w2"][l]), "rw_a2": _bf(p["rw_a2"][l]), "rw_g2": _bf(p["rw_g2"][l]),
        "w_br_a": _bf(p["w_br_a"][l]), "w_br_b": _bf(p["w_br_b"][l]), "w_br_c": _bf(p["w_br_c"][l]),
        "w_out": _bf(p["w_out"][l]),
        "norm_ffn": p["norm_ffn"][l].reshape(1, D_MODEL),
        "peer_wq": _bf(p["peer_wq"][l]),
        "peer_sk": _bf(p["peer_subkeys"][l].reshape(PEER_HEADS * 2, PEER_KEYS, PEER_QDIM // 2)),
        "peer_u": p["peer_u_bf"],
        "peer_vt": p["peer_vt_bf"],
    }


def _project(h, lw):
    nm = lw["norm_mix"]
    return {s: _norm_matmul(h, nm, lw["w_" + s]) for s in ("a", "b", "bg", "c", "g")}


def _kv_rows(pa, gi, lo, hi):
    k = pa[lo:hi, ATT_WIDTH + gi * ATT_OUT:ATT_WIDTH + (gi + 1) * ATT_OUT]
    v = pa[lo:hi, 2 * ATT_WIDTH + gi * ATT_OUT:2 * ATT_WIDTH + (gi + 1) * ATT_OUT]
    return jnp.stack([k, v], axis=1).reshape(hi - lo, 2, ATT_HPG, ATT_HD)


def _prompt_layer(h, lw, layer):
    t = h.shape[0]
    assert t % ATT_TILE == 0 and t % SEQ_TILE == 0
    pr = _project(h, lw)
    groups = [_attn_prompt_group(pr["a"], t, gi) for gi in range(len(ATT_GROUPS))]
    ob, dn_s = _dn_mixer(pr["b"], pr["bg"], jnp.zeros((1, V7X_SUBLANES, B_QKV), f32),
                         jnp.zeros((1, DN_HEADS, DN_HD, DN_HD), f32), lw,
                         nseq=1, tt=SEQ_TILE, chunk=DN_CHUNK, valid=(0, DN_CHUNK), nstate=1)
    oc, rw_s = _rw_mixer(pr["c"], jnp.zeros((1, 1, C_COLS), f32),
                         jnp.zeros((1, RW_HEADS, RW_HD, RW_HD), f32), lw,
                         nseq=1, tt=SEQ_TILE, chunk=RW_CHUNK, valid=(0, RW_CHUNK), nstate=1)
    h = _merge(h, [g[0] for g in groups], [g[1] for g in groups], ob, oc, pr["g"], lw)
    h = _peer(h, lw, layer)
    kvs = [_kv_rows(pr["a"], gi, t - min(win, t), t)[None] for gi, (win, _) in enumerate(ATT_GROUPS)]
    conv = pr["b"][t - (DN_CONV - 1):t, 0:B_QKV][None]
    shift = pr["c"][t - 1:t]
    return h, kvs, dn_s, conv, rw_s, shift


def _sample_rows(hist, x, nseq, seq):
    n = x.shape[1]
    nh = hist.shape[1]
    rows = jnp.concatenate([hist, x.reshape(nseq, seq, n), jnp.zeros((nseq, SAMPLE_ROWS - nh - seq, n), f32)], axis=1)
    return rows.reshape(nseq * SAMPLE_ROWS, n)


def _new_rows(x, nseq, nh, seq):
    return x.reshape(nseq, SAMPLE_ROWS, x.shape[1])[:, nh:nh + seq].reshape(nseq * seq, x.shape[1])


def _sample_layer(h, lw, layer, caches, cached, dn_state, dn_conv, rw_state, rw_shift, nseq, seq):
    pr = _project(h, lw)
    oa = _attn_sample(pr["a"], caches, cached, layer, nseq, seq)
    nh_dn, nh_rw = DN_CONV - 1, 1
    assert nseq % SAMPLE_STEP == 0 and nh_dn + seq <= SAMPLE_ROWS
    steps = nseq // SAMPLE_STEP
    tt = SAMPLE_STEP * SAMPLE_ROWS
    conv_hist = jnp.pad(dn_conv, ((0, 0), (0, 0), (0, B_MAIN - B_QKV)))
    ob, dn_s = _dn_mixer(_sample_rows(conv_hist, pr["b"], nseq, seq),
                         _sample_rows(jnp.zeros((nseq, nh_dn, V7X_LANES), f32), pr["bg"], nseq, seq),
                         jnp.zeros((steps, V7X_SUBLANES, B_QKV), f32), dn_state, lw,
                         nseq=steps, tt=tt, chunk=SAMPLE_ROWS, valid=(nh_dn, nh_dn + seq), nstate=SAMPLE_STEP)
    oc, rw_s = _rw_mixer(_sample_rows(rw_shift[:, None, :], pr["c"], nseq, seq),
                         jnp.zeros((steps, 1, C_COLS), f32), rw_state, lw,
                         nseq=steps, tt=tt, chunk=SAMPLE_ROWS, valid=(nh_rw, nh_rw + seq), nstate=SAMPLE_STEP)
    h = _merge(h, oa, None, _new_rows(ob, nseq, nh_dn, seq), _new_rows(oc, nseq, nh_rw, seq), pr["g"], lw)
    h = _peer(h, lw, layer)
    kvs = [_kv_rows(pr["a"], gi, 0, nseq * seq).reshape(nseq, seq, 2, ATT_HPG, ATT_HD)
           for gi in range(len(ATT_GROUPS))]
    conv = pr["b"][:, 0:B_QKV].reshape(nseq, seq, B_QKV)[:, seq - (DN_CONV - 1):]
    shift = pr["c"].reshape(nseq, seq, C_COLS)[:, seq - 1]
    return h, kvs, dn_s, conv, rw_s, shift


def kernel(x_prompt, x_sample, cache_kv_w128, cache_kv_w512, cache_kv_w2048, state_dn, state_dn_conv, state_rw, state_rw_shift, norm_mix, w_in, dn_conv_w, dn_a_log, dn_dt_bias, dn_norm_w, rw_mu, rw_w0, rw_w2, rw_a0, rw_a2, rw_g2, rw_k_k, rw_k_a, rw_r_k, rw_ln_w, rw_ln_b, w_br_a, w_br_b, w_br_c, w_out, norm_ffn, peer_wq, peer_subkeys, peer_u, peer_v, norm_final):
    p = dict(norm_mix=norm_mix, w_in=w_in, dn_conv_w=dn_conv_w, dn_a_log=dn_a_log, dn_dt_bias=dn_dt_bias,
             dn_norm_w=dn_norm_w, rw_mu=rw_mu, rw_w0=rw_w0, rw_w2=rw_w2, rw_a0=rw_a0, rw_a2=rw_a2, rw_g2=rw_g2,
             rw_k_k=rw_k_k, rw_k_a=rw_k_a, rw_r_k=rw_r_k, rw_ln_w=rw_ln_w, rw_ln_b=rw_ln_b, w_br_a=w_br_a,
             w_br_b=w_br_b, w_br_c=w_br_c, w_out=w_out, norm_ffn=norm_ffn, peer_wq=peer_wq,
             peer_subkeys=peer_subkeys, peer_u=peer_u, peer_v=peer_v)
    depth = w_in.shape[0]
    p["peer_u_bf"] = _bf(peer_u)
    p["peer_vt_bf"] = jnp.swapaxes(_bf(peer_v).reshape(depth, PEER_EXPERTS // PEER_EBLK, PEER_EBLK, D_MODEL), 2, 3)
    bp, t, d = x_prompt.shape
    nseq, seq, _ = x_sample.shape
    assert bp == 1 and d == D_MODEL and seq <= SAMPLE_ROWS // 2 and DN_CONV - 1 <= seq
    caches, cached = [], []
    for (win, dil), c in zip(ATT_GROUPS, (cache_kv_w128, cache_kv_w512, cache_kv_w2048)):
        assert c.shape[2] == win, "cached window must hold exactly `window` rows"
        cc, meta = _compact_cache(c, win, dil, seq)
        caches.append(cc)
        cached.append(meta)
    hp = x_prompt.reshape(t, d)
    hs = x_sample.reshape(nseq * seq, d)
    outs_p, outs_s = [], []
    for l in range(depth):
        lw = _layer_weights(l, p)
        hp, *st_p = _prompt_layer(hp, lw, l)
        hs, *st_s = _sample_layer(hs, lw, l, caches, cached, state_dn[l], state_dn_conv[l], state_rw[l],
                                  state_rw_shift[l], nseq, seq)
        outs_p.append(st_p)
        outs_s.append(st_s)
    y_p = _rmsnorm(hp, norm_final).reshape(bp, t, d)
    y_s = _rmsnorm(hs, norm_final).reshape(nseq, seq, d)

    def stack(outs, pick):
        return jnp.stack([pick(o) for o in outs], axis=0)

    res = [y_p, y_s]
    for gi in range(len(ATT_GROUPS)):
        res.append(stack(outs_p, lambda o: o[0][gi]))
        res.append(stack(outs_s, lambda o: o[0][gi]))
    for idx in (1, 2, 3, 4):
        res.append(stack(outs_p, lambda o: o[idx]))
        res.append(stack(outs_s, lambda o: o[idx]))
    return tuple(res)
```

```python
import functools
import math

import jax
import jax.numpy as jnp
import numpy as np
from jax import lax
from jax.experimental import pallas as pl
from jax.experimental.pallas import tpu as pltpu

f32 = jnp.float32
bf16 = jnp.bfloat16

V7X_LANES = 128
V7X_SUBLANES = 8
V7X_VMEM_BYTES = 64 * 1024 * 1024
VMEM_CEILING = 56 * 1024 * 1024

D_MODEL = 1024
ATT_GROUPS = ((128, 1), (512, 4), (2048, 16))
ATT_HPG = 4
ATT_HD = 64
ATT_HEADS = ATT_HPG * len(ATT_GROUPS)
ATT_WIDTH = ATT_HEADS * ATT_HD
ATT_OUT = ATT_HPG * ATT_HD
ATT_STEPS = 128
DN_HEADS = 4
DN_HD = 128
DN_WIDTH = DN_HEADS * DN_HD
DN_CONV = 4
DN_CHUNK = 64
DN_GROUP = 4
RW_HEADS = 8
RW_HD = 64
RW_WIDTH = RW_HEADS * RW_HD
RW_DECAY_LORA = 64
RW_A_LORA = 64
RW_GATE_LORA = 128
RW_GN_EPS = 64e-5
RW_CHUNK = 64
RW_GROUP = 2
PEER_KEYS = 128
PEER_EXPERTS = PEER_KEYS * PEER_KEYS
PEER_HEADS = 8
PEER_QDIM = 128
PEER_TOPK = 16
RMS_EPS = 1e-6
L2_EPS = 1e-6

A_COLS = 3 * ATT_WIDTH
B_QKV = 3 * DN_WIDTH
B_MAIN = 4 * DN_WIDTH
B_COLS = B_MAIN + 2 * DN_HEADS
C_COLS = 3 * RW_WIDTH + RW_DECAY_LORA + RW_A_LORA + RW_GATE_LORA
G_COLS = 3 * D_MODEL

ROW_TILE = 512
ATT_TILE = 2048
ATT_GROUP = 4
SEQ_TILE = 512
SAMPLE_ROWS = 8
SAMPLE_STEP = 8
PEER_TOKENS = 512
PEER_EBLK = 1024

NEG_INF = float("-inf")
RANK_CODE = 2.0 ** 100
GELU_C1 = math.sqrt(2.0 / math.pi)
GELU_C3 = 0.044715 * GELU_C1


def _alibi_slopes():
    h = np.arange(1, ATT_HEADS + 1, dtype=np.float32)
    return np.power(np.float32(2.0), -8.0 * h / ATT_HEADS).astype(np.float32)


def _cparams(semantics, vmem_bytes):
    return pltpu.CompilerParams(dimension_semantics=semantics,
                                vmem_limit_bytes=int(min(max(vmem_bytes, 16 * 1024 * 1024), VMEM_CEILING)))


def _dot(a, b):
    return jnp.dot(a, b, preferred_element_type=f32)


def _dot_nt(a, b):
    return lax.dot_general(a, b, (((1,), (1,)), ((), ())), preferred_element_type=f32)


def _dot_tn(a, b):
    return lax.dot_general(a, b, (((0,), (0,)), ((), ())), preferred_element_type=f32)


def _bf(x):
    return x.astype(bf16)


def _sigmoid(x):
    return 1.0 / (1.0 + jnp.exp(-x))


def _segsum(x, bd):
    hi = _bf(x)
    lo = _bf(x - hi.astype(f32))
    return _dot(hi, bd) + _dot(lo, bd)


def _seg_cumsum(x, rowc, seg):
    k = 1
    while k < seg:
        x = x + jnp.where(rowc >= k, pltpu.roll(x, k, 0), 0.0)
        k *= 2
    return x


def _block_diag_ones(width, seg):
    i = np.arange(width)
    return jnp.asarray((i[:, None] // seg) == (i[None, :] // seg), dtype=bf16)


def _norm_matmul_kernel(h_ref, nw_ref, w_ref, o_ref):
    x = h_ref[...]
    xn = x * lax.rsqrt(jnp.mean(x * x, axis=-1, keepdims=True) + RMS_EPS) * nw_ref[...]
    o_ref[...] = _dot(_bf(xn), w_ref[...])


def _norm_matmul(h, nw, w):
    rows, d = h.shape
    n = w.shape[1]
    tr = min(ROW_TILE, rows)
    vmem = 2 * (tr * d * 4 + d * n * 2 + tr * n * 4) + 4 * tr * d * 4
    return pl.pallas_call(
        _norm_matmul_kernel,
        out_shape=jax.ShapeDtypeStruct((rows, n), f32),
        grid=(rows // tr,),
        in_specs=[pl.BlockSpec((tr, d), lambda i: (i, 0)),
                  pl.BlockSpec((1, d), lambda i: (0, 0)),
                  pl.BlockSpec((d, n), lambda i: (0, 0))],
        out_specs=pl.BlockSpec((tr, n), lambda i: (i, 0)),
        compiler_params=_cparams(("parallel",), vmem),
        name="norm_matmul",
    )(h, nw.reshape(1, d), w)


def _rmsnorm_kernel(h_ref, nw_ref, o_ref):
    x = h_ref[...]
    o_ref[...] = x * lax.rsqrt(jnp.mean(x * x, axis=-1, keepdims=True) + RMS_EPS) * nw_ref[...]


def _rmsnorm(h, nw):
    rows, d = h.shape
    tr = min(ROW_TILE, rows)
    return pl.pallas_call(
        _rmsnorm_kernel,
        out_shape=jax.ShapeDtypeStruct((rows, d), f32),
        grid=(rows // tr,),
        in_specs=[pl.BlockSpec((tr, d), lambda i: (i, 0)), pl.BlockSpec((1, d), lambda i: (0, 0))],
        out_specs=pl.BlockSpec((tr, d), lambda i: (i, 0)),
        compiler_params=_cparams(("parallel",), 6 * tr * d * 4),
        name="final_rmsnorm",
    )(h, nw.reshape(1, d))


def _attn_prompt_kernel(q_ref, kc_ref, vc_ref, kp_ref, vp_ref, o_ref, lse_ref, kk_scr, vv_scr, *, dil, slopes):
    n = ATT_STEPS
    nblk = ATT_TILE // (dil * n)
    tile = pl.program_id(0)
    pair = pl.program_id(1)
    kk_scr[0:ATT_TILE, :] = kp_ref[...]
    kk_scr[ATT_TILE:2 * ATT_TILE, :] = kc_ref[...]
    vv_scr[0:ATT_TILE, :] = vp_ref[...]
    vv_scr[ATT_TILE:2 * ATT_TILE, :] = vc_ref[...]
    ii = lax.broadcasted_iota(jnp.int32, (n, 2 * n), 0)
    jj = lax.broadcasted_iota(jnp.int32, (n, 2 * n), 1)
    steps = n + ii - jj
    band = (steps >= 0) & (steps <= n)
    dist = (steps * dil).astype(f32)
    biases = []
    for hh in range(2):
        slope = jnp.where(pair == 0, slopes[hh], slopes[2 + hh])
        biases.append(jnp.where(band, -slope * dist, NEG_INF))

    def body(gidx, carry):
        items = []
        for cc in range(ATT_GROUP):
            c = gidx * ATT_GROUP + cc
            r = c % dil
            b = c // dil
            qs = r + dil * n * b
            rows_q = pl.ds(qs, n, stride=dil)
            rows_k = pl.ds(ATT_TILE + qs - dil * n, 2 * n, stride=dil)
            q = q_ref[rows_q, :] * (ATT_HD ** -0.5)
            k = kk_scr[rows_k, :]
            first_key = jnp.where(tile * nblk + b == 0, n, 0)
            ss = [_dot_nt(_bf(q[:, hh * ATT_HD:(hh + 1) * ATT_HD]), _bf(k[:, hh * ATT_HD:(hh + 1) * ATT_HD]))
                  for hh in range(2)]
            items.append(dict(rows_q=rows_q, rows_k=rows_k, first_key=first_key, s=ss))
        for it in items:
            it["p"], it["den"], it["lse"] = [], [], []
            for hh in range(2):
                s = jnp.where(jj >= it["first_key"], it["s"][hh] + biases[hh], NEG_INF)
                mx = jnp.max(s, axis=-1, keepdims=True)
                p = jnp.exp(s - mx)
                den = jnp.sum(p, axis=-1, keepdims=True)
                it["p"].append(_bf(p))
                it["den"].append(den)
                it["lse"].append(jnp.broadcast_to(mx + jnp.log(den), (n, ATT_HD)))
        for it in items:
            v = vv_scr[it["rows_k"], :]
            outs = [_dot(it["p"][hh], _bf(v[:, hh * ATT_HD:(hh + 1) * ATT_HD])) / it["den"][hh] for hh in range(2)]
            o_ref[it["rows_q"], :] = jnp.concatenate(outs, axis=1)
            lse_ref[it["rows_q"], :] = jnp.concatenate(it["lse"], axis=1)
        return carry

    lax.fori_loop(0, dil * nblk // ATT_GROUP, body, 0)


def _attn_prompt_group(pa, seq_len, gi):
    _, dil = ATT_GROUPS[gi]
    slopes = tuple(float(s) for s in _alibi_slopes()[gi * ATT_HPG:(gi + 1) * ATT_HPG])
    w = V7X_LANES
    qb, kb, vb = (gi * 2, 6 + gi * 2, 12 + gi * 2)
    blk = (ATT_TILE, w)
    cur = lambda base: pl.BlockSpec(blk, lambda i, j, base=base: (i, base + j))
    prev = lambda base: pl.BlockSpec(blk, lambda i, j, base=base: (jnp.maximum(i - 1, 0), base + j))
    vmem = 2 * 7 * ATT_TILE * w * 4 + 2 * 2 * ATT_TILE * w * 4 + 8 * 1024 * 1024
    return pl.pallas_call(
        functools.partial(_attn_prompt_kernel, dil=dil, slopes=slopes),
        out_shape=(jax.ShapeDtypeStruct((seq_len, ATT_OUT), f32), jax.ShapeDtypeStruct((seq_len, ATT_OUT), f32)),
        grid=(seq_len // ATT_TILE, 2),
        in_specs=[cur(qb), cur(kb), cur(vb), prev(kb), prev(vb)],
        out_specs=(pl.BlockSpec(blk, lambda i, j: (i, j)), pl.BlockSpec(blk, lambda i, j: (i, j))),
        scratch_shapes=[pltpu.VMEM((2 * ATT_TILE, w), f32), pltpu.VMEM((2 * ATT_TILE, w), f32)],
        compiler_params=_cparams(("parallel", "parallel"), vmem),
        name=f"attn_prompt_g{gi}",
    )(pa, pa, pa, pa, pa)


def _attn_sample_kernel(q_ref, k_ref, v_ref, c0_ref, c1_ref, c2_ref, o_ref, *, slopes, seq):
    rows = 2 * seq
    caches = (c0_ref, c1_ref, c2_ref)
    row = lax.broadcasted_iota(jnp.int32, (rows, 1), 0)
    own0 = row < seq
    qpos = row % seq
    npos = lax.broadcasted_iota(jnp.int32, (rows, rows), 1)
    q_all = q_ref[0] * (ATT_HD ** -0.5)
    k_all = k_ref[0]
    v_all = v_ref[0]
    outs, lses = [], []
    for gi, (win, dil) in enumerate(ATT_GROUPS):
        cpos = lax.broadcasted_iota(jnp.int32, (rows, win), 1)
        dist_c = win + qpos - cpos
        ok_c = (dist_c <= win) & ((dist_c & (dil - 1)) == 0)
        dist_n = row - npos
        ok_n = (dist_n >= 0) & (dist_n <= qpos) & ((dist_n & (dil - 1)) == 0)
        dcf = dist_c.astype(f32)
        dnf = dist_n.astype(f32)
        for h in range(ATT_HPG):
            slope = slopes[gi * ATT_HPG + h]
            cols = slice(gi * ATT_OUT + h * ATT_HD, gi * ATT_OUT + (h + 1) * ATT_HD)
            kcols = slice(h * ATT_HD, (h + 1) * ATT_HD)
            vcols = slice(ATT_OUT + h * ATT_HD, ATT_OUT + (h + 1) * ATT_HD)
            qh = _bf(q_all[:, cols])
            sn = jnp.where(ok_n, _dot_nt(qh, _bf(k_all[:, cols])) - slope * dnf, NEG_INF)
            scs = []
            for b in range(2):
                sc = _dot_nt(qh, _bf(caches[gi][0, b, :, kcols])) - slope * dcf
                scs.append(jnp.where(ok_c, sc, NEG_INF))
            sc = jnp.where(own0, scs[0], scs[1])
            mx = jnp.maximum(jnp.max(sc, axis=-1, keepdims=True), jnp.max(sn, axis=-1, keepdims=True))
            pc = jnp.exp(sc - mx)
            pn = jnp.exp(sn - mx)
            den = jnp.sum(pc, axis=-1, keepdims=True) + jnp.sum(pn, axis=-1, keepdims=True)
            pcb = _bf(pc)
            oc = jnp.where(own0, _dot(pcb, _bf(caches[gi][0, 0, :, vcols])), _dot(pcb, _bf(caches[gi][0, 1, :, vcols])))
            o = (oc + _dot(_bf(pn), _bf(v_all[:, cols]))) / den
            outs.append(o)
            lses.append(mx + jnp.log(den))
    merged = []
    for h in range(ATT_HPG):
        ls = [lses[gi * ATT_HPG + h] for gi in range(len(ATT_GROUPS))]
        mx = jnp.maximum(jnp.maximum(ls[0], ls[1]), ls[2])
        es = [jnp.exp(l - mx) for l in ls]
        den = es[0] + es[1] + es[2]
        acc = es[0] * outs[h] + es[1] * outs[ATT_HPG + h] + es[2] * outs[2 * ATT_HPG + h]
        merged.append(acc / den)
    o_ref[0] = jnp.concatenate(merged, axis=1)


def _attn_sample(pa, caches, layer, nseq, seq):
    rows = 2 * seq
    assert rows == SAMPLE_ROWS and nseq % 2 == 0
    slopes = tuple(float(s) for s in _alibi_slopes())
    pa3 = pa.reshape(nseq // 2, rows, A_COLS)
    in_specs = [pl.BlockSpec((1, rows, ATT_WIDTH), lambda i, c=c: (i, 0, c)) for c in range(3)]
    vmem = 0
    for (win, _), c in zip(ATT_GROUPS, caches):
        assert c.shape[2] == win, "cached window must hold exactly `window` rows"
        in_specs.append(pl.BlockSpec((1, 2, win, 2 * ATT_OUT), lambda i: (layer, i, 0, 0)))
        vmem += 2 * 2 * win * 2 * ATT_OUT * 4
    out = pl.pallas_call(
        functools.partial(_attn_sample_kernel, slopes=slopes, seq=seq),
        out_shape=jax.ShapeDtypeStruct((nseq // 2, rows, ATT_OUT), f32),
        grid=(nseq // 2,),
        in_specs=in_specs,
        out_specs=pl.BlockSpec((1, rows, ATT_OUT), lambda i: (i, 0, 0)),
        compiler_params=_cparams(("parallel",), vmem + 16 * 1024 * 1024),
        name="attn_sample",
    )(pa3, pa3, pa3, *caches)
    return out.reshape(nseq * seq, ATT_OUT)


def _dn_kernel(pb_ref, pbg_ref, conv0_ref, s0_ref, cw_ref, prm_ref, nw_ref, bd_ref,
               o_ref, sout_ref,
               xs_scr, q_scr, k_scr, v_scr, g_scr, b_scr, oc_scr, qp_scr, m_scr, n_scr, s_scr,
               *, tt, chunk, valid, nstate):
    t = pl.program_id(1)

    @pl.when(t == 0)
    def _():
        xs_scr[0:V7X_SUBLANES, :] = conv0_ref[0]
        if nstate == 1:
            s_scr[...] = s0_ref[0]

    x = pb_ref[:, 0:B_QKV]
    xs_scr[pl.ds(V7X_SUBLANES, tt), :] = x
    cw = cw_ref[...]
    y = x * cw[3:4, :]
    for j in range(DN_CONV - 1):
        y = y + xs_scr[pl.ds(V7X_SUBLANES - (DN_CONV - 1) + j, tt), :] * cw[j:j + 1, :]
    xs_scr[0:V7X_SUBLANES, :] = xs_scr[pl.ds(tt, V7X_SUBLANES), :]
    y = y * _sigmoid(y)
    bd = bd_ref[...]
    q = y[:, 0:DN_WIDTH]
    k = y[:, DN_WIDTH:2 * DN_WIDTH]
    q_scr[...] = q * lax.rsqrt(_segsum(q * q, bd) + L2_EPS) * (DN_HD ** -0.5)
    k_scr[...] = k * lax.rsqrt(_segsum(k * k, bd) + L2_EPS)
    v_scr[...] = y[:, 2 * DN_WIDTH:3 * DN_WIDTH]
    pg = pbg_ref[...]
    beta = _sigmoid(pg)
    g = -jnp.exp(prm_ref[0:1, :]) * jax.nn.softplus(pg + prm_ref[1:2, :])
    rown = lax.broadcasted_iota(jnp.int32, (tt, V7X_LANES), 0)
    if valid != (0, chunk):
        rc = rown & (chunk - 1)
        beta = jnp.where(rc >= valid[0], jnp.where(rc < valid[1], beta, 0.0), 0.0)
        g = jnp.where(rc >= valid[0], jnp.where(rc < valid[1], g, 0.0), 0.0)
    g_scr[...] = _seg_cumsum(g, rown & (chunk - 1), chunk)
    b_scr[...] = beta

    ci = lax.broadcasted_iota(jnp.int32, (chunk, chunk), 0)
    cj = lax.broadcasted_iota(jnp.int32, (chunk, chunk), 1)
    incl = ci >= cj
    strict = ci > cj

    nchunks = tt // chunk
    group = min(DN_GROUP, nchunks)

    def precompute(gi, carry):
        items = []
        for cc in range(group):
            cidx = gi * group + cc
            rows = pl.ds(pl.multiple_of(cidx * chunk, chunk), chunk)
            gc = g_scr[rows, :]
            bc = b_scr[rows, :]
            gct = gc.T
            for h in range(DN_HEADS):
                cols = slice(h * DN_HD, (h + 1) * DN_HD)
                qh, kh, vh = q_scr[rows, cols], k_scr[rows, cols], v_scr[rows, cols]
                gcol = gc[:, DN_HEADS + h:DN_HEADS + h + 1]
                grow = gct[DN_HEADS + h:DN_HEADS + h + 1, :]
                bcol = bc[:, h:h + 1]
                glast = gc[chunk - 1:chunk, DN_HEADS + h:DN_HEADS + h + 1]
                decay = jnp.exp(jnp.where(incl, gcol - grow, NEG_INF))
                kb = kh * bcol
                eg = jnp.exp(gcol)
                both = _dot_nt(_bf(jnp.concatenate([kb, qh], axis=0)), _bf(kh))
                items.append(dict(
                    rows=rows, cols=cols, idx=cidx * DN_HEADS + h,
                    p=jnp.where(strict, -both[0:chunk] * decay, 0.0),
                    attn=jnp.where(incl, both[chunk:2 * chunk] * decay, 0.0),
                    sol=jnp.concatenate([vh * bcol, kb * eg], axis=1),
                    qeg=qh * eg, kdec=_bf(kh * jnp.exp(glast - gcol))))
        k = 1
        while k < chunk:
            for it in items:
                it["sol"] = it["sol"] + _dot(_bf(it["p"]), _bf(it["sol"]))
            if 2 * k < chunk:
                for it in items:
                    pb = _bf(it["p"])
                    it["p"] = _dot(pb, pb)
            k *= 2
        for it in items:
            u = _bf(it["sol"][:, 0:DN_HD])
            w = _bf(it["sol"][:, DN_HD:2 * DN_HD])
            attn = _bf(it["attn"])
            qp_scr[it["rows"], it["cols"]] = it["qeg"] - _dot(attn, w)
            oc_scr[it["rows"], it["cols"]] = _dot(attn, u)
            m_scr[it["idx"]] = _dot_tn(it["kdec"], w)
            n_scr[it["idx"]] = _dot_tn(it["kdec"], u)
        return carry

    lax.fori_loop(0, nchunks // group, precompute, 0)

    def recur(c, carry):
        rows = pl.ds(pl.multiple_of(c * chunk, chunk), chunk)
        g_tail = g_scr[pl.ds(pl.multiple_of((c + 1) * chunk - V7X_SUBLANES, V7X_SUBLANES), V7X_SUBLANES), :]
        states = [s_scr[h] if nstate == 1 else s0_ref[c, h] for h in range(DN_HEADS)]
        sb = [_bf(s) for s in states]
        outs = [_dot(_bf(qp_scr[rows, h * DN_HD:(h + 1) * DN_HD]), sb[h]) for h in range(DN_HEADS)]
        upd = [_dot(_bf(m_scr[c * DN_HEADS + h]), sb[h]) for h in range(DN_HEADS)]
        for h in range(DN_HEADS):
            cols = slice(h * DN_HD, (h + 1) * DN_HD)
            oc_scr[rows, cols] = oc_scr[rows, cols] + outs[h]
            g_end = jnp.exp(g_tail[V7X_SUBLANES - 1:V7X_SUBLANES, DN_HEADS + h:DN_HEADS + h + 1])
            new = states[h] * g_end - upd[h] + n_scr[c * DN_HEADS + h]
            if nstate == 1:
                s_scr[h] = new
            else:
                sout_ref[c, h] = new
        return carry

    lax.fori_loop(0, nchunks, recur, 0)

    o = oc_scr[...]
    z = pb_ref[:, B_QKV:B_MAIN]
    o = o * lax.rsqrt(_segsum(o * o, bd) * (1.0 / DN_HD) + RMS_EPS) * nw_ref[...]
    o_ref[...] = o * (z * _sigmoid(z))

    if nstate == 1:
        @pl.when(t == pl.num_programs(1) - 1)
        def _():
            sout_ref[0] = s_scr[...]


def _dn_mixer(pb, pbg, conv0, s0, lw, *, nseq, tt, chunk, valid, nstate):
    rows = pb.shape[0]
    nt = rows // (nseq * tt)
    nmat = (tt // chunk) * DN_HEADS
    vmem = (2 * (tt * (B_MAIN + V7X_LANES + DN_WIDTH) * 4 + DN_WIDTH * DN_WIDTH * 2)
            + (tt + 8) * B_QKV * 4 + 4 * tt * DN_WIDTH * 4 + 2 * tt * V7X_LANES * 4
            + tt * DN_WIDTH * 4 + (2 * nmat + (6 + 4 * nstate) * DN_HEADS) * DN_HD * DN_HD * 4 + 8 * tt * B_QKV * 4)
    return pl.pallas_call(
        functools.partial(_dn_kernel, tt=tt, chunk=chunk, valid=valid, nstate=nstate),
        out_shape=(jax.ShapeDtypeStruct((rows, DN_WIDTH), f32),
                   jax.ShapeDtypeStruct((nseq * nstate, DN_HEADS, DN_HD, DN_HD), f32)),
        grid=(nseq, nt),
        in_specs=[pl.BlockSpec((tt, B_MAIN), lambda s, t: (s * nt + t, 0)),
                  pl.BlockSpec((tt, V7X_LANES), lambda s, t: (s * nt + t, 0)),
                  pl.BlockSpec((1, V7X_SUBLANES, B_QKV), lambda s, t: (s, 0, 0)),
                  pl.BlockSpec((nstate, DN_HEADS, DN_HD, DN_HD), lambda s, t: (s, 0, 0, 0)),
                  pl.BlockSpec((DN_CONV, B_QKV), lambda s, t: (0, 0)),
                  pl.BlockSpec((V7X_SUBLANES, V7X_LANES), lambda s, t: (0, 0)),
                  pl.BlockSpec((1, DN_WIDTH), lambda s, t: (0, 0)),
                  pl.BlockSpec((DN_WIDTH, DN_WIDTH), lambda s, t: (0, 0))],
        out_specs=(pl.BlockSpec((tt, DN_WIDTH), lambda s, t: (s * nt + t, 0)),
                   pl.BlockSpec((nstate, DN_HEADS, DN_HD, DN_HD), lambda s, t: (s, 0, 0, 0))),
        scratch_shapes=[pltpu.VMEM((tt + V7X_SUBLANES, B_QKV), f32),
                        pltpu.VMEM((tt, DN_WIDTH), f32), pltpu.VMEM((tt, DN_WIDTH), f32),
                        pltpu.VMEM((tt, DN_WIDTH), f32),
                        pltpu.VMEM((tt, V7X_LANES), f32), pltpu.VMEM((tt, V7X_LANES), f32),
                        pltpu.VMEM((tt, DN_WIDTH), f32), pltpu.VMEM((tt, DN_WIDTH), f32),
                        pltpu.VMEM((nmat, DN_HD, DN_HD), f32), pltpu.VMEM((nmat, DN_HD, DN_HD), f32),
                        pltpu.VMEM((DN_HEADS, DN_HD, DN_HD), f32)],
        compiler_params=_cparams(("parallel", "arbitrary"), vmem),
        name="deltanet",
    )(pb, pbg, conv0, s0, lw["dn_conv_w"], lw["dn_prm"], lw["dn_norm_w"], lw["bd128"])


def _rw_kernel(pc_ref, shift0_ref, s0_ref, mu_ref, vec_ref, w2_ref, a2_ref, g2_ref, bd_ref,
               o_ref, sout_ref,
               xs_scr, ar_scr, bk_scr, v_scr, pe_scr, oc_scr, rp_scr, m_scr, n_scr, s_scr, *, tt, chunk, valid, nstate):
    t = pl.program_id(1)

    @pl.when(t == 0)
    def _():
        xs_scr[V7X_SUBLANES - 1:V7X_SUBLANES, :] = shift0_ref[0]
        if nstate == 1:
            s_scr[...] = s0_ref[0]

    pc = pc_ref[...]
    xs_scr[pl.ds(V7X_SUBLANES, tt), :] = pc
    prev = xs_scr[pl.ds(V7X_SUBLANES - 1, tt), :]
    xs_scr[0:V7X_SUBLANES, :] = xs_scr[pl.ds(tt, V7X_SUBLANES), :]
    xc = pc + (prev - pc) * mu_ref[...]
    w3 = 3 * RW_WIDTH
    r = xc[:, 0:RW_WIDTH]
    k = xc[:, RW_WIDTH:2 * RW_WIDTH]
    v = xc[:, 2 * RW_WIDTH:w3]
    wd = xc[:, w3:w3 + RW_DECAY_LORA]
    ad = xc[:, w3 + RW_DECAY_LORA:w3 + RW_DECAY_LORA + RW_A_LORA]
    gd = xc[:, w3 + RW_DECAY_LORA + RW_A_LORA:C_COLS]
    w0, a0, k_k, k_a, r_k, ln_w, ln_b = (vec_ref[i:i + 1, :] for i in range(7))
    w_log = -jax.nn.softplus(-(w0 + _dot(_bf(jnp.tanh(wd)), w2_ref[...]))) - 0.5
    lw = -jnp.exp(w_log)
    a = _sigmoid(a0 + _dot(_bf(ad), a2_ref[...]))
    gate = _dot(_bf(_sigmoid(gd)), g2_ref[...])
    bd = bd_ref[...]
    kkr = k * k_k
    kk = kkr * lax.rsqrt(_segsum(kkr * kkr, bd) + L2_EPS)
    k2 = k * (1.0 + (a - 1.0) * k_a)
    bonus = _segsum(r * k2 * r_k, bd) * v
    rown = lax.broadcasted_iota(jnp.int32, (tt, RW_WIDTH), 0)
    if valid != (0, chunk):
        rc = rown & (chunk - 1)
        pad = jnp.where(rc >= valid[0], jnp.where(rc < valid[1], 0, 1), 1) == 1
        lw = jnp.where(pad, 0.0, lw)
        kk = jnp.where(pad, 0.0, kk)
        k2 = jnp.where(pad, 0.0, k2)
    gcum = _seg_cumsum(lw, rown & (chunk - 1), chunk)
    e_pos = jnp.exp(gcum)
    e_neg = jnp.exp(-gcum)
    ar_scr[:, 0:RW_WIDTH] = -kk * jnp.exp(gcum - lw)
    ar_scr[:, RW_WIDTH:2 * RW_WIDTH] = r * e_pos
    bk_scr[:, 0:RW_WIDTH] = kk * a * e_neg
    bk_scr[:, RW_WIDTH:2 * RW_WIDTH] = k2 * e_neg
    v_scr[...] = v
    pe_scr[...] = e_pos

    ci = lax.broadcasted_iota(jnp.int32, (2 * chunk, 2 * chunk), 0)
    cj = lax.broadcasted_iota(jnp.int32, (2 * chunk, 2 * chunk), 1)
    keep = jnp.where(ci < chunk, ci - 1, ci - chunk) >= (cj & (chunk - 1))

    nchunks = tt // chunk
    group = min(RW_GROUP, nchunks)

    def precompute(gi, carry):
        items = []
        for cc in range(group):
            cidx = gi * group + cc
            rows = pl.ds(pl.multiple_of(cidx * chunk, chunk), chunk)
            p_tail = pe_scr[pl.ds(pl.multiple_of((cidx + 1) * chunk - V7X_SUBLANES, V7X_SUBLANES), V7X_SUBLANES), :]
            for h in range(RW_HEADS):
                cols = slice(h * RW_HD, (h + 1) * RW_HD)
                cols2 = slice(RW_WIDTH + h * RW_HD, RW_WIDTH + (h + 1) * RW_HD)
                at, rt = ar_scr[rows, cols], ar_scr[rows, cols2]
                bt, kt = bk_scr[rows, cols], bk_scr[rows, cols2]
                bkb = _bf(jnp.concatenate([bt, kt], axis=0))
                m = jnp.where(keep, _dot_nt(_bf(jnp.concatenate([at, rt], axis=0)), bkb), 0.0)
                items.append(dict(rows=rows, cols=cols, idx=cidx * RW_HEADS + h, at=at, rt=rt, bt=_bf(bt), bkb=bkb,
                                  vh=v_scr[rows, cols], p=m[0:chunk, 0:chunk], aak=_bf(m[0:chunk, chunk:2 * chunk]),
                                  mlow=_bf(m[chunk:2 * chunk, :]), p_end=p_tail[V7X_SUBLANES - 1:V7X_SUBLANES, cols]))
        for it in items:
            it["sol"] = jnp.concatenate([it["at"], _dot(it["aak"], _bf(it["vh"]))], axis=1)
        k = 1
        while k < chunk:
            for it in items:
                it["sol"] = it["sol"] + _dot(_bf(it["p"]), _bf(it["sol"]))
            if 2 * k < chunk:
                for it in items:
                    pb = _bf(it["p"])
                    it["p"] = _dot(pb, pb)
            k *= 2
        for it in items:
            wm = _bf(it["sol"][:, 0:RW_HD])
            uv = _bf(jnp.concatenate([it["sol"][:, RW_HD:2 * RW_HD], it["vh"]], axis=0))
            rp_scr[it["rows"], it["cols"]] = it["rt"] + _dot(it["mlow"][:, 0:chunk], wm)
            oc_scr[it["rows"], it["cols"]] = _dot(it["mlow"], uv)
            m_scr[it["idx"]] = _dot_tn(wm, it["bt"]) * it["p_end"]
            n_scr[it["idx"]] = _dot_tn(uv, it["bkb"]) * it["p_end"]
        return carry

    lax.fori_loop(0, nchunks // group, precompute, 0)

    def recur(c, carry):
        rows = pl.ds(pl.multiple_of(c * chunk, chunk), chunk)
        p_tail = pe_scr[pl.ds(pl.multiple_of((c + 1) * chunk - V7X_SUBLANES, V7X_SUBLANES), V7X_SUBLANES), :]
        states = [s_scr[h] if nstate == 1 else s0_ref[c, h] for h in range(RW_HEADS)]
        sb = [_bf(s) for s in states]
        outs = [_dot_nt(_bf(rp_scr[rows, h * RW_HD:(h + 1) * RW_HD]), sb[h]) for h in range(RW_HEADS)]
        upd = [_dot(sb[h], _bf(m_scr[c * RW_HEADS + h])) for h in range(RW_HEADS)]
        for h in range(RW_HEADS):
            cols = slice(h * RW_HD, (h + 1) * RW_HD)
            oc_scr[rows, cols] = oc_scr[rows, cols] + outs[h]
            new = states[h] * p_tail[V7X_SUBLANES - 1:V7X_SUBLANES, cols] + upd[h] + n_scr[c * RW_HEADS + h]
            if nstate == 1:
                s_scr[h] = new
            else:
                sout_ref[c, h] = new
        return carry

    lax.fori_loop(0, nchunks, recur, 0)

    o = oc_scr[...]
    mean = _segsum(o, bd) * (1.0 / RW_HD)
    d = o - mean
    var = _segsum(d * d, bd) * (1.0 / RW_HD)
    o = d * lax.rsqrt(var + RW_GN_EPS) * ln_w + ln_b
    o_ref[...] = (o + bonus) * gate

    if nstate == 1:
        @pl.when(t == pl.num_programs(1) - 1)
        def _():
            sout_ref[0] = s_scr[...]


def _rw_mixer(pc, shift0, s0, lw, *, nseq, tt, chunk, valid, nstate):
    rows = pc.shape[0]
    nt = rows // (nseq * tt)
    nmat = (tt // chunk) * RW_HEADS
    vmem = (2 * (tt * (C_COLS + RW_WIDTH) * 4 + RW_WIDTH * RW_WIDTH * 2 + 4 * RW_WIDTH * V7X_LANES * 2)
            + (tt + 8) * C_COLS * 4 + 8 * tt * RW_WIDTH * 4 + (2 * nmat + (6 + 4 * nstate) * RW_HEADS) * RW_HD * V7X_LANES * 4
            + 14 * tt * RW_WIDTH * 4)
    const = lambda shape: pl.BlockSpec(shape, lambda s, t: (0,) * len(shape))
    return pl.pallas_call(
        functools.partial(_rw_kernel, tt=tt, chunk=chunk, valid=valid, nstate=nstate),
        out_shape=(jax.ShapeDtypeStruct((rows, RW_WIDTH), f32),
                   jax.ShapeDtypeStruct((nseq * nstate, RW_HEADS, RW_HD, RW_HD), f32)),
        grid=(nseq, nt),
        in_specs=[pl.BlockSpec((tt, C_COLS), lambda s, t: (s * nt + t, 0)),
                  pl.BlockSpec((1, 1, C_COLS), lambda s, t: (s, 0, 0)),
                  pl.BlockSpec((nstate, RW_HEADS, RW_HD, RW_HD), lambda s, t: (s, 0, 0, 0)),
                  const((1, C_COLS)), const((V7X_SUBLANES, RW_WIDTH)),
                  const((RW_DECAY_LORA, RW_WIDTH)), const((RW_A_LORA, RW_WIDTH)), const((RW_GATE_LORA, RW_WIDTH)),
                  const((RW_WIDTH, RW_WIDTH))],
        out_specs=(pl.BlockSpec((tt, RW_WIDTH), lambda s, t: (s * nt + t, 0)),
                   pl.BlockSpec((nstate, RW_HEADS, RW_HD, RW_HD), lambda s, t: (s, 0, 0, 0))),
        scratch_shapes=[pltpu.VMEM((tt + V7X_SUBLANES, C_COLS), f32),
                        pltpu.VMEM((tt, 2 * RW_WIDTH), f32), pltpu.VMEM((tt, 2 * RW_WIDTH), f32),
                        pltpu.VMEM((tt, RW_WIDTH), f32), pltpu.VMEM((tt, RW_WIDTH), f32),
                        pltpu.VMEM((tt, RW_WIDTH), f32), pltpu.VMEM((tt, RW_WIDTH), f32),
                        pltpu.VMEM((nmat, RW_HD, RW_HD), f32), pltpu.VMEM((nmat, RW_HD, RW_HD), f32),
                        pltpu.VMEM((RW_HEADS, RW_HD, RW_HD), f32)],
        compiler_params=_cparams(("parallel", "arbitrary"), vmem),
        name="rwkv7",
    )(pc, shift0, s0, lw["rw_mu"], lw["rw_vec"], lw["rw_w2"], lw["rw_a2"], lw["rw_g2"], lw["bd64"])


def _merge_kernel(h_ref, o0_ref, o1_ref, o2_ref, l0_ref, l1_ref, l2_ref, ob_ref, oc_ref, pg_ref,
                  wa_ref, wb_ref, wc_ref, wo_ref, out_ref):
    l0, l1, l2 = l0_ref[...], l1_ref[...], l2_ref[...]
    mx = jnp.maximum(jnp.maximum(l0, l1), l2)
    e0, e1, e2 = jnp.exp(l0 - mx), jnp.exp(l1 - mx), jnp.exp(l2 - mx)
    oa = (e0 * o0_ref[...] + e1 * o1_ref[...] + e2 * o2_ref[...]) / (e0 + e1 + e2)
    d = D_MODEL
    merged = (_sigmoid(pg_ref[:, 0:d]) * _dot(_bf(oa), wa_ref[...])
              + _sigmoid(pg_ref[:, d:2 * d]) * _dot(_bf(ob_ref[...]), wb_ref[...])
              + _sigmoid(pg_ref[:, 2 * d:3 * d]) * _dot(_bf(oc_ref[...]), wc_ref[...]))
    out_ref[...] = h_ref[...] + _dot(_bf(merged), wo_ref[...])


def _merge_sample_kernel(h_ref, oa_ref, ob_ref, oc_ref, pg_ref, wa_ref, wb_ref, wc_ref, wo_ref, out_ref):
    d = D_MODEL
    merged = (_sigmoid(pg_ref[:, 0:d]) * _dot(_bf(oa_ref[...]), wa_ref[...])
              + _sigmoid(pg_ref[:, d:2 * d]) * _dot(_bf(ob_ref[...]), wb_ref[...])
              + _sigmoid(pg_ref[:, 2 * d:3 * d]) * _dot(_bf(oc_ref[...]), wc_ref[...]))
    out_ref[...] = h_ref[...] + _dot(_bf(merged), wo_ref[...])


def _merge(h, oas, lses, ob, oc, pg, lw):
    rows, d = h.shape
    tr = min(ROW_TILE, rows)
    row = lambda n: pl.BlockSpec((tr, n), lambda i: (i, 0))
    const = lambda a: pl.BlockSpec(a.shape, lambda i: (0, 0))
    ws = (lw["w_br_a"], lw["w_br_b"], lw["w_br_c"], lw["w_out"])
    if lses is None:
        kern, acts = _merge_sample_kernel, (oas,)
    else:
        kern, acts = _merge_kernel, (*oas, *lses)
    acts = (h, *acts, ob, oc, pg)
    vmem = 2 * sum(tr * a.shape[1] * 4 for a in acts) + 2 * sum(w.size * 2 for w in ws) + 8 * tr * d * 4
    return pl.pallas_call(
        kern,
        out_shape=jax.ShapeDtypeStruct((rows, d), f32),
        grid=(rows // tr,),
        in_specs=[row(a.shape[1]) for a in acts] + [const(w) for w in ws],
        out_specs=row(d),
        compiler_params=_cparams(("parallel",), vmem),
        name="merge",
    )(*acts, *ws)


def _peer_candidate_groups(tv_scr):
    k = PEER_TOPK
    s = V7X_SUBLANES
    a_lo = tv_scr[0:s, :]
    a_hi = tv_scr[s:k, :]
    b = [tv_scr[k + j:k + j + 1, :] for j in range(s)]
    b_hi = tv_scr[k + s:2 * k, :]
    row = lax.broadcasted_iota(jnp.int32, a_lo.shape, 0)
    groups = [a_lo + b[0], a_hi + b[0], a_lo + b[1]]
    for j in range(2, s):
        groups.append(jnp.where(row < k // (j + 1), a_lo + b[j], NEG_INF))
    groups.append(tv_scr[0:1, :] + b_hi)
    return groups


def _peer_kernel(h_ref, nw_ref, wq_ref, sk_ref, u_ref, vt_ref, out_ref,
                 xn_scr, n1_scr, e1_scr, r2_scr, e2_scr, sc_scr, rk_scr, tv_scr, acc_scr, *, tokens, eblk):
    e = pl.program_id(1)
    k = PEER_TOPK
    half = PEER_QDIM // 2
    unranked = float(k + 1)

    @pl.when(e == 0)
    def _():
        x = h_ref[...]
        xn = _bf(x * lax.rsqrt(jnp.mean(x * x, axis=-1, keepdims=True) + RMS_EPS) * nw_ref[...])
        xn_scr[...] = xn
        q = _bf(_dot(xn, wq_ref[...]))
        lane_tiles = [slice(lt * V7X_LANES, (lt + 1) * V7X_LANES) for lt in range(tokens // V7X_LANES)]
        for h in range(PEER_HEADS):
            for p in range(2):
                hp = 2 * h + p
                s = _dot_nt(sk_ref[hp], q[:, hp * half:(hp + 1) * half])
                if p == 0:
                    e1_scr[h] = s
                else:
                    sc_scr[...] = s
                for lanes in lane_tiles:
                    cur = s[:, lanes]
                    for i in range(k):
                        m = jnp.max(cur, axis=0, keepdims=True)
                        tv_scr[p * k + i:p * k + i + 1, lanes] = m
                        cur = jnp.where(cur >= m, -RANK_CODE * (1.0 + (i + 1) / 64.0), cur)
                    rank = jnp.where(cur <= -RANK_CODE, (cur * (-1.0 / RANK_CODE) - 1.0) * 64.0, unranked)
                    if p == 0:
                        rk_scr[:, lanes] = rank
                    else:
                        r2_scr[h, :, lanes] = _bf(rank)
            cur = _peer_candidate_groups(tv_scr)
            thr = None
            for i in range(k):
                m = cur[0]
                for g in cur[1:]:
                    m = jnp.maximum(m, g)
                thr = jnp.max(m, axis=0, keepdims=True)
                cur = [jnp.where(g >= thr, NEG_INF, g) for g in cur]
            top1 = tv_scr[0:1, :]
            top2 = tv_scr[k:k + 1, :]
            tops1 = tv_scr[0:k, :]
            cnt = jnp.zeros(tops1.shape, f32)
            z = jnp.zeros_like(thr)
            for j in range(k):
                sums = tops1 + tv_scr[k + j:k + j + 1, :]
                ok = sums >= thr
                cnt = cnt + jnp.where(ok, 1.0, 0.0)
                z = z + jnp.sum(jnp.where(ok, jnp.exp(sums - (top1 + top2)), 0.0), axis=0, keepdims=True)
            scale2 = 0.5 / z
            for lanes in lane_tiles:
                rank1 = rk_scr[:, lanes]
                n1 = jnp.zeros(rank1.shape, f32)
                for i in range(k):
                    n1 = jnp.where(rank1 == float(i + 1), cnt[i:i + 1, lanes], n1)
                n1_scr[h, :, lanes] = n1
                e1_scr[h, :, lanes] = jnp.exp(e1_scr[h, :, lanes] - top1[:, lanes])
                e2_scr[h, :, lanes] = _bf(jnp.exp(sc_scr[:, lanes] - top2[:, lanes]) * scale2[:, lanes])
        acc_scr[...] = jnp.zeros_like(acc_scr)

    nb1 = eblk // PEER_KEYS
    nlt = tokens // V7X_LANES
    pack = 2 * V7X_SUBLANES
    xu = _dot_nt(u_ref[0], xn_scr[...])
    act2 = xu + xu * jnp.tanh(xu * (GELU_C1 + GELU_C3 * (xu * xu)))
    zero = jnp.zeros((), bf16)
    parts = []
    for ip in range(nb1 // 2):
        i1s = [e * nb1 + 2 * ip + j for j in range(2)]
        nrows = [[n1_scr[h, pl.ds(i1, 1), :] for h in range(PEER_HEADS)] for i1 in i1s]
        erows = [[e1_scr[h, pl.ds(i1, 1), :] for h in range(PEER_HEADS)] for i1 in i1s]
        tiles = [[None] * nlt for _ in range(2)]
        for lt in range(nlt):
            lanes = slice(lt * V7X_LANES, (lt + 1) * V7X_LANES)
            g = [None, None]
            for h in range(PEER_HEADS):
                r2t = r2_scr[h, :, lanes]
                e2t = e2_scr[h, :, lanes]
                for j in range(2):
                    nb = _bf(jnp.broadcast_to(nrows[j][h][:, lanes], (pack, V7X_LANES)))
                    eb = _bf(jnp.broadcast_to(erows[j][h][:, lanes], (pack, V7X_LANES)))
                    nb = jnp.concatenate([nb] * (PEER_KEYS // pack), axis=0)
                    eb = jnp.concatenate([eb] * (PEER_KEYS // pack), axis=0)
                    hit = jnp.where(r2t <= nb, e2t * eb, zero)
                    g[j] = hit if g[j] is None else g[j] + hit
            for j in range(2):
                tiles[j][lt] = g[j]
        for j in range(2):
            il = 2 * ip + j
            gate = tiles[j][0] if nlt == 1 else jnp.concatenate(tiles[j], axis=1)
            parts.append(gate * _bf(act2[il * PEER_KEYS:(il + 1) * PEER_KEYS, :]))
    acc_scr[...] += _dot(vt_ref[0], jnp.concatenate(parts, axis=0))

    @pl.when(e == pl.num_programs(1) - 1)
    def _():
        out_ref[...] = h_ref[...] + acc_scr[...].T


def _peer(h, lw, layer):
    rows, d = h.shape
    tokens = min(PEER_TOKENS, rows)
    assert tokens % V7X_LANES == 0 and rows % tokens == 0
    eblk = PEER_EBLK
    n_half = PEER_HEADS * 2
    vmem = (2 * (2 * tokens * d * 4 + d * d * 2 + 2 * eblk * d * 2)
            + tokens * d * 2 + 4 * PEER_HEADS * PEER_KEYS * tokens * 4 + d * tokens * 4
            + 5 * eblk * tokens * 4)
    return pl.pallas_call(
        functools.partial(_peer_kernel, tokens=tokens, eblk=eblk),
        out_shape=jax.ShapeDtypeStruct((rows, d), f32),
        grid=(rows // tokens, PEER_EXPERTS // eblk),
        in_specs=[pl.BlockSpec((tokens, d), lambda i, e: (i, 0)),
                  pl.BlockSpec((1, d), lambda i, e: (0, 0)),
                  pl.BlockSpec((d, PEER_HEADS * PEER_QDIM), lambda i, e: (0, 0)),
                  pl.BlockSpec((n_half, PEER_KEYS, PEER_QDIM // 2), lambda i, e: (0, 0, 0)),
                  pl.BlockSpec((1, eblk, d), lambda i, e: (layer, e, 0)),
                  pl.BlockSpec((1, d, eblk), lambda i, e: (layer, 0, e))],
        out_specs=pl.BlockSpec((tokens, d), lambda i, e: (i, 0)),
        scratch_shapes=[pltpu.VMEM((tokens, d), bf16),
                        pltpu.VMEM((PEER_HEADS, PEER_KEYS, tokens), f32),
                        pltpu.VMEM((PEER_HEADS, PEER_KEYS, tokens), f32),
                        pltpu.VMEM((PEER_HEADS, PEER_KEYS, tokens), bf16),
                        pltpu.VMEM((PEER_HEADS, PEER_KEYS, tokens), bf16),
                        pltpu.VMEM((PEER_KEYS, tokens), f32), pltpu.VMEM((PEER_KEYS, tokens), f32),
                        pltpu.VMEM((2 * PEER_TOPK, tokens), f32),
                        pltpu.VMEM((d, tokens), f32)],
        compiler_params=_cparams(("parallel", "arbitrary"), vmem),
        name="peer",
    )(h, lw["norm_ffn"], lw["peer_wq"], lw["peer_sk"], lw["peer_u"], lw["peer_vt"])


def _layer_weights(l, p):
    w_in = p["w_in"][l]
    o_b, o_c, o_g = A_COLS, A_COLS + B_COLS, A_COLS + B_COLS + C_COLS
    lane = jnp.arange(V7X_LANES)
    on_alpha = (lane >= DN_HEADS) & (lane < 2 * DN_HEADS)

    def alpha_lanes(vec):
        return jnp.where(on_alpha, jnp.pad(vec, (DN_HEADS, V7X_LANES - 2 * DN_HEADS)), 0.0)

    rw_vec = jnp.stack([p["rw_w0"][l], p["rw_a0"][l], p["rw_k_k"][l], p["rw_k_a"][l],
                        p["rw_r_k"][l].reshape(RW_WIDTH), p["rw_ln_w"][l], p["rw_ln_b"][l],
                        jnp.zeros((RW_WIDTH,), f32)])
    return {
        "norm_mix": p["norm_mix"][l],
        "w_a": _bf(w_in[:, 0:A_COLS]),
        "w_b": _bf(w_in[:, o_b:o_b + B_MAIN]),
        "w_bg": _bf(jnp.pad(w_in[:, o_b + B_MAIN:o_c], ((0, 0), (0, V7X_LANES - 2 * DN_HEADS)))),
        "w_c": _bf(w_in[:, o_c:o_g]),
        "w_g": _bf(w_in[:, o_g:]),
        "dn_conv_w": p["dn_conv_w"][l],
        "dn_prm": jnp.zeros((V7X_SUBLANES, V7X_LANES), f32)
                  .at[0].set(alpha_lanes(p["dn_a_log"][l])).at[1].set(alpha_lanes(p["dn_dt_bias"][l])),
        "dn_norm_w": jnp.tile(p["dn_norm_w"][l], DN_HEADS).reshape(1, DN_WIDTH),
        "bd128": _block_diag_ones(DN_WIDTH, DN_HD),
        "bd64": _block_diag_ones(RW_WIDTH, RW_HD),
        "rw_mu": p["rw_mu"][l].reshape(1, C_COLS),
        "rw_vec": rw_vec,
        "rw_w2": _bf(p["rw_w2"][l]), "rw_a2": _bf(p["rw_a2"][l]), "rw_g2": _bf(p["rw_g2"][l]),
        "w_br_a": _bf(p["w_br_a"][l]), "w_br_b": _bf(p["w_br_b"][l]), "w_br_c": _bf(p["w_br_c"][l]),
        "w_out": _bf(p["w_out"][l]),
        "norm_ffn": p["norm_ffn"][l].reshape(1, D_MODEL),
        "peer_wq": _bf(p["peer_wq"][l]),
        "peer_sk": _bf(p["peer_subkeys"][l].reshape(PEER_HEADS * 2, PEER_KEYS, PEER_QDIM // 2)),
        "peer_u": p["peer_u_bf"],
        "peer_vt": p["peer_vt_bf"],
    }


def _project(h, lw):
    nm = lw["norm_mix"]
    return {s: _norm_matmul(h, nm, lw["w_" + s]) for s in ("a", "b", "bg", "c", "g")}


def _kv_rows(pa, gi, lo, hi):
    k = pa[lo:hi, ATT_WIDTH + gi * ATT_OUT:ATT_WIDTH + (gi + 1) * ATT_OUT]
    v = pa[lo:hi, 2 * ATT_WIDTH + gi * ATT_OUT:2 * ATT_WIDTH + (gi + 1) * ATT_OUT]
    return jnp.stack([k, v], axis=1).reshape(hi - lo, 2, ATT_HPG, ATT_HD)


def _prompt_layer(h, lw, layer):
    t = h.shape[0]
    assert t % ATT_TILE == 0 and t % SEQ_TILE == 0
    pr = _project(h, lw)
    groups = [_attn_prompt_group(pr["a"], t, gi) for gi in range(len(ATT_GROUPS))]
    ob, dn_s = _dn_mixer(pr["b"], pr["bg"], jnp.zeros((1, V7X_SUBLANES, B_QKV), f32),
                         jnp.zeros((1, DN_HEADS, DN_HD, DN_HD), f32), lw,
                         nseq=1, tt=SEQ_TILE, chunk=DN_CHUNK, valid=(0, DN_CHUNK), nstate=1)
    oc, rw_s = _rw_mixer(pr["c"], jnp.zeros((1, 1, C_COLS), f32),
                         jnp.zeros((1, RW_HEADS, RW_HD, RW_HD), f32), lw,
                         nseq=1, tt=SEQ_TILE, chunk=RW_CHUNK, valid=(0, RW_CHUNK), nstate=1)
    h = _merge(h, [g[0] for g in groups], [g[1] for g in groups], ob, oc, pr["g"], lw)
    h = _peer(h, lw, layer)
    kvs = [_kv_rows(pr["a"], gi, t - min(win, t), t)[None] for gi, (win, _) in enumerate(ATT_GROUPS)]
    conv = pr["b"][t - (DN_CONV - 1):t, 0:B_QKV][None]
    shift = pr["c"][t - 1:t]
    return h, kvs, dn_s, conv, rw_s, shift


def _sample_rows(hist, x, nseq, seq):
    n = x.shape[1]
    nh = hist.shape[1]
    rows = jnp.concatenate([hist, x.reshape(nseq, seq, n), jnp.zeros((nseq, SAMPLE_ROWS - nh - seq, n), f32)], axis=1)
    return rows.reshape(nseq * SAMPLE_ROWS, n)


def _new_rows(x, nseq, nh, seq):
    return x.reshape(nseq, SAMPLE_ROWS, x.shape[1])[:, nh:nh + seq].reshape(nseq * seq, x.shape[1])


def _sample_layer(h, lw, layer, caches, dn_state, dn_conv, rw_state, rw_shift, nseq, seq):
    pr = _project(h, lw)
    oa = _attn_sample(pr["a"], caches, layer, nseq, seq)
    nh_dn, nh_rw = DN_CONV - 1, 1
    assert nseq % SAMPLE_STEP == 0 and nh_dn + seq <= SAMPLE_ROWS
    steps = nseq // SAMPLE_STEP
    tt = SAMPLE_STEP * SAMPLE_ROWS
    conv_hist = jnp.pad(dn_conv, ((0, 0), (0, 0), (0, B_MAIN - B_QKV)))
    ob, dn_s = _dn_mixer(_sample_rows(conv_hist, pr["b"], nseq, seq),
                         _sample_rows(jnp.zeros((nseq, nh_dn, V7X_LANES), f32), pr["bg"], nseq, seq),
                         jnp.zeros((steps, V7X_SUBLANES, B_QKV), f32), dn_state, lw,
                         nseq=steps, tt=tt, chunk=SAMPLE_ROWS, valid=(nh_dn, nh_dn + seq), nstate=SAMPLE_STEP)
    oc, rw_s = _rw_mixer(_sample_rows(rw_shift[:, None, :], pr["c"], nseq, seq),
                         jnp.zeros((steps, 1, C_COLS), f32), rw_state, lw,
                         nseq=steps, tt=tt, chunk=SAMPLE_ROWS, valid=(nh_rw, nh_rw + seq), nstate=SAMPLE_STEP)
    h = _merge(h, oa, None, _new_rows(ob, nseq, nh_dn, seq), _new_rows(oc, nseq, nh_rw, seq), pr["g"], lw)
    h = _peer(h, lw, layer)
    kvs = [_kv_rows(pr["a"], gi, 0, nseq * seq).reshape(nseq, seq, 2, ATT_HPG, ATT_HD)
           for gi in range(len(ATT_GROUPS))]
    conv = pr["b"][:, 0:B_QKV].reshape(nseq, seq, B_QKV)[:, seq - (DN_CONV - 1):]
    shift = pr["c"].reshape(nseq, seq, C_COLS)[:, seq - 1]
    return h, kvs, dn_s, conv, rw_s, shift


def kernel(x_prompt, x_sample, cache_kv_w128, cache_kv_w512, cache_kv_w2048, state_dn, state_dn_conv, state_rw, state_rw_shift, norm_mix, w_in, dn_conv_w, dn_a_log, dn_dt_bias, dn_norm_w, rw_mu, rw_w0, rw_w2, rw_a0, rw_a2, rw_g2, rw_k_k, rw_k_a, rw_r_k, rw_ln_w, rw_ln_b, w_br_a, w_br_b, w_br_c, w_out, norm_ffn, peer_wq, peer_subkeys, peer_u, peer_v, norm_final):
    p = dict(norm_mix=norm_mix, w_in=w_in, dn_conv_w=dn_conv_w, dn_a_log=dn_a_log, dn_dt_bias=dn_dt_bias,
             dn_norm_w=dn_norm_w, rw_mu=rw_mu, rw_w0=rw_w0, rw_w2=rw_w2, rw_a0=rw_a0, rw_a2=rw_a2, rw_g2=rw_g2,
             rw_k_k=rw_k_k, rw_k_a=rw_k_a, rw_r_k=rw_r_k, rw_ln_w=rw_ln_w, rw_ln_b=rw_ln_b, w_br_a=w_br_a,
             w_br_b=w_br_b, w_br_c=w_br_c, w_out=w_out, norm_ffn=norm_ffn, peer_wq=peer_wq,
             peer_subkeys=peer_subkeys, peer_u=peer_u, peer_v=peer_v)
    depth = w_in.shape[0]
    p["peer_u_bf"] = _bf(peer_u)
    p["peer_vt_bf"] = jnp.swapaxes(_bf(peer_v), 1, 2)
    bp, t, d = x_prompt.shape
    nseq, seq, _ = x_sample.shape
    assert bp == 1 and d == D_MODEL and seq <= SAMPLE_ROWS // 2 and DN_CONV - 1 <= seq
    caches = [c.reshape(depth, nseq, c.shape[2], 2 * ATT_OUT) for c in (cache_kv_w128, cache_kv_w512, cache_kv_w2048)]
    hp = x_prompt.reshape(t, d)
    hs = x_sample.reshape(nseq * seq, d)
    outs_p, outs_s = [], []
    for l in range(depth):
        lw = _layer_weights(l, p)
        hp, *st_p = _prompt_layer(hp, lw, l)
        hs, *st_s = _sample_layer(hs, lw, l, caches, state_dn[l], state_dn_conv[l], state_rw[l],
                                  state_rw_shift[l], nseq, seq)
        outs_p.append(st_p)
        outs_s.append(st_s)
    y_p = _rmsnorm(hp, norm_final).reshape(bp, t, d)
    y_s = _rmsnorm(hs, norm_final).reshape(nseq, seq, d)

    def stack(outs, pick):
        return jnp.stack([pick(o) for o in outs], axis=0)

    res = [y_p, y_s]
    for gi in range(len(ATT_GROUPS)):
        res.append(stack(outs_p, lambda o: o[0][gi]))
        res.append(stack(outs_s, lambda o: o[0][gi]))
    for idx in (1, 2, 3, 4):
        res.append(stack(outs_p, lambda o: o[idx]))
        res.append(stack(outs_s, lambda o: o[idx]))
    return tuple(res)
```

```python
import functools
import math

import jax
import jax.numpy as jnp
import numpy as np
from jax import lax
from jax.experimental import pallas as pl
from jax.experimental.pallas import tpu as pltpu

f32 = jnp.float32
bf16 = jnp.bfloat16

V7X_LANES = 128
V7X_SUBLANES = 8
V7X_VMEM_BYTES = 64 * 1024 * 1024
VMEM_CEILING = 56 * 1024 * 1024

D_MODEL = 1024
ATT_GROUPS = ((128, 1), (512, 4), (2048, 16))
ATT_HPG = 4
ATT_HD = 64
ATT_HEADS = ATT_HPG * len(ATT_GROUPS)
ATT_WIDTH = ATT_HEADS * ATT_HD
ATT_OUT = ATT_HPG * ATT_HD
ATT_STEPS = 128
DN_HEADS = 4
DN_HD = 128
DN_WIDTH = DN_HEADS * DN_HD
DN_CONV = 4
DN_CHUNK = 64
DN_GROUP = 4
RW_HEADS = 8
RW_HD = 64
RW_WIDTH = RW_HEADS * RW_HD
RW_DECAY_LORA = 64
RW_A_LORA = 64
RW_GATE_LORA = 128
RW_GN_EPS = 64e-5
RW_CHUNK = 64
RW_GROUP = 2
PEER_KEYS = 128
PEER_EXPERTS = PEER_KEYS * PEER_KEYS
PEER_HEADS = 8
PEER_QDIM = 128
PEER_TOPK = 16
RMS_EPS = 1e-6
L2_EPS = 1e-6

A_COLS = 3 * ATT_WIDTH
B_QKV = 3 * DN_WIDTH
B_MAIN = 4 * DN_WIDTH
B_COLS = B_MAIN + 2 * DN_HEADS
C_COLS = 3 * RW_WIDTH + RW_DECAY_LORA + RW_A_LORA + RW_GATE_LORA
G_COLS = 3 * D_MODEL

ROW_TILE = 512
PROJ_TILE = 1024
ATT_TILE = 2048
ATT_GROUP = 4
SEQ_TILE = 512
SAMPLE_ROWS = 8
SAMPLE_STEP = 8
PEER_TOKENS = 512
PEER_EBLK = 1024

NEG_INF = float("-inf")
RANK_CODE = 2.0 ** 100
GELU_C1 = math.sqrt(2.0 / math.pi)
GELU_C3 = 0.044715 * GELU_C1


def _alibi_slopes():
    h = np.arange(1, ATT_HEADS + 1, dtype=np.float32)
    return np.power(np.float32(2.0), -8.0 * h / ATT_HEADS).astype(np.float32)


def _cparams(semantics, vmem_bytes):
    return pltpu.CompilerParams(dimension_semantics=semantics,
                                vmem_limit_bytes=int(min(max(vmem_bytes, 16 * 1024 * 1024), VMEM_CEILING)))


def _dot(a, b):
    return jnp.dot(a, b, preferred_element_type=f32)


def _dot_nt(a, b):
    return lax.dot_general(a, b, (((1,), (1,)), ((), ())), preferred_element_type=f32)


def _dot_tn(a, b):
    return lax.dot_general(a, b, (((0,), (0,)), ((), ())), preferred_element_type=f32)


def _bf(x):
    return x.astype(bf16)


def _sigmoid(x):
    return 1.0 / (1.0 + jnp.exp(-x))


def _segsum(x, bd):
    hi = _bf(x)
    lo = _bf(x - hi.astype(f32))
    return _dot(hi, bd) + _dot(lo, bd)


def _seg_cumsum(x, rowc, seg):
    k = 1
    while k < seg:
        x = x + jnp.where(rowc >= k, pltpu.roll(x, k, 0), 0.0)
        k *= 2
    return x


def _block_diag_ones(width, seg):
    i = np.arange(width)
    return jnp.asarray((i[:, None] // seg) == (i[None, :] // seg), dtype=bf16)


def _norm_matmul_kernel(h_ref, nw_ref, w_ref, o_ref):
    x = h_ref[...]
    xn = x * lax.rsqrt(jnp.mean(x * x, axis=-1, keepdims=True) + RMS_EPS) * nw_ref[...]
    o_ref[...] = _dot(_bf(xn), w_ref[...])


def _norm_matmul(h, nw, w):
    rows, d = h.shape
    n = w.shape[1]
    tr = min(PROJ_TILE, rows)
    vmem = 2 * (tr * d * 4 + d * n * 2 + tr * n * 4) + 4 * tr * d * 4
    return pl.pallas_call(
        _norm_matmul_kernel,
        out_shape=jax.ShapeDtypeStruct((rows, n), f32),
        grid=(rows // tr,),
        in_specs=[pl.BlockSpec((tr, d), lambda i: (i, 0)),
                  pl.BlockSpec((1, d), lambda i: (0, 0)),
                  pl.BlockSpec((d, n), lambda i: (0, 0))],
        out_specs=pl.BlockSpec((tr, n), lambda i: (i, 0)),
        compiler_params=_cparams(("parallel",), vmem),
        name="norm_matmul",
    )(h, nw.reshape(1, d), w)


def _attn_prompt_kernel(q_ref, kc_ref, vc_ref, kp_ref, vp_ref, o_ref, lse_ref, kk_scr, vv_scr, *, dil, slopes):
    n = ATT_STEPS
    nblk = ATT_TILE // (dil * n)
    tile = pl.program_id(0)
    pair = pl.program_id(1)
    kk_scr[0:ATT_TILE, :] = kp_ref[...]
    kk_scr[ATT_TILE:2 * ATT_TILE, :] = kc_ref[...]
    vv_scr[0:ATT_TILE, :] = vp_ref[...]
    vv_scr[ATT_TILE:2 * ATT_TILE, :] = vc_ref[...]
    ii = lax.broadcasted_iota(jnp.int32, (n, 2 * n), 0)
    jj = lax.broadcasted_iota(jnp.int32, (n, 2 * n), 1)
    steps = n + ii - jj
    band = (steps >= 0) & (steps <= n)
    dist = (steps * dil).astype(f32)
    biases = []
    for hh in range(2):
        slope = jnp.where(pair == 0, slopes[hh], slopes[2 + hh])
        biases.append(jnp.where(band, -slope * dist, NEG_INF))

    def body(gidx, carry):
        items = []
        for cc in range(ATT_GROUP):
            c = gidx * ATT_GROUP + cc
            r = c % dil
            b = c // dil
            qs = r + dil * n * b
            rows_q = pl.ds(qs, n, stride=dil)
            rows_k = pl.ds(ATT_TILE + qs - dil * n, 2 * n, stride=dil)
            q = q_ref[rows_q, :] * (ATT_HD ** -0.5)
            k = kk_scr[rows_k, :]
            first_key = jnp.where(tile * nblk + b == 0, n, 0)
            ss = [_dot_nt(_bf(q[:, hh * ATT_HD:(hh + 1) * ATT_HD]), _bf(k[:, hh * ATT_HD:(hh + 1) * ATT_HD]))
                  for hh in range(2)]
            items.append(dict(rows_q=rows_q, rows_k=rows_k, first_key=first_key, s=ss))
        for it in items:
            it["p"], it["den"], it["lse"] = [], [], []
            for hh in range(2):
                s = jnp.where(jj >= it["first_key"], it["s"][hh] + biases[hh], NEG_INF)
                mx = jnp.max(s, axis=-1, keepdims=True)
                p = jnp.exp(s - mx)
                den = jnp.sum(p, axis=-1, keepdims=True)
                it["p"].append(_bf(p))
                it["den"].append(den)
                it["lse"].append(jnp.broadcast_to(mx + jnp.log(den), (n, ATT_HD)))
        for it in items:
            v = vv_scr[it["rows_k"], :]
            outs = [_dot(it["p"][hh], _bf(v[:, hh * ATT_HD:(hh + 1) * ATT_HD])) / it["den"][hh] for hh in range(2)]
            o_ref[it["rows_q"], :] = jnp.concatenate(outs, axis=1)
            lse_ref[it["rows_q"], :] = jnp.concatenate(it["lse"], axis=1)
        return carry

    lax.fori_loop(0, dil * nblk // ATT_GROUP, body, 0)


def _attn_prompt_group(pa, seq_len, gi):
    _, dil = ATT_GROUPS[gi]
    slopes = tuple(float(s) for s in _alibi_slopes()[gi * ATT_HPG:(gi + 1) * ATT_HPG])
    w = V7X_LANES
    qb, kb, vb = (gi * 2, 6 + gi * 2, 12 + gi * 2)
    blk = (ATT_TILE, w)
    cur = lambda base: pl.BlockSpec(blk, lambda i, j, base=base: (i, base + j))
    prev = lambda base: pl.BlockSpec(blk, lambda i, j, base=base: (jnp.maximum(i - 1, 0), base + j))
    vmem = 2 * 7 * ATT_TILE * w * 4 + 2 * 2 * ATT_TILE * w * 4 + 8 * 1024 * 1024
    return pl.pallas_call(
        functools.partial(_attn_prompt_kernel, dil=dil, slopes=slopes),
        out_shape=(jax.ShapeDtypeStruct((seq_len, ATT_OUT), f32), jax.ShapeDtypeStruct((seq_len, ATT_OUT), f32)),
        grid=(seq_len // ATT_TILE, 2),
        in_specs=[cur(qb), cur(kb), cur(vb), prev(kb), prev(vb)],
        out_specs=(pl.BlockSpec(blk, lambda i, j: (i, j)), pl.BlockSpec(blk, lambda i, j: (i, j))),
        scratch_shapes=[pltpu.VMEM((2 * ATT_TILE, w), f32), pltpu.VMEM((2 * ATT_TILE, w), f32)],
        compiler_params=_cparams(("parallel", "parallel"), vmem),
        name=f"attn_prompt_g{gi}",
    )(pa, pa, pa, pa, pa)


def _attn_sample_kernel(q_ref, k_ref, v_ref, c0_ref, c1_ref, c2_ref, o_ref, *, slopes, seq):
    rows = 2 * seq
    caches = (c0_ref, c1_ref, c2_ref)
    row = lax.broadcasted_iota(jnp.int32, (rows, 1), 0)
    own0 = row < seq
    qpos = row % seq
    npos = lax.broadcasted_iota(jnp.int32, (rows, rows), 1)
    q_all = q_ref[0] * (ATT_HD ** -0.5)
    k_all = k_ref[0]
    v_all = v_ref[0]
    outs, lses = [], []
    for gi, (win, dil) in enumerate(ATT_GROUPS):
        cpos = lax.broadcasted_iota(jnp.int32, (rows, win), 1)
        dist_c = win + qpos - cpos
        ok_c = (dist_c <= win) & ((dist_c & (dil - 1)) == 0)
        dist_n = row - npos
        ok_n = (dist_n >= 0) & (dist_n <= qpos) & ((dist_n & (dil - 1)) == 0)
        dcf = dist_c.astype(f32)
        dnf = dist_n.astype(f32)
        for h in range(ATT_HPG):
            slope = slopes[gi * ATT_HPG + h]
            cols = slice(gi * ATT_OUT + h * ATT_HD, gi * ATT_OUT + (h + 1) * ATT_HD)
            kcols = slice(h * ATT_HD, (h + 1) * ATT_HD)
            vcols = slice(ATT_OUT + h * ATT_HD, ATT_OUT + (h + 1) * ATT_HD)
            qh = _bf(q_all[:, cols])
            sn = jnp.where(ok_n, _dot_nt(qh, _bf(k_all[:, cols])) - slope * dnf, NEG_INF)
            scs = []
            for b in range(2):
                sc = _dot_nt(qh, _bf(caches[gi][0, b, :, kcols])) - slope * dcf
                scs.append(jnp.where(ok_c, sc, NEG_INF))
            sc = jnp.where(own0, scs[0], scs[1])
            mx = jnp.maximum(jnp.max(sc, axis=-1, keepdims=True), jnp.max(sn, axis=-1, keepdims=True))
            pc = jnp.exp(sc - mx)
            pn = jnp.exp(sn - mx)
            den = jnp.sum(pc, axis=-1, keepdims=True) + jnp.sum(pn, axis=-1, keepdims=True)
            pcb = _bf(pc)
            oc = jnp.where(own0, _dot(pcb, _bf(caches[gi][0, 0, :, vcols])), _dot(pcb, _bf(caches[gi][0, 1, :, vcols])))
            o = (oc + _dot(_bf(pn), _bf(v_all[:, cols]))) / den
            outs.append(o)
            lses.append(mx + jnp.log(den))
    merged = []
    for h in range(ATT_HPG):
        ls = [lses[gi * ATT_HPG + h] for gi in range(len(ATT_GROUPS))]
        mx = jnp.maximum(jnp.maximum(ls[0], ls[1]), ls[2])
        es = [jnp.exp(l - mx) for l in ls]
        den = es[0] + es[1] + es[2]
        acc = es[0] * outs[h] + es[1] * outs[ATT_HPG + h] + es[2] * outs[2 * ATT_HPG + h]
        merged.append(acc / den)
    o_ref[0] = jnp.concatenate(merged, axis=1)


def _attn_sample(pa, caches, layer, nseq, seq):
    rows = 2 * seq
    assert rows == SAMPLE_ROWS and nseq % 2 == 0
    slopes = tuple(float(s) for s in _alibi_slopes())
    pa3 = pa.reshape(nseq // 2, rows, A_COLS)
    in_specs = [pl.BlockSpec((1, rows, ATT_WIDTH), lambda i, c=c: (i, 0, c)) for c in range(3)]
    vmem = 0
    for (win, _), c in zip(ATT_GROUPS, caches):
        assert c.shape[2] == win, "cached window must hold exactly `window` rows"
        in_specs.append(pl.BlockSpec((1, 2, win, 2 * ATT_OUT), lambda i: (layer, i, 0, 0)))
        vmem += 2 * 2 * win * 2 * ATT_OUT * 4
    out = pl.pallas_call(
        functools.partial(_attn_sample_kernel, slopes=slopes, seq=seq),
        out_shape=jax.ShapeDtypeStruct((nseq // 2, rows, ATT_OUT), f32),
        grid=(nseq // 2,),
        in_specs=in_specs,
        out_specs=pl.BlockSpec((1, rows, ATT_OUT), lambda i: (i, 0, 0)),
        compiler_params=_cparams(("parallel",), vmem + 16 * 1024 * 1024),
        name="attn_sample",
    )(pa3, pa3, pa3, *caches)
    return out.reshape(nseq * seq, ATT_OUT)


def _dn_kernel(pb_ref, pbg_ref, conv0_ref, s0_ref, cw_ref, prm_ref, nw_ref, bd_ref,
               o_ref, sout_ref,
               xs_scr, q_scr, k_scr, v_scr, g_scr, b_scr, oc_scr, qp_scr, m_scr, n_scr, s_scr,
               *, tt, chunk, valid, nstate):
    t = pl.program_id(1)

    @pl.when(t == 0)
    def _():
        xs_scr[0:V7X_SUBLANES, :] = conv0_ref[0]
        if nstate == 1:
            s_scr[...] = s0_ref[0]

    x = pb_ref[:, 0:B_QKV]
    xs_scr[pl.ds(V7X_SUBLANES, tt), :] = x
    cw = cw_ref[...]
    y = x * cw[3:4, :]
    for j in range(DN_CONV - 1):
        y = y + xs_scr[pl.ds(V7X_SUBLANES - (DN_CONV - 1) + j, tt), :] * cw[j:j + 1, :]
    xs_scr[0:V7X_SUBLANES, :] = xs_scr[pl.ds(tt, V7X_SUBLANES), :]
    y = y * _sigmoid(y)
    bd = bd_ref[...]
    q = y[:, 0:DN_WIDTH]
    k = y[:, DN_WIDTH:2 * DN_WIDTH]
    q_scr[...] = q * lax.rsqrt(_segsum(q * q, bd) + L2_EPS) * (DN_HD ** -0.5)
    k_scr[...] = k * lax.rsqrt(_segsum(k * k, bd) + L2_EPS)
    v_scr[...] = y[:, 2 * DN_WIDTH:3 * DN_WIDTH]
    pg = pbg_ref[...]
    beta = _sigmoid(pg)
    g = -jnp.exp(prm_ref[0:1, :]) * jax.nn.softplus(pg + prm_ref[1:2, :])
    rown = lax.broadcasted_iota(jnp.int32, (tt, V7X_LANES), 0)
    if valid != (0, chunk):
        rc = rown & (chunk - 1)
        beta = jnp.where(rc >= valid[0], jnp.where(rc < valid[1], beta, 0.0), 0.0)
        g = jnp.where(rc >= valid[0], jnp.where(rc < valid[1], g, 0.0), 0.0)
    g_scr[...] = _seg_cumsum(g, rown & (chunk - 1), chunk)
    b_scr[...] = beta

    ci = lax.broadcasted_iota(jnp.int32, (chunk, chunk), 0)
    cj = lax.broadcasted_iota(jnp.int32, (chunk, chunk), 1)
    incl = ci >= cj
    strict = ci > cj

    nchunks = tt // chunk
    group = min(DN_GROUP, nchunks)

    def precompute(gi, carry):
        items = []
        for cc in range(group):
            cidx = gi * group + cc
            rows = pl.ds(pl.multiple_of(cidx * chunk, chunk), chunk)
            gc = g_scr[rows, :]
            bc = b_scr[rows, :]
            gct = gc.T
            for h in range(DN_HEADS):
                cols = slice(h * DN_HD, (h + 1) * DN_HD)
                qh, kh, vh = q_scr[rows, cols], k_scr[rows, cols], v_scr[rows, cols]
                gcol = gc[:, DN_HEADS + h:DN_HEADS + h + 1]
                grow = gct[DN_HEADS + h:DN_HEADS + h + 1, :]
                bcol = bc[:, h:h + 1]
                glast = gc[chunk - 1:chunk, DN_HEADS + h:DN_HEADS + h + 1]
                decay = jnp.exp(jnp.where(incl, gcol - grow, NEG_INF))
                kb = kh * bcol
                eg = jnp.exp(gcol)
                both = _dot_nt(_bf(jnp.concatenate([kb, qh], axis=0)), _bf(kh))
                items.append(dict(
                    rows=rows, cols=cols, idx=cidx * DN_HEADS + h,
                    p=jnp.where(strict, -both[0:chunk] * decay, 0.0),
                    attn=jnp.where(incl, both[chunk:2 * chunk] * decay, 0.0),
                    sol=jnp.concatenate([vh * bcol, kb * eg], axis=1),
                    qeg=qh * eg, kdec=_bf(kh * jnp.exp(glast - gcol))))
        k = 1
        while k < chunk:
            for it in items:
                it["sol"] = it["sol"] + _dot(_bf(it["p"]), _bf(it["sol"]))
            if 2 * k < chunk:
                for it in items:
                    pb = _bf(it["p"])
                    it["p"] = _dot(pb, pb)
            k *= 2
        for it in items:
            u = _bf(it["sol"][:, 0:DN_HD])
            w = _bf(it["sol"][:, DN_HD:2 * DN_HD])
            attn = _bf(it["attn"])
            qp_scr[it["rows"], it["cols"]] = it["qeg"] - _dot(attn, w)
            oc_scr[it["rows"], it["cols"]] = _dot(attn, u)
            m_scr[it["idx"]] = _dot_tn(it["kdec"], w)
            n_scr[it["idx"]] = _dot_tn(it["kdec"], u)
        return carry

    lax.fori_loop(0, nchunks // group, precompute, 0)

    def recur(c, carry):
        rows = pl.ds(pl.multiple_of(c * chunk, chunk), chunk)
        g_tail = g_scr[pl.ds(pl.multiple_of((c + 1) * chunk - V7X_SUBLANES, V7X_SUBLANES), V7X_SUBLANES), :]
        states = [s_scr[h] if nstate == 1 else s0_ref[c, h] for h in range(DN_HEADS)]
        sb = [_bf(s) for s in states]
        outs = [_dot(_bf(qp_scr[rows, h * DN_HD:(h + 1) * DN_HD]), sb[h]) for h in range(DN_HEADS)]
        upd = [_dot(_bf(m_scr[c * DN_HEADS + h]), sb[h]) for h in range(DN_HEADS)]
        for h in range(DN_HEADS):
            cols = slice(h * DN_HD, (h + 1) * DN_HD)
            oc_scr[rows, cols] = oc_scr[rows, cols] + outs[h]
            g_end = jnp.exp(g_tail[V7X_SUBLANES - 1:V7X_SUBLANES, DN_HEADS + h:DN_HEADS + h + 1])
            new = states[h] * g_end - upd[h] + n_scr[c * DN_HEADS + h]
            if nstate == 1:
                s_scr[h] = new
            else:
                sout_ref[c, h] = new
        return carry

    lax.fori_loop(0, nchunks, recur, 0)

    o = oc_scr[...]
    z = pb_ref[:, B_QKV:B_MAIN]
    o = o * lax.rsqrt(_segsum(o * o, bd) * (1.0 / DN_HD) + RMS_EPS) * nw_ref[...]
    o_ref[...] = o * (z * _sigmoid(z))

    if nstate == 1:
        @pl.when(t == pl.num_programs(1) - 1)
        def _():
            sout_ref[0] = s_scr[...]


def _dn_mixer(pb, pbg, conv0, s0, lw, *, nseq, tt, chunk, valid, nstate):
    rows = pb.shape[0]
    nt = rows // (nseq * tt)
    nmat = (tt // chunk) * DN_HEADS
    vmem = (2 * (tt * (B_MAIN + V7X_LANES + DN_WIDTH) * 4 + DN_WIDTH * DN_WIDTH * 2)
            + (tt + 8) * B_QKV * 4 + 4 * tt * DN_WIDTH * 4 + 2 * tt * V7X_LANES * 4
            + tt * DN_WIDTH * 4 + (2 * nmat + (6 + 4 * nstate) * DN_HEADS) * DN_HD * DN_HD * 4 + 8 * tt * B_QKV * 4)
    return pl.pallas_call(
        functools.partial(_dn_kernel, tt=tt, chunk=chunk, valid=valid, nstate=nstate),
        out_shape=(jax.ShapeDtypeStruct((rows, DN_WIDTH), f32),
                   jax.ShapeDtypeStruct((nseq * nstate, DN_HEADS, DN_HD, DN_HD), f32)),
        grid=(nseq, nt),
        in_specs=[pl.BlockSpec((tt, B_MAIN), lambda s, t: (s * nt + t, 0)),
                  pl.BlockSpec((tt, V7X_LANES), lambda s, t: (s * nt + t, 0)),
                  pl.BlockSpec((1, V7X_SUBLANES, B_QKV), lambda s, t: (s, 0, 0)),
                  pl.BlockSpec((nstate, DN_HEADS, DN_HD, DN_HD), lambda s, t: (s, 0, 0, 0)),
                  pl.BlockSpec((DN_CONV, B_QKV), lambda s, t: (0, 0)),
                  pl.BlockSpec((V7X_SUBLANES, V7X_LANES), lambda s, t: (0, 0)),
                  pl.BlockSpec((1, DN_WIDTH), lambda s, t: (0, 0)),
                  pl.BlockSpec((DN_WIDTH, DN_WIDTH), lambda s, t: (0, 0))],
        out_specs=(pl.BlockSpec((tt, DN_WIDTH), lambda s, t: (s * nt + t, 0)),
                   pl.BlockSpec((nstate, DN_HEADS, DN_HD, DN_HD), lambda s, t: (s, 0, 0, 0))),
        scratch_shapes=[pltpu.VMEM((tt + V7X_SUBLANES, B_QKV), f32),
                        pltpu.VMEM((tt, DN_WIDTH), f32), pltpu.VMEM((tt, DN_WIDTH), f32),
                        pltpu.VMEM((tt, DN_WIDTH), f32),
                        pltpu.VMEM((tt, V7X_LANES), f32), pltpu.VMEM((tt, V7X_LANES), f32),
                        pltpu.VMEM((tt, DN_WIDTH), f32), pltpu.VMEM((tt, DN_WIDTH), f32),
                        pltpu.VMEM((nmat, DN_HD, DN_HD), f32), pltpu.VMEM((nmat, DN_HD, DN_HD), f32),
                        pltpu.VMEM((DN_HEADS, DN_HD, DN_HD), f32)],
        compiler_params=_cparams(("parallel", "arbitrary"), vmem),
        name="deltanet",
    )(pb, pbg, conv0, s0, lw["dn_conv_w"], lw["dn_prm"], lw["dn_norm_w"], lw["bd128"])


def _rw_kernel(pc_ref, shift0_ref, s0_ref, mu_ref, vec_ref, w2_ref, a2_ref, g2_ref, bd_ref,
               o_ref, sout_ref,
               xs_scr, ar_scr, bk_scr, v_scr, pe_scr, oc_scr, rp_scr, m_scr, n_scr, s_scr, *, tt, chunk, valid, nstate):
    t = pl.program_id(1)

    @pl.when(t == 0)
    def _():
        xs_scr[V7X_SUBLANES - 1:V7X_SUBLANES, :] = shift0_ref[0]
        if nstate == 1:
            s_scr[...] = s0_ref[0]

    pc = pc_ref[...]
    xs_scr[pl.ds(V7X_SUBLANES, tt), :] = pc
    prev = xs_scr[pl.ds(V7X_SUBLANES - 1, tt), :]
    xs_scr[0:V7X_SUBLANES, :] = xs_scr[pl.ds(tt, V7X_SUBLANES), :]
    xc = pc + (prev - pc) * mu_ref[...]
    w3 = 3 * RW_WIDTH
    r = xc[:, 0:RW_WIDTH]
    k = xc[:, RW_WIDTH:2 * RW_WIDTH]
    v = xc[:, 2 * RW_WIDTH:w3]
    wd = xc[:, w3:w3 + RW_DECAY_LORA]
    ad = xc[:, w3 + RW_DECAY_LORA:w3 + RW_DECAY_LORA + RW_A_LORA]
    gd = xc[:, w3 + RW_DECAY_LORA + RW_A_LORA:C_COLS]
    w0, a0, k_k, k_a, r_k, ln_w, ln_b = (vec_ref[i:i + 1, :] for i in range(7))
    w_log = -jax.nn.softplus(-(w0 + _dot(_bf(jnp.tanh(wd)), w2_ref[...]))) - 0.5
    lw = -jnp.exp(w_log)
    a = _sigmoid(a0 + _dot(_bf(ad), a2_ref[...]))
    gate = _dot(_bf(_sigmoid(gd)), g2_ref[...])
    bd = bd_ref[...]
    kkr = k * k_k
    kk = kkr * lax.rsqrt(_segsum(kkr * kkr, bd) + L2_EPS)
    k2 = k * (1.0 + (a - 1.0) * k_a)
    bonus = _segsum(r * k2 * r_k, bd) * v
    rown = lax.broadcasted_iota(jnp.int32, (tt, RW_WIDTH), 0)
    if valid != (0, chunk):
        rc = rown & (chunk - 1)
        pad = jnp.where(rc >= valid[0], jnp.where(rc < valid[1], 0, 1), 1) == 1
        lw = jnp.where(pad, 0.0, lw)
        kk = jnp.where(pad, 0.0, kk)
        k2 = jnp.where(pad, 0.0, k2)
    gcum = _seg_cumsum(lw, rown & (chunk - 1), chunk)
    e_pos = jnp.exp(gcum)
    e_neg = jnp.exp(-gcum)
    ar_scr[:, 0:RW_WIDTH] = -kk * jnp.exp(gcum - lw)
    ar_scr[:, RW_WIDTH:2 * RW_WIDTH] = r * e_pos
    bk_scr[:, 0:RW_WIDTH] = kk * a * e_neg
    bk_scr[:, RW_WIDTH:2 * RW_WIDTH] = k2 * e_neg
    v_scr[...] = v
    pe_scr[...] = e_pos

    ci = lax.broadcasted_iota(jnp.int32, (2 * chunk, 2 * chunk), 0)
    cj = lax.broadcasted_iota(jnp.int32, (2 * chunk, 2 * chunk), 1)
    keep = jnp.where(ci < chunk, ci - 1, ci - chunk) >= (cj & (chunk - 1))

    nchunks = tt // chunk
    group = min(RW_GROUP, nchunks)

    def precompute(gi, carry):
        items = []
        for cc in range(group):
            cidx = gi * group + cc
            rows = pl.ds(pl.multiple_of(cidx * chunk, chunk), chunk)
            p_tail = pe_scr[pl.ds(pl.multiple_of((cidx + 1) * chunk - V7X_SUBLANES, V7X_SUBLANES), V7X_SUBLANES), :]
            for h in range(RW_HEADS):
                cols = slice(h * RW_HD, (h + 1) * RW_HD)
                cols2 = slice(RW_WIDTH + h * RW_HD, RW_WIDTH + (h + 1) * RW_HD)
                at, rt = ar_scr[rows, cols], ar_scr[rows, cols2]
                bt, kt = bk_scr[rows, cols], bk_scr[rows, cols2]
                bkb = _bf(jnp.concatenate([bt, kt], axis=0))
                m = jnp.where(keep, _dot_nt(_bf(jnp.concatenate([at, rt], axis=0)), bkb), 0.0)
                items.append(dict(rows=rows, cols=cols, idx=cidx * RW_HEADS + h, at=at, rt=rt, bt=_bf(bt), bkb=bkb,
                                  vh=v_scr[rows, cols], p=m[0:chunk, 0:chunk], aak=_bf(m[0:chunk, chunk:2 * chunk]),
                                  mlow=_bf(m[chunk:2 * chunk, :]), p_end=p_tail[V7X_SUBLANES - 1:V7X_SUBLANES, cols]))
        for it in items:
            it["sol"] = jnp.concatenate([it["at"], _dot(it["aak"], _bf(it["vh"]))], axis=1)
        k = 1
        while k < chunk:
            for it in items:
                it["sol"] = it["sol"] + _dot(_bf(it["p"]), _bf(it["sol"]))
            if 2 * k < chunk:
                for it in items:
                    pb = _bf(it["p"])
                    it["p"] = _dot(pb, pb)
            k *= 2
        for it in items:
            wm = _bf(it["sol"][:, 0:RW_HD])
            uv = _bf(jnp.concatenate([it["sol"][:, RW_HD:2 * RW_HD], it["vh"]], axis=0))
            rp_scr[it["rows"], it["cols"]] = it["rt"] + _dot(it["mlow"][:, 0:chunk], wm)
            oc_scr[it["rows"], it["cols"]] = _dot(it["mlow"], uv)
            m_scr[it["idx"]] = _dot_tn(wm, it["bt"]) * it["p_end"]
            n_scr[it["idx"]] = _dot_tn(uv, it["bkb"]) * it["p_end"]
        return carry

    lax.fori_loop(0, nchunks // group, precompute, 0)

    def recur(c, carry):
        rows = pl.ds(pl.multiple_of(c * chunk, chunk), chunk)
        p_tail = pe_scr[pl.ds(pl.multiple_of((c + 1) * chunk - V7X_SUBLANES, V7X_SUBLANES), V7X_SUBLANES), :]
        states = [s_scr[h] if nstate == 1 else s0_ref[c, h] for h in range(RW_HEADS)]
        sb = [_bf(s) for s in states]
        outs = [_dot_nt(_bf(rp_scr[rows, h * RW_HD:(h + 1) * RW_HD]), sb[h]) for h in range(RW_HEADS)]
        upd = [_dot(sb[h], _bf(m_scr[c * RW_HEADS + h])) for h in range(RW_HEADS)]
        for h in range(RW_HEADS):
            cols = slice(h * RW_HD, (h + 1) * RW_HD)
            oc_scr[rows, cols] = oc_scr[rows, cols] + outs[h]
            new = states[h] * p_tail[V7X_SUBLANES - 1:V7X_SUBLANES, cols] + upd[h] + n_scr[c * RW_HEADS + h]
            if nstate == 1:
                s_scr[h] = new
            else:
                sout_ref[c, h] = new
        return carry

    lax.fori_loop(0, nchunks, recur, 0)

    o = oc_scr[...]
    mean = _segsum(o, bd) * (1.0 / RW_HD)
    d = o - mean
    var = _segsum(d * d, bd) * (1.0 / RW_HD)
    o = d * lax.rsqrt(var + RW_GN_EPS) * ln_w + ln_b
    o_ref[...] = (o + bonus) * gate

    if nstate == 1:
        @pl.when(t == pl.num_programs(1) - 1)
        def _():
            sout_ref[0] = s_scr[...]


def _rw_mixer(pc, shift0, s0, lw, *, nseq, tt, chunk, valid, nstate):
    rows = pc.shape[0]
    nt = rows // (nseq * tt)
    nmat = (tt // chunk) * RW_HEADS
    vmem = (2 * (tt * (C_COLS + RW_WIDTH) * 4 + RW_WIDTH * RW_WIDTH * 2 + 4 * RW_WIDTH * V7X_LANES * 2)
            + (tt + 8) * C_COLS * 4 + 8 * tt * RW_WIDTH * 4 + (2 * nmat + (6 + 4 * nstate) * RW_HEADS) * RW_HD * V7X_LANES * 4
            + 14 * tt * RW_WIDTH * 4)
    const = lambda shape: pl.BlockSpec(shape, lambda s, t: (0,) * len(shape))
    return pl.pallas_call(
        functools.partial(_rw_kernel, tt=tt, chunk=chunk, valid=valid, nstate=nstate),
        out_shape=(jax.ShapeDtypeStruct((rows, RW_WIDTH), f32),
                   jax.ShapeDtypeStruct((nseq * nstate, RW_HEADS, RW_HD, RW_HD), f32)),
        grid=(nseq, nt),
        in_specs=[pl.BlockSpec((tt, C_COLS), lambda s, t: (s * nt + t, 0)),
                  pl.BlockSpec((1, 1, C_COLS), lambda s, t: (s, 0, 0)),
                  pl.BlockSpec((nstate, RW_HEADS, RW_HD, RW_HD), lambda s, t: (s, 0, 0, 0)),
                  const((1, C_COLS)), const((V7X_SUBLANES, RW_WIDTH)),
                  const((RW_DECAY_LORA, RW_WIDTH)), const((RW_A_LORA, RW_WIDTH)), const((RW_GATE_LORA, RW_WIDTH)),
                  const((RW_WIDTH, RW_WIDTH))],
        out_specs=(pl.BlockSpec((tt, RW_WIDTH), lambda s, t: (s * nt + t, 0)),
                   pl.BlockSpec((nstate, RW_HEADS, RW_HD, RW_HD), lambda s, t: (s, 0, 0, 0))),
        scratch_shapes=[pltpu.VMEM((tt + V7X_SUBLANES, C_COLS), f32),
                        pltpu.VMEM((tt, 2 * RW_WIDTH), f32), pltpu.VMEM((tt, 2 * RW_WIDTH), f32),
                        pltpu.VMEM((tt, RW_WIDTH), f32), pltpu.VMEM((tt, RW_WIDTH), f32),
                        pltpu.VMEM((tt, RW_WIDTH), f32), pltpu.VMEM((tt, RW_WIDTH), f32),
                        pltpu.VMEM((nmat, RW_HD, RW_HD), f32), pltpu.VMEM((nmat, RW_HD, RW_HD), f32),
                        pltpu.VMEM((RW_HEADS, RW_HD, RW_HD), f32)],
        compiler_params=_cparams(("parallel", "arbitrary"), vmem),
        name="rwkv7",
    )(pc, shift0, s0, lw["rw_mu"], lw["rw_vec"], lw["rw_w2"], lw["rw_a2"], lw["rw_g2"], lw["bd64"])


def _merge_kernel(h_ref, o0_ref, o1_ref, o2_ref, l0_ref, l1_ref, l2_ref, ob_ref, oc_ref, pg_ref,
                  wa_ref, wb_ref, wc_ref, wo_ref, out_ref):
    l0, l1, l2 = l0_ref[...], l1_ref[...], l2_ref[...]
    mx = jnp.maximum(jnp.maximum(l0, l1), l2)
    e0, e1, e2 = jnp.exp(l0 - mx), jnp.exp(l1 - mx), jnp.exp(l2 - mx)
    oa = (e0 * o0_ref[...] + e1 * o1_ref[...] + e2 * o2_ref[...]) / (e0 + e1 + e2)
    d = D_MODEL
    merged = (_sigmoid(pg_ref[:, 0:d]) * _dot(_bf(oa), wa_ref[...])
              + _sigmoid(pg_ref[:, d:2 * d]) * _dot(_bf(ob_ref[...]), wb_ref[...])
              + _sigmoid(pg_ref[:, 2 * d:3 * d]) * _dot(_bf(oc_ref[...]), wc_ref[...]))
    out_ref[...] = h_ref[...] + _dot(_bf(merged), wo_ref[...])


def _merge_sample_kernel(h_ref, oa_ref, ob_ref, oc_ref, pg_ref, wa_ref, wb_ref, wc_ref, wo_ref, out_ref):
    d = D_MODEL
    merged = (_sigmoid(pg_ref[:, 0:d]) * _dot(_bf(oa_ref[...]), wa_ref[...])
              + _sigmoid(pg_ref[:, d:2 * d]) * _dot(_bf(ob_ref[...]), wb_ref[...])
              + _sigmoid(pg_ref[:, 2 * d:3 * d]) * _dot(_bf(oc_ref[...]), wc_ref[...]))
    out_ref[...] = h_ref[...] + _dot(_bf(merged), wo_ref[...])


def _merge(h, oas, lses, ob, oc, pg, lw):
    rows, d = h.shape
    tr = min(ROW_TILE, rows)
    row = lambda n: pl.BlockSpec((tr, n), lambda i: (i, 0))
    const = lambda a: pl.BlockSpec(a.shape, lambda i: (0, 0))
    ws = (lw["w_br_a"], lw["w_br_b"], lw["w_br_c"], lw["w_out"])
    if lses is None:
        kern, acts = _merge_sample_kernel, (oas,)
    else:
        kern, acts = _merge_kernel, (*oas, *lses)
    acts = (h, *acts, ob, oc, pg)
    vmem = 2 * sum(tr * a.shape[1] * 4 for a in acts) + 2 * sum(w.size * 2 for w in ws) + 8 * tr * d * 4
    return pl.pallas_call(
        kern,
        out_shape=jax.ShapeDtypeStruct((rows, d), f32),
        grid=(rows // tr,),
        in_specs=[row(a.shape[1]) for a in acts] + [const(w) for w in ws],
        out_specs=row(d),
        compiler_params=_cparams(("parallel",), vmem),
        name="merge",
    )(*acts, *ws)


def _peer_candidate_groups(tv_scr):
    k = PEER_TOPK
    s = V7X_SUBLANES
    a_lo = tv_scr[0:s, :]
    a_hi = tv_scr[s:k, :]
    b = [tv_scr[k + j:k + j + 1, :] for j in range(s)]
    b_hi = tv_scr[k + s:2 * k, :]
    row = lax.broadcasted_iota(jnp.int32, a_lo.shape, 0)
    groups = [a_lo + b[0], a_hi + b[0], a_lo + b[1]]
    for j in range(2, s):
        groups.append(jnp.where(row < k // (j + 1), a_lo + b[j], NEG_INF))
    groups.append(tv_scr[0:1, :] + b_hi)
    return groups


def _peer_kernel(h_ref, nw_ref, wq_ref, wql_ref, sk_ref, skl_ref, u_ref, vt_ref, fw_ref, out_ref,
                 xn_scr, n1_scr, e1_scr, r2_scr, e2_scr, sc_scr, rk_scr, tv_scr, acc_scr, *, tokens, eblk, final_norm):
    e = pl.program_id(1)
    k = PEER_TOPK
    half = PEER_QDIM // 2
    unranked = float(k + 1)

    @pl.when(e == 0)
    def _():
        x = h_ref[...]
        xf = x * lax.rsqrt(jnp.mean(x * x, axis=-1, keepdims=True) + RMS_EPS) * nw_ref[...]
        xn = _bf(xf)
        xn_scr[...] = xn
        xl = _bf(xf - xn.astype(f32))
        qf = _dot(xn, wq_ref[...]) + (_dot(xn, wql_ref[...]) + _dot(xl, wq_ref[...]))
        q = _bf(qf)
        ql = _bf(qf - q.astype(f32))
        lane_tiles = [slice(lt * V7X_LANES, (lt + 1) * V7X_LANES) for lt in range(tokens // V7X_LANES)]
        for h in range(PEER_HEADS):
            for p in range(2):
                hp = 2 * h + p
                qcols = slice(hp * half, (hp + 1) * half)
                s = _dot_nt(sk_ref[hp], q[:, qcols]) + (_dot_nt(sk_ref[hp], ql[:, qcols])
                                                        + _dot_nt(skl_ref[hp], q[:, qcols]))
                if p == 0:
                    e1_scr[h] = s
                else:
                    sc_scr[...] = s
                for lanes in lane_tiles:
                    cur = s[:, lanes]
                    for i in range(k):
                        m = jnp.max(cur, axis=0, keepdims=True)
                        tv_scr[p * k + i:p * k + i + 1, lanes] = m
                        cur = jnp.where(cur >= m, -RANK_CODE * (1.0 + (i + 1) / 64.0), cur)
                    rank = jnp.where(cur <= -RANK_CODE, (cur * (-1.0 / RANK_CODE) - 1.0) * 64.0, unranked)
                    if p == 0:
                        rk_scr[:, lanes] = rank
                    else:
                        r2_scr[h, :, lanes] = _bf(rank)
            cur = _peer_candidate_groups(tv_scr)
            thr = None
            for i in range(k):
                m = cur[0]
                for g in cur[1:]:
                    m = jnp.maximum(m, g)
                thr = jnp.max(m, axis=0, keepdims=True)
                cur = [jnp.where(g >= thr, NEG_INF, g) for g in cur]
            top1 = tv_scr[0:1, :]
            top2 = tv_scr[k:k + 1, :]
            tops1 = tv_scr[0:k, :]
            cnt = jnp.zeros(tops1.shape, f32)
            z = jnp.zeros_like(thr)
            for j in range(k):
                sums = tops1 + tv_scr[k + j:k + j + 1, :]
                ok = sums >= thr
                cnt = cnt + jnp.where(ok, 1.0, 0.0)
                z = z + jnp.sum(jnp.where(ok, jnp.exp(sums - (top1 + top2)), 0.0), axis=0, keepdims=True)
            scale2 = 0.5 / z
            for lanes in lane_tiles:
                rank1 = rk_scr[:, lanes]
                n1 = jnp.zeros(rank1.shape, f32)
                for i in range(k):
                    n1 = jnp.where(rank1 == float(i + 1), cnt[i:i + 1, lanes], n1)
                n1_scr[h, :, lanes] = n1
                e1_scr[h, :, lanes] = jnp.exp(e1_scr[h, :, lanes] - top1[:, lanes])
                e2_scr[h, :, lanes] = _bf(jnp.exp(sc_scr[:, lanes] - top2[:, lanes]) * scale2[:, lanes])
        acc_scr[...] = jnp.zeros_like(acc_scr)

    nb1 = eblk // PEER_KEYS
    nlt = tokens // V7X_LANES
    pack = 2 * V7X_SUBLANES
    xu = _dot_nt(u_ref[0], xn_scr[...])
    act2 = xu + xu * jnp.tanh(xu * (GELU_C1 + GELU_C3 * (xu * xu)))
    zero = jnp.zeros((), bf16)
    parts = []
    for ip in range(nb1 // 2):
        i1s = [e * nb1 + 2 * ip + j for j in range(2)]
        nrows = [[n1_scr[h, pl.ds(i1, 1), :] for h in range(PEER_HEADS)] for i1 in i1s]
        erows = [[e1_scr[h, pl.ds(i1, 1), :] for h in range(PEER_HEADS)] for i1 in i1s]
        tiles = [[None] * nlt for _ in range(2)]
        for lt in range(nlt):
            lanes = slice(lt * V7X_LANES, (lt + 1) * V7X_LANES)
            g = [None, None]
            for h in range(PEER_HEADS):
                r2t = r2_scr[h, :, lanes]
                e2t = e2_scr[h, :, lanes]
                for j in range(2):
                    nb = _bf(jnp.broadcast_to(nrows[j][h][:, lanes], (pack, V7X_LANES)))
                    eb = _bf(jnp.broadcast_to(erows[j][h][:, lanes], (pack, V7X_LANES)))
                    nb = jnp.concatenate([nb] * (PEER_KEYS // pack), axis=0)
                    eb = jnp.concatenate([eb] * (PEER_KEYS // pack), axis=0)
                    hit = jnp.where(r2t <= nb, e2t * eb, zero)
                    g[j] = hit if g[j] is None else g[j] + hit
            for j in range(2):
                tiles[j][lt] = g[j]
        for j in range(2):
            il = 2 * ip + j
            gate = tiles[j][0] if nlt == 1 else jnp.concatenate(tiles[j], axis=1)
            parts.append(gate * _bf(act2[il * PEER_KEYS:(il + 1) * PEER_KEYS, :]))
    acc_scr[...] += _dot(vt_ref[0], jnp.concatenate(parts, axis=0))

    @pl.when(e == pl.num_programs(1) - 1)
    def _():
        y = h_ref[...] + acc_scr[...].T
        if final_norm:
            y = y * lax.rsqrt(jnp.mean(y * y, axis=-1, keepdims=True) + RMS_EPS) * fw_ref[...]
        out_ref[...] = y


def _peer(h, lw, layer, final_w=None):
    rows, d = h.shape
    tokens = min(PEER_TOKENS, rows)
    assert tokens % V7X_LANES == 0 and rows % tokens == 0
    eblk = PEER_EBLK
    n_half = PEER_HEADS * 2
    vmem = (2 * (2 * tokens * d * 4 + 2 * d * d * 2 + 2 * eblk * d * 2)
            + tokens * d * 2 + 4 * PEER_HEADS * PEER_KEYS * tokens * 4 + d * tokens * 4
            + 5 * eblk * tokens * 4)
    return pl.pallas_call(
        functools.partial(_peer_kernel, tokens=tokens, eblk=eblk, final_norm=final_w is not None),
        out_shape=jax.ShapeDtypeStruct((rows, d), f32),
        grid=(rows // tokens, PEER_EXPERTS // eblk),
        in_specs=[pl.BlockSpec((tokens, d), lambda i, e: (i, 0)),
                  pl.BlockSpec((1, d), lambda i, e: (0, 0)),
                  pl.BlockSpec((d, PEER_HEADS * PEER_QDIM), lambda i, e: (0, 0)),
                  pl.BlockSpec((d, PEER_HEADS * PEER_QDIM), lambda i, e: (0, 0)),
                  pl.BlockSpec((n_half, PEER_KEYS, PEER_QDIM // 2), lambda i, e: (0, 0, 0)),
                  pl.BlockSpec((n_half, PEER_KEYS, PEER_QDIM // 2), lambda i, e: (0, 0, 0)),
                  pl.BlockSpec((1, eblk, d), lambda i, e: (layer, e, 0)),
                  pl.BlockSpec((1, d, eblk), lambda i, e: (layer, 0, e)),
                  pl.BlockSpec((1, d), lambda i, e: (0, 0))],
        out_specs=pl.BlockSpec((tokens, d), lambda i, e: (i, 0)),
        scratch_shapes=[pltpu.VMEM((tokens, d), bf16),
                        pltpu.VMEM((PEER_HEADS, PEER_KEYS, tokens), f32),
                        pltpu.VMEM((PEER_HEADS, PEER_KEYS, tokens), f32),
                        pltpu.VMEM((PEER_HEADS, PEER_KEYS, tokens), bf16),
                        pltpu.VMEM((PEER_HEADS, PEER_KEYS, tokens), bf16),
                        pltpu.VMEM((PEER_KEYS, tokens), f32), pltpu.VMEM((PEER_KEYS, tokens), f32),
                        pltpu.VMEM((2 * PEER_TOPK, tokens), f32),
                        pltpu.VMEM((d, tokens), f32)],
        compiler_params=_cparams(("parallel", "arbitrary"), vmem),
        name="peer",
    )(h, lw["norm_ffn"], lw["peer_wq"], lw["peer_wq_lo"], lw["peer_sk"], lw["peer_sk_lo"], lw["peer_u"], lw["peer_vt"],
      (lw["norm_ffn"] if final_w is None else final_w).reshape(1, d))


def _layer_weights(l, p):
    w_in = p["w_in"][l]
    o_b, o_c, o_g = A_COLS, A_COLS + B_COLS, A_COLS + B_COLS + C_COLS
    lane = jnp.arange(V7X_LANES)
    on_alpha = (lane >= DN_HEADS) & (lane < 2 * DN_HEADS)

    def alpha_lanes(vec):
        return jnp.where(on_alpha, jnp.pad(vec, (DN_HEADS, V7X_LANES - 2 * DN_HEADS)), 0.0)

    sk = p["peer_subkeys"][l].reshape(PEER_HEADS * 2, PEER_KEYS, PEER_QDIM // 2)
    rw_vec = jnp.stack([p["rw_w0"][l], p["rw_a0"][l], p["rw_k_k"][l], p["rw_k_a"][l],
                        p["rw_r_k"][l].reshape(RW_WIDTH), p["rw_ln_w"][l], p["rw_ln_b"][l],
                        jnp.zeros((RW_WIDTH,), f32)])
    return {
        "norm_mix": p["norm_mix"][l],
        "w_a": _bf(w_in[:, 0:A_COLS]),
        "w_b": _bf(w_in[:, o_b:o_b + B_MAIN]),
        "w_bg": _bf(jnp.pad(w_in[:, o_b + B_MAIN:o_c], ((0, 0), (0, V7X_LANES - 2 * DN_HEADS)))),
        "w_c": _bf(w_in[:, o_c:o_g]),
        "w_g": _bf(w_in[:, o_g:]),
        "dn_conv_w": p["dn_conv_w"][l],
        "dn_prm": jnp.zeros((V7X_SUBLANES, V7X_LANES), f32)
                  .at[0].set(alpha_lanes(p["dn_a_log"][l])).at[1].set(alpha_lanes(p["dn_dt_bias"][l])),
        "dn_norm_w": jnp.tile(p["dn_norm_w"][l], DN_HEADS).reshape(1, DN_WIDTH),
        "bd128": _block_diag_ones(DN_WIDTH, DN_HD),
        "bd64": _block_diag_ones(RW_WIDTH, RW_HD),
        "rw_mu": p["rw_mu"][l].reshape(1, C_COLS),
        "rw_vec": rw_vec,
        "rw_w2": _bf(p["rw_w2"][l]), "rw_a2": _bf(p["rw_a2"][l]), "rw_g2": _bf(p["rw_g2"][l]),
        "w_br_a": _bf(p["w_br_a"][l]), "w_br_b": _bf(p["w_br_b"][l]), "w_br_c": _bf(p["w_br_c"][l]),
        "w_out": _bf(p["w_out"][l]),
        "norm_ffn": p["norm_ffn"][l].reshape(1, D_MODEL),
        "peer_wq": _bf(p["peer_wq"][l]),
        "peer_wq_lo": _bf(p["peer_wq"][l] - _bf(p["peer_wq"][l]).astype(f32)),
        "peer_sk": _bf(sk),
        "peer_sk_lo": _bf(sk - _bf(sk).astype(f32)),
        "peer_u": p["peer_u_bf"],
        "peer_vt": p["peer_vt_bf"],
    }


def _project(h, lw):
    nm = lw["norm_mix"]
    return {s: _norm_matmul(h, nm, lw["w_" + s]) for s in ("a", "b", "bg", "c", "g")}


def _kv_rows(pa, gi, lo, hi):
    k = pa[lo:hi, ATT_WIDTH + gi * ATT_OUT:ATT_WIDTH + (gi + 1) * ATT_OUT]
    v = pa[lo:hi, 2 * ATT_WIDTH + gi * ATT_OUT:2 * ATT_WIDTH + (gi + 1) * ATT_OUT]
    return jnp.stack([k, v], axis=1).reshape(hi - lo, 2, ATT_HPG, ATT_HD)


def _prompt_layer(h, lw, layer, final_w):
    t = h.shape[0]
    assert t % ATT_TILE == 0 and t % SEQ_TILE == 0
    pr = _project(h, lw)
    groups = [_attn_prompt_group(pr["a"], t, gi) for gi in range(len(ATT_GROUPS))]
    ob, dn_s = _dn_mixer(pr["b"], pr["bg"], jnp.zeros((1, V7X_SUBLANES, B_QKV), f32),
                         jnp.zeros((1, DN_HEADS, DN_HD, DN_HD), f32), lw,
                         nseq=1, tt=SEQ_TILE, chunk=DN_CHUNK, valid=(0, DN_CHUNK), nstate=1)
    oc, rw_s = _rw_mixer(pr["c"], jnp.zeros((1, 1, C_COLS), f32),
                         jnp.zeros((1, RW_HEADS, RW_HD, RW_HD), f32), lw,
                         nseq=1, tt=SEQ_TILE, chunk=RW_CHUNK, valid=(0, RW_CHUNK), nstate=1)
    h = _merge(h, [g[0] for g in groups], [g[1] for g in groups], ob, oc, pr["g"], lw)
    h = _peer(h, lw, layer, final_w)
    kvs = [_kv_rows(pr["a"], gi, t - min(win, t), t)[None] for gi, (win, _) in enumerate(ATT_GROUPS)]
    conv = pr["b"][t - (DN_CONV - 1):t, 0:B_QKV][None]
    shift = pr["c"][t - 1:t]
    return h, kvs, dn_s, conv, rw_s, shift


def _sample_rows(hist, x, nseq, seq):
    n = x.shape[1]
    nh = hist.shape[1]
    rows = jnp.concatenate([hist, x.reshape(nseq, seq, n), jnp.zeros((nseq, SAMPLE_ROWS - nh - seq, n), f32)], axis=1)
    return rows.reshape(nseq * SAMPLE_ROWS, n)


def _new_rows(x, nseq, nh, seq):
    return x.reshape(nseq, SAMPLE_ROWS, x.shape[1])[:, nh:nh + seq].reshape(nseq * seq, x.shape[1])


def _sample_layer(h, lw, layer, final_w, caches, dn_state, dn_conv, rw_state, rw_shift, nseq, seq):
    pr = _project(h, lw)
    oa = _attn_sample(pr["a"], caches, layer, nseq, seq)
    nh_dn, nh_rw = DN_CONV - 1, 1
    assert nseq % SAMPLE_STEP == 0 and nh_dn + seq <= SAMPLE_ROWS
    steps = nseq // SAMPLE_STEP
    tt = SAMPLE_STEP * SAMPLE_ROWS
    conv_hist = jnp.pad(dn_conv, ((0, 0), (0, 0), (0, B_MAIN - B_QKV)))
    ob, dn_s = _dn_mixer(_sample_rows(conv_hist, pr["b"], nseq, seq),
                         _sample_rows(jnp.zeros((nseq, nh_dn, V7X_LANES), f32), pr["bg"], nseq, seq),
                         jnp.zeros((steps, V7X_SUBLANES, B_QKV), f32), dn_state, lw,
                         nseq=steps, tt=tt, chunk=SAMPLE_ROWS, valid=(nh_dn, nh_dn + seq), nstate=SAMPLE_STEP)
    oc, rw_s = _rw_mixer(_sample_rows(rw_shift[:, None, :], pr["c"], nseq, seq),
                         jnp.zeros((steps, 1, C_COLS), f32), rw_state, lw,
                         nseq=steps, tt=tt, chunk=SAMPLE_ROWS, valid=(nh_rw, nh_rw + seq), nstate=SAMPLE_STEP)
    h = _merge(h, oa, None, _new_rows(ob, nseq, nh_dn, seq), _new_rows(oc, nseq, nh_rw, seq), pr["g"], lw)
    h = _peer(h, lw, layer, final_w)
    kvs = [_kv_rows(pr["a"], gi, 0, nseq * seq).reshape(nseq, seq, 2, ATT_HPG, ATT_HD)
           for gi in range(len(ATT_GROUPS))]
    conv = pr["b"][:, 0:B_QKV].reshape(nseq, seq, B_QKV)[:, seq - (DN_CONV - 1):]
    shift = pr["c"].reshape(nseq, seq, C_COLS)[:, seq - 1]
    return h, kvs, dn_s, conv, rw_s, shift


def kernel(x_prompt, x_sample, cache_kv_w128, cache_kv_w512, cache_kv_w2048, state_dn, state_dn_conv, state_rw, state_rw_shift, norm_mix, w_in, dn_conv_w, dn_a_log, dn_dt_bias, dn_norm_w, rw_mu, rw_w0, rw_w2, rw_a0, rw_a2, rw_g2, rw_k_k, rw_k_a, rw_r_k, rw_ln_w, rw_ln_b, w_br_a, w_br_b, w_br_c, w_out, norm_ffn, peer_wq, peer_subkeys, peer_u, peer_v, norm_final):
    p = dict(norm_mix=norm_mix, w_in=w_in, dn_conv_w=dn_conv_w, dn_a_log=dn_a_log, dn_dt_bias=dn_dt_bias,
             dn_norm_w=dn_norm_w, rw_mu=rw_mu, rw_w0=rw_w0, rw_w2=rw_w2, rw_a0=rw_a0, rw_a2=rw_a2, rw_g2=rw_g2,
             rw_k_k=rw_k_k, rw_k_a=rw_k_a, rw_r_k=rw_r_k, rw_ln_w=rw_ln_w, rw_ln_b=rw_ln_b, w_br_a=w_br_a,
             w_br_b=w_br_b, w_br_c=w_br_c, w_out=w_out, norm_ffn=norm_ffn, peer_wq=peer_wq,
             peer_subkeys=peer_subkeys, peer_u=peer_u, peer_v=peer_v)
    depth = w_in.shape[0]
    p["peer_u_bf"] = _bf(peer_u)
    p["peer_vt_bf"] = jnp.swapaxes(_bf(peer_v), 1, 2)
    bp, t, d = x_prompt.shape
    nseq, seq, _ = x_sample.shape
    assert bp == 1 and d == D_MODEL and seq <= SAMPLE_ROWS // 2 and DN_CONV - 1 <= seq
    caches = [c.reshape(depth, nseq, c.shape[2], 2 * ATT_OUT) for c in (cache_kv_w128, cache_kv_w512, cache_kv_w2048)]
    hp = x_prompt.reshape(t, d)
    hs = x_sample.reshape(nseq * seq, d)
    outs_p, outs_s = [], []
    for l in range(depth):
        lw = _layer_weights(l, p)
        final_w = norm_final if l == depth - 1 else None
        hp, *st_p = _prompt_layer(hp, lw, l, final_w)
        hs, *st_s = _sample_layer(hs, lw, l, final_w, caches, state_dn[l], state_dn_conv[l], state_rw[l],
                                  state_rw_shift[l], nseq, seq)
        outs_p.append(st_p)
        outs_s.append(st_s)
    y_p = hp.reshape(bp, t, d)
    y_s = hs.reshape(nseq, seq, d)

    def stack(outs, pick):
        return jnp.stack([pick(o) for o in outs], axis=0)

    res = [y_p, y_s]
    for gi in range(len(ATT_GROUPS)):
        res.append(stack(outs_p, lambda o: o[0][gi]))
        res.append(stack(outs_s, lambda o: o[0][gi]))
    for idx in (1, 2, 3, 4):
        res.append(stack(outs_p, lambda o: o[idx]))
        res.append(stack(outs_s, lambda o: o[idx]))
    return tuple(res)
```

```python
import functools
import math

import jax
import jax.numpy as jnp
import numpy as np
from jax import lax
from jax.experimental import pallas as pl
from jax.experimental.pallas import tpu as pltpu

f32 = jnp.float32
bf16 = jnp.bfloat16

V7X_LANES = 128
V7X_SUBLANES = 8
V7X_VMEM_BYTES = 64 * 1024 * 1024
VMEM_CEILING = 56 * 1024 * 1024

D_MODEL = 1024
ATT_GROUPS = ((128, 1), (512, 4), (2048, 16))
ATT_HPG = 4
ATT_HD = 64
ATT_HEADS = ATT_HPG * len(ATT_GROUPS)
ATT_WIDTH = ATT_HEADS * ATT_HD
ATT_OUT = ATT_HPG * ATT_HD
ATT_STEPS = 128
DN_HEADS = 4
DN_HD = 128
DN_WIDTH = DN_HEADS * DN_HD
DN_CONV = 4
DN_CHUNK = 64
DN_GROUP = 8
RW_HEADS = 8
RW_HD = 64
RW_WIDTH = RW_HEADS * RW_HD
RW_DECAY_LORA = 64
RW_A_LORA = 64
RW_GATE_LORA = 128
RW_GN_EPS = 64e-5
RW_CHUNK = 64
RW_GROUP = 4
PEER_KEYS = 128
PEER_EXPERTS = PEER_KEYS * PEER_KEYS
PEER_HEADS = 8
PEER_QDIM = 128
PEER_TOPK = 16
RMS_EPS = 1e-6
L2_EPS = 1e-6

A_COLS = 3 * ATT_WIDTH
B_QKV = 3 * DN_WIDTH
B_MAIN = 4 * DN_WIDTH
B_COLS = B_MAIN + 2 * DN_HEADS
C_COLS = 3 * RW_WIDTH + RW_DECAY_LORA + RW_A_LORA + RW_GATE_LORA
G_COLS = 3 * D_MODEL

ROW_TILE = 512
PROJ_TILE = 1024
ATT_TILE = 2048
ATT_GROUP = 4
SEQ_TILE = 512
SAMPLE_ROWS = 8
SAMPLE_STEP = 8
PEER_TOKENS = 512
PEER_EBLK = 1024

NEG_INF = float("-inf")
RANK_CODE = 2.0 ** 100
GELU_C1 = math.sqrt(2.0 / math.pi)
GELU_C3 = 0.044715 * GELU_C1


def _alibi_slopes():
    h = np.arange(1, ATT_HEADS + 1, dtype=np.float32)
    return np.power(np.float32(2.0), -8.0 * h / ATT_HEADS).astype(np.float32)


def _cparams(semantics, vmem_bytes):
    return pltpu.CompilerParams(dimension_semantics=semantics,
                                vmem_limit_bytes=int(min(max(vmem_bytes, 16 * 1024 * 1024), VMEM_CEILING)))


def _dot(a, b):
    return jnp.dot(a, b, preferred_element_type=f32)


def _dot_nt(a, b):
    return lax.dot_general(a, b, (((1,), (1,)), ((), ())), preferred_element_type=f32)


def _dot_tn(a, b):
    return lax.dot_general(a, b, (((0,), (0,)), ((), ())), preferred_element_type=f32)


def _bf(x):
    return x.astype(bf16)


def _sigmoid(x):
    return 1.0 / (1.0 + jnp.exp(-x))


def _segsum(x, bd):
    hi = _bf(x)
    lo = _bf(x - hi.astype(f32))
    return _dot(hi, bd) + _dot(lo, bd)


def _seg_cumsum(x, rowc, seg):
    k = 1
    while k < seg:
        x = x + jnp.where(rowc >= k, pltpu.roll(x, k, 0), 0.0)
        k *= 2
    return x


def _block_diag_ones(width, seg):
    i = np.arange(width)
    return jnp.asarray((i[:, None] // seg) == (i[None, :] // seg), dtype=bf16)


def _norm_matmul_kernel(h_ref, nw_ref, w_ref, o_ref):
    x = h_ref[...]
    xn = x * lax.rsqrt(jnp.mean(x * x, axis=-1, keepdims=True) + RMS_EPS) * nw_ref[...]
    o_ref[...] = _dot(_bf(xn), w_ref[...])


def _norm_matmul(h, nw, w):
    rows, d = h.shape
    n = w.shape[1]
    tr = min(PROJ_TILE, rows)
    vmem = 2 * (tr * d * 4 + d * n * 2 + tr * n * 4) + 4 * tr * d * 4
    return pl.pallas_call(
        _norm_matmul_kernel,
        out_shape=jax.ShapeDtypeStruct((rows, n), f32),
        grid=(rows // tr,),
        in_specs=[pl.BlockSpec((tr, d), lambda i: (i, 0)),
                  pl.BlockSpec((1, d), lambda i: (0, 0)),
                  pl.BlockSpec((d, n), lambda i: (0, 0))],
        out_specs=pl.BlockSpec((tr, n), lambda i: (i, 0)),
        compiler_params=_cparams(("parallel",), vmem),
        name="norm_matmul",
    )(h, nw.reshape(1, d), w)


def _attn_prompt_kernel(q_ref, kc_ref, vc_ref, kp_ref, vp_ref, o_ref, lse_ref, kk_scr, vv_scr, *, dil, slopes):
    n = ATT_STEPS
    nblk = ATT_TILE // (dil * n)
    tile = pl.program_id(0)
    pair = pl.program_id(1)
    kk_scr[0:ATT_TILE, :] = kp_ref[...]
    kk_scr[ATT_TILE:2 * ATT_TILE, :] = kc_ref[...]
    vv_scr[0:ATT_TILE, :] = vp_ref[...]
    vv_scr[ATT_TILE:2 * ATT_TILE, :] = vc_ref[...]
    ii = lax.broadcasted_iota(jnp.int32, (n, 2 * n), 0)
    jj = lax.broadcasted_iota(jnp.int32, (n, 2 * n), 1)
    steps = n + ii - jj
    band = (steps >= 0) & (steps <= n)
    dist = (steps * dil).astype(f32)
    biases = []
    for hh in range(2):
        slope = jnp.where(pair == 0, slopes[hh], slopes[2 + hh])
        biases.append(jnp.where(band, -slope * dist, NEG_INF))

    def body(gidx, carry):
        items = []
        for cc in range(ATT_GROUP):
            c = gidx * ATT_GROUP + cc
            r = c % dil
            b = c // dil
            qs = r + dil * n * b
            rows_q = pl.ds(qs, n, stride=dil)
            rows_k = pl.ds(ATT_TILE + qs - dil * n, 2 * n, stride=dil)
            q = q_ref[rows_q, :] * (ATT_HD ** -0.5)
            k = kk_scr[rows_k, :]
            first_key = jnp.where(tile * nblk + b == 0, n, 0)
            ss = [_dot_nt(_bf(q[:, hh * ATT_HD:(hh + 1) * ATT_HD]), _bf(k[:, hh * ATT_HD:(hh + 1) * ATT_HD]))
                  for hh in range(2)]
            items.append(dict(rows_q=rows_q, rows_k=rows_k, first_key=first_key, s=ss))
        for it in items:
            it["p"], it["den"], it["lse"] = [], [], []
            for hh in range(2):
                s = jnp.where(jj >= it["first_key"], it["s"][hh] + biases[hh], NEG_INF)
                mx = jnp.max(s, axis=-1, keepdims=True)
                p = jnp.exp(s - mx)
                den = jnp.sum(p, axis=-1, keepdims=True)
                it["p"].append(_bf(p))
                it["den"].append(den)
                it["lse"].append(jnp.broadcast_to(mx + jnp.log(den), (n, ATT_HD)))
        for it in items:
            v = vv_scr[it["rows_k"], :]
            outs = [_dot(it["p"][hh], _bf(v[:, hh * ATT_HD:(hh + 1) * ATT_HD])) / it["den"][hh] for hh in range(2)]
            o_ref[it["rows_q"], :] = jnp.concatenate(outs, axis=1)
            lse_ref[it["rows_q"], :] = jnp.concatenate(it["lse"], axis=1)
        return carry

    lax.fori_loop(0, dil * nblk // ATT_GROUP, body, 0)


def _attn_prompt_group(pa, seq_len, gi):
    _, dil = ATT_GROUPS[gi]
    slopes = tuple(float(s) for s in _alibi_slopes()[gi * ATT_HPG:(gi + 1) * ATT_HPG])
    w = V7X_LANES
    qb, kb, vb = (gi * 2, 6 + gi * 2, 12 + gi * 2)
    blk = (ATT_TILE, w)
    cur = lambda base: pl.BlockSpec(blk, lambda i, j, base=base: (i, base + j))
    prev = lambda base: pl.BlockSpec(blk, lambda i, j, base=base: (jnp.maximum(i - 1, 0), base + j))
    vmem = 2 * 7 * ATT_TILE * w * 4 + 2 * 2 * ATT_TILE * w * 4 + 8 * 1024 * 1024
    return pl.pallas_call(
        functools.partial(_attn_prompt_kernel, dil=dil, slopes=slopes),
        out_shape=(jax.ShapeDtypeStruct((seq_len, ATT_OUT), f32), jax.ShapeDtypeStruct((seq_len, ATT_OUT), f32)),
        grid=(seq_len // ATT_TILE, 2),
        in_specs=[cur(qb), cur(kb), cur(vb), prev(kb), prev(vb)],
        out_specs=(pl.BlockSpec(blk, lambda i, j: (i, j)), pl.BlockSpec(blk, lambda i, j: (i, j))),
        scratch_shapes=[pltpu.VMEM((2 * ATT_TILE, w), f32), pltpu.VMEM((2 * ATT_TILE, w), f32)],
        compiler_params=_cparams(("parallel", "parallel"), vmem),
        name=f"attn_prompt_g{gi}",
    )(pa, pa, pa, pa, pa)


def _attn_sample_kernel(q_ref, k_ref, v_ref, c0_ref, c1_ref, c2_ref, o_ref, *, slopes, seq):
    rows = 2 * seq
    caches = (c0_ref, c1_ref, c2_ref)
    row = lax.broadcasted_iota(jnp.int32, (rows, 1), 0)
    own0 = row < seq
    qpos = row % seq
    npos = lax.broadcasted_iota(jnp.int32, (rows, rows), 1)
    q_all = q_ref[0] * (ATT_HD ** -0.5)
    k_all = k_ref[0]
    v_all = v_ref[0]
    outs, lses = [], []
    for gi, (win, dil) in enumerate(ATT_GROUPS):
        cpos = lax.broadcasted_iota(jnp.int32, (rows, win), 1)
        dist_c = win + qpos - cpos
        ok_c = (dist_c <= win) & ((dist_c & (dil - 1)) == 0)
        dist_n = row - npos
        ok_n = (dist_n >= 0) & (dist_n <= qpos) & ((dist_n & (dil - 1)) == 0)
        dcf = dist_c.astype(f32)
        dnf = dist_n.astype(f32)
        for h in range(ATT_HPG):
            slope = slopes[gi * ATT_HPG + h]
            cols = slice(gi * ATT_OUT + h * ATT_HD, gi * ATT_OUT + (h + 1) * ATT_HD)
            kcols = slice(h * ATT_HD, (h + 1) * ATT_HD)
            vcols = slice(ATT_OUT + h * ATT_HD, ATT_OUT + (h + 1) * ATT_HD)
            qh = _bf(q_all[:, cols])
            sn = jnp.where(ok_n, _dot_nt(qh, _bf(k_all[:, cols])) - slope * dnf, NEG_INF)
            scs = []
            for b in range(2):
                sc = _dot_nt(qh, _bf(caches[gi][0, b, :, kcols])) - slope * dcf
                scs.append(jnp.where(ok_c, sc, NEG_INF))
            sc = jnp.where(own0, scs[0], scs[1])
            mx = jnp.maximum(jnp.max(sc, axis=-1, keepdims=True), jnp.max(sn, axis=-1, keepdims=True))
            pc = jnp.exp(sc - mx)
            pn = jnp.exp(sn - mx)
            den = jnp.sum(pc, axis=-1, keepdims=True) + jnp.sum(pn, axis=-1, keepdims=True)
            pcb = _bf(pc)
            oc = jnp.where(own0, _dot(pcb, _bf(caches[gi][0, 0, :, vcols])), _dot(pcb, _bf(caches[gi][0, 1, :, vcols])))
            o = (oc + _dot(_bf(pn), _bf(v_all[:, cols]))) / den
            outs.append(o)
            lses.append(mx + jnp.log(den))
    merged = []
    for h in range(ATT_HPG):
        ls = [lses[gi * ATT_HPG + h] for gi in range(len(ATT_GROUPS))]
        mx = jnp.maximum(jnp.maximum(ls[0], ls[1]), ls[2])
        es = [jnp.exp(l - mx) for l in ls]
        den = es[0] + es[1] + es[2]
        acc = es[0] * outs[h] + es[1] * outs[ATT_HPG + h] + es[2] * outs[2 * ATT_HPG + h]
        merged.append(acc / den)
    o_ref[0] = jnp.concatenate(merged, axis=1)


def _attn_sample(pa, caches, layer, nseq, seq):
    rows = 2 * seq
    assert rows == SAMPLE_ROWS and nseq % 2 == 0
    slopes = tuple(float(s) for s in _alibi_slopes())
    pa3 = pa.reshape(nseq // 2, rows, A_COLS)
    in_specs = [pl.BlockSpec((1, rows, ATT_WIDTH), lambda i, c=c: (i, 0, c)) for c in range(3)]
    vmem = 0
    for (win, _), c in zip(ATT_GROUPS, caches):
        assert c.shape[2] == win, "cached window must hold exactly `window` rows"
        in_specs.append(pl.BlockSpec((1, 2, win, 2 * ATT_OUT), lambda i: (layer, i, 0, 0)))
        vmem += 2 * 2 * win * 2 * ATT_OUT * 4
    out = pl.pallas_call(
        functools.partial(_attn_sample_kernel, slopes=slopes, seq=seq),
        out_shape=jax.ShapeDtypeStruct((nseq // 2, rows, ATT_OUT), f32),
        grid=(nseq // 2,),
        in_specs=in_specs,
        out_specs=pl.BlockSpec((1, rows, ATT_OUT), lambda i: (i, 0, 0)),
        compiler_params=_cparams(("parallel",), vmem + 16 * 1024 * 1024),
        name="attn_sample",
    )(pa3, pa3, pa3, *caches)
    return out.reshape(nseq * seq, ATT_OUT)


def _dn_kernel(pb_ref, pbg_ref, conv0_ref, s0_ref, cw_ref, prm_ref, nw_ref, bd_ref,
               o_ref, sout_ref,
               xs_scr, q_scr, k_scr, v_scr, g_scr, b_scr, oc_scr, qp_scr, m_scr, n_scr, s_scr,
               *, tt, chunk, valid, nstate):
    t = pl.program_id(1)

    @pl.when(t == 0)
    def _():
        xs_scr[0:V7X_SUBLANES, :] = conv0_ref[0]
        if nstate == 1:
            s_scr[...] = s0_ref[0]

    x = pb_ref[:, 0:B_QKV]
    xs_scr[pl.ds(V7X_SUBLANES, tt), :] = x
    cw = cw_ref[...]
    y = x * cw[3:4, :]
    for j in range(DN_CONV - 1):
        y = y + xs_scr[pl.ds(V7X_SUBLANES - (DN_CONV - 1) + j, tt), :] * cw[j:j + 1, :]
    xs_scr[0:V7X_SUBLANES, :] = xs_scr[pl.ds(tt, V7X_SUBLANES), :]
    y = y * _sigmoid(y)
    bd = bd_ref[...]
    q = y[:, 0:DN_WIDTH]
    k = y[:, DN_WIDTH:2 * DN_WIDTH]
    q_scr[...] = q * lax.rsqrt(_segsum(q * q, bd) + L2_EPS) * (DN_HD ** -0.5)
    k_scr[...] = k * lax.rsqrt(_segsum(k * k, bd) + L2_EPS)
    v_scr[...] = y[:, 2 * DN_WIDTH:3 * DN_WIDTH]
    pg = pbg_ref[...]
    beta = _sigmoid(pg)
    g = -jnp.exp(prm_ref[0:1, :]) * jax.nn.softplus(pg + prm_ref[1:2, :])
    rown = lax.broadcasted_iota(jnp.int32, (tt, V7X_LANES), 0)
    if valid != (0, chunk):
        rc = rown & (chunk - 1)
        beta = jnp.where(rc >= valid[0], jnp.where(rc < valid[1], beta, 0.0), 0.0)
        g = jnp.where(rc >= valid[0], jnp.where(rc < valid[1], g, 0.0), 0.0)
    g_scr[...] = _seg_cumsum(g, rown & (chunk - 1), chunk)
    b_scr[...] = beta

    ci = lax.broadcasted_iota(jnp.int32, (chunk, chunk), 0)
    cj = lax.broadcasted_iota(jnp.int32, (chunk, chunk), 1)
    incl = ci >= cj
    strict = ci > cj

    nchunks = tt // chunk
    group = min(DN_GROUP, nchunks)

    def precompute(gi, carry):
        items = []
        for cc in range(group):
            cidx = gi * group + cc
            rows = pl.ds(pl.multiple_of(cidx * chunk, chunk), chunk)
            gc = g_scr[rows, :]
            bc = b_scr[rows, :]
            gct = gc.T
            for h in range(DN_HEADS):
                cols = slice(h * DN_HD, (h + 1) * DN_HD)
                qh, kh, vh = q_scr[rows, cols], k_scr[rows, cols], v_scr[rows, cols]
                gcol = gc[:, DN_HEADS + h:DN_HEADS + h + 1]
                grow = gct[DN_HEADS + h:DN_HEADS + h + 1, :]
                bcol = bc[:, h:h + 1]
                glast = gc[chunk - 1:chunk, DN_HEADS + h:DN_HEADS + h + 1]
                decay = jnp.exp(jnp.where(incl, gcol - grow, NEG_INF))
                kb = kh * bcol
                eg = jnp.exp(gcol)
                both = _dot_nt(_bf(jnp.concatenate([kb, qh], axis=0)), _bf(kh))
                items.append(dict(
                    rows=rows, cols=cols, idx=cidx * DN_HEADS + h,
                    p=jnp.where(strict, -both[0:chunk] * decay, 0.0),
                    attn=jnp.where(incl, both[chunk:2 * chunk] * decay, 0.0),
                    sol=jnp.concatenate([vh * bcol, kb * eg], axis=1),
                    qeg=qh * eg, kdec=_bf(kh * jnp.exp(glast - gcol))))
        k = 1
        while k < chunk:
            for it in items:
                it["sol"] = it["sol"] + _dot(_bf(it["p"]), _bf(it["sol"]))
            if 2 * k < chunk:
                for it in items:
                    pb = _bf(it["p"])
                    it["p"] = _dot(pb, pb)
            k *= 2
        for it in items:
            u = _bf(it["sol"][:, 0:DN_HD])
            w = _bf(it["sol"][:, DN_HD:2 * DN_HD])
            attn = _bf(it["attn"])
            qp_scr[it["rows"], it["cols"]] = it["qeg"] - _dot(attn, w)
            oc_scr[it["rows"], it["cols"]] = _dot(attn, u)
            m_scr[it["idx"]] = _dot_tn(it["kdec"], w)
            n_scr[it["idx"]] = _dot_tn(it["kdec"], u)
        return carry

    lax.fori_loop(0, nchunks // group, precompute, 0)

    def recur(c, carry):
        rows = pl.ds(pl.multiple_of(c * chunk, chunk), chunk)
        g_tail = g_scr[pl.ds(pl.multiple_of((c + 1) * chunk - V7X_SUBLANES, V7X_SUBLANES), V7X_SUBLANES), :]
        states = [s_scr[h] if nstate == 1 else s0_ref[c, h] for h in range(DN_HEADS)]
        sb = [_bf(s) for s in states]
        outs = [_dot(_bf(qp_scr[rows, h * DN_HD:(h + 1) * DN_HD]), sb[h]) for h in range(DN_HEADS)]
        upd = [_dot(_bf(m_scr[c * DN_HEADS + h]), sb[h]) for h in range(DN_HEADS)]
        for h in range(DN_HEADS):
            cols = slice(h * DN_HD, (h + 1) * DN_HD)
            oc_scr[rows, cols] = oc_scr[rows, cols] + outs[h]
            g_end = jnp.exp(g_tail[V7X_SUBLANES - 1:V7X_SUBLANES, DN_HEADS + h:DN_HEADS + h + 1])
            new = states[h] * g_end - upd[h] + n_scr[c * DN_HEADS + h]
            if nstate == 1:
                s_scr[h] = new
            else:
                sout_ref[c, h] = new
        return carry

    lax.fori_loop(0, nchunks, recur, 0)

    o = oc_scr[...]
    z = pb_ref[:, B_QKV:B_MAIN]
    o = o * lax.rsqrt(_segsum(o * o, bd) * (1.0 / DN_HD) + RMS_EPS) * nw_ref[...]
    o_ref[...] = o * (z * _sigmoid(z))

    if nstate == 1:
        @pl.when(t == pl.num_programs(1) - 1)
        def _():
            sout_ref[0] = s_scr[...]


def _dn_mixer(pb, pbg, conv0, s0, lw, *, nseq, tt, chunk, valid, nstate):
    rows = pb.shape[0]
    nt = rows // (nseq * tt)
    nmat = (tt // chunk) * DN_HEADS
    vmem = (2 * (tt * (B_MAIN + V7X_LANES + DN_WIDTH) * 4 + DN_WIDTH * DN_WIDTH * 2)
            + (tt + 8) * B_QKV * 4 + 4 * tt * DN_WIDTH * 4 + 2 * tt * V7X_LANES * 4
            + tt * DN_WIDTH * 4 + (2 * nmat + (6 + 4 * nstate) * DN_HEADS) * DN_HD * DN_HD * 4 + 8 * tt * B_QKV * 4)
    return pl.pallas_call(
        functools.partial(_dn_kernel, tt=tt, chunk=chunk, valid=valid, nstate=nstate),
        out_shape=(jax.ShapeDtypeStruct((rows, DN_WIDTH), f32),
                   jax.ShapeDtypeStruct((nseq * nstate, DN_HEADS, DN_HD, DN_HD), f32)),
        grid=(nseq, nt),
        in_specs=[pl.BlockSpec((tt, B_MAIN), lambda s, t: (s * nt + t, 0)),
                  pl.BlockSpec((tt, V7X_LANES), lambda s, t: (s * nt + t, 0)),
                  pl.BlockSpec((1, V7X_SUBLANES, B_QKV), lambda s, t: (s, 0, 0)),
                  pl.BlockSpec((nstate, DN_HEADS, DN_HD, DN_HD), lambda s, t: (s, 0, 0, 0)),
                  pl.BlockSpec((DN_CONV, B_QKV), lambda s, t: (0, 0)),
                  pl.BlockSpec((V7X_SUBLANES, V7X_LANES), lambda s, t: (0, 0)),
                  pl.BlockSpec((1, DN_WIDTH), lambda s, t: (0, 0)),
                  pl.BlockSpec((DN_WIDTH, DN_WIDTH), lambda s, t: (0, 0))],
        out_specs=(pl.BlockSpec((tt, DN_WIDTH), lambda s, t: (s * nt + t, 0)),
                   pl.BlockSpec((nstate, DN_HEADS, DN_HD, DN_HD), lambda s, t: (s, 0, 0, 0))),
        scratch_shapes=[pltpu.VMEM((tt + V7X_SUBLANES, B_QKV), f32),
                        pltpu.VMEM((tt, DN_WIDTH), f32), pltpu.VMEM((tt, DN_WIDTH), f32),
                        pltpu.VMEM((tt, DN_WIDTH), f32),
                        pltpu.VMEM((tt, V7X_LANES), f32), pltpu.VMEM((tt, V7X_LANES), f32),
                        pltpu.VMEM((tt, DN_WIDTH), f32), pltpu.VMEM((tt, DN_WIDTH), f32),
                        pltpu.VMEM((nmat, DN_HD, DN_HD), f32), pltpu.VMEM((nmat, DN_HD, DN_HD), f32),
                        pltpu.VMEM((DN_HEADS, DN_HD, DN_HD), f32)],
        compiler_params=_cparams(("parallel", "arbitrary"), vmem),
        name="deltanet",
    )(pb, pbg, conv0, s0, lw["dn_conv_w"], lw["dn_prm"], lw["dn_norm_w"], lw["bd128"])


def _rw_kernel(pc_ref, shift0_ref, s0_ref, mu_ref, vec_ref, w2_ref, a2_ref, g2_ref, bd_ref,
               o_ref, sout_ref,
               xs_scr, ar_scr, bk_scr, v_scr, pe_scr, oc_scr, rp_scr, m_scr, n_scr, s_scr, *, tt, chunk, valid, nstate):
    t = pl.program_id(1)

    @pl.when(t == 0)
    def _():
        xs_scr[V7X_SUBLANES - 1:V7X_SUBLANES, :] = shift0_ref[0]
        if nstate == 1:
            s_scr[...] = s0_ref[0]

    pc = pc_ref[...]
    xs_scr[pl.ds(V7X_SUBLANES, tt), :] = pc
    prev = xs_scr[pl.ds(V7X_SUBLANES - 1, tt), :]
    xs_scr[0:V7X_SUBLANES, :] = xs_scr[pl.ds(tt, V7X_SUBLANES), :]
    xc = pc + (prev - pc) * mu_ref[...]
    w3 = 3 * RW_WIDTH
    r = xc[:, 0:RW_WIDTH]
    k = xc[:, RW_WIDTH:2 * RW_WIDTH]
    v = xc[:, 2 * RW_WIDTH:w3]
    wd = xc[:, w3:w3 + RW_DECAY_LORA]
    ad = xc[:, w3 + RW_DECAY_LORA:w3 + RW_DECAY_LORA + RW_A_LORA]
    gd = xc[:, w3 + RW_DECAY_LORA + RW_A_LORA:C_COLS]
    w0, a0, k_k, k_a, r_k, ln_w, ln_b = (vec_ref[i:i + 1, :] for i in range(7))
    w_log = -jax.nn.softplus(-(w0 + _dot(_bf(jnp.tanh(wd)), w2_ref[...]))) - 0.5
    lw = -jnp.exp(w_log)
    a = _sigmoid(a0 + _dot(_bf(ad), a2_ref[...]))
    gate = _dot(_bf(_sigmoid(gd)), g2_ref[...])
    bd = bd_ref[...]
    kkr = k * k_k
    kk = kkr * lax.rsqrt(_segsum(kkr * kkr, bd) + L2_EPS)
    k2 = k * (1.0 + (a - 1.0) * k_a)
    bonus = _segsum(r * k2 * r_k, bd) * v
    rown = lax.broadcasted_iota(jnp.int32, (tt, RW_WIDTH), 0)
    if valid != (0, chunk):
        rc = rown & (chunk - 1)
        pad = jnp.where(rc >= valid[0], jnp.where(rc < valid[1], 0, 1), 1) == 1
        lw = jnp.where(pad, 0.0, lw)
        kk = jnp.where(pad, 0.0, kk)
        k2 = jnp.where(pad, 0.0, k2)
    gcum = _seg_cumsum(lw, rown & (chunk - 1), chunk)
    e_pos = jnp.exp(gcum)
    e_neg = jnp.exp(-gcum)
    ar_scr[:, 0:RW_WIDTH] = -kk * jnp.exp(gcum - lw)
    ar_scr[:, RW_WIDTH:2 * RW_WIDTH] = r * e_pos
    bk_scr[:, 0:RW_WIDTH] = kk * a * e_neg
    bk_scr[:, RW_WIDTH:2 * RW_WIDTH] = k2 * e_neg
    v_scr[...] = v
    pe_scr[...] = e_pos

    ci = lax.broadcasted_iota(jnp.int32, (2 * chunk, 2 * chunk), 0)
    cj = lax.broadcasted_iota(jnp.int32, (2 * chunk, 2 * chunk), 1)
    keep = jnp.where(ci < chunk, ci - 1, ci - chunk) >= (cj & (chunk - 1))

    nchunks = tt // chunk
    group = min(RW_GROUP, nchunks)

    def precompute(gi, carry):
        items = []
        for cc in range(group):
            cidx = gi * group + cc
            rows = pl.ds(pl.multiple_of(cidx * chunk, chunk), chunk)
            p_tail = pe_scr[pl.ds(pl.multiple_of((cidx + 1) * chunk - V7X_SUBLANES, V7X_SUBLANES), V7X_SUBLANES), :]
            for h in range(RW_HEADS):
                cols = slice(h * RW_HD, (h + 1) * RW_HD)
                cols2 = slice(RW_WIDTH + h * RW_HD, RW_WIDTH + (h + 1) * RW_HD)
                at, rt = ar_scr[rows, cols], ar_scr[rows, cols2]
                bt, kt = bk_scr[rows, cols], bk_scr[rows, cols2]
                bkb = _bf(jnp.concatenate([bt, kt], axis=0))
                m = jnp.where(keep, _dot_nt(_bf(jnp.concatenate([at, rt], axis=0)), bkb), 0.0)
                items.append(dict(rows=rows, cols=cols, idx=cidx * RW_HEADS + h, at=at, rt=rt, bt=_bf(bt), bkb=bkb,
                                  vh=v_scr[rows, cols], p=m[0:chunk, 0:chunk], aak=_bf(m[0:chunk, chunk:2 * chunk]),
                                  mlow=_bf(m[chunk:2 * chunk, :]), p_end=p_tail[V7X_SUBLANES - 1:V7X_SUBLANES, cols]))
        for it in items:
            it["sol"] = jnp.concatenate([it["at"], _dot(it["aak"], _bf(it["vh"]))], axis=1)
        k = 1
        while k < chunk:
            for it in items:
                it["sol"] = it["sol"] + _dot(_bf(it["p"]), _bf(it["sol"]))
            if 2 * k < chunk:
                for it in items:
                    pb = _bf(it["p"])
                    it["p"] = _dot(pb, pb)
            k *= 2
        for it in items:
            wm = _bf(it["sol"][:, 0:RW_HD])
            uv = _bf(jnp.concatenate([it["sol"][:, RW_HD:2 * RW_HD], it["vh"]], axis=0))
            rp_scr[it["rows"], it["cols"]] = it["rt"] + _dot(it["mlow"][:, 0:chunk], wm)
            oc_scr[it["rows"], it["cols"]] = _dot(it["mlow"], uv)
            m_scr[it["idx"]] = _dot_tn(wm, it["bt"]) * it["p_end"]
            n_scr[it["idx"]] = _dot_tn(uv, it["bkb"]) * it["p_end"]
        return carry

    lax.fori_loop(0, nchunks // group, precompute, 0)

    def recur(c, carry):
        rows = pl.ds(pl.multiple_of(c * chunk, chunk), chunk)
        p_tail = pe_scr[pl.ds(pl.multiple_of((c + 1) * chunk - V7X_SUBLANES, V7X_SUBLANES), V7X_SUBLANES), :]
        states = [s_scr[h] if nstate == 1 else s0_ref[c, h] for h in range(RW_HEADS)]
        sb = [_bf(s) for s in states]
        outs = [_dot_nt(_bf(rp_scr[rows, h * RW_HD:(h + 1) * RW_HD]), sb[h]) for h in range(RW_HEADS)]
        upd = [_dot(sb[h], _bf(m_scr[c * RW_HEADS + h])) for h in range(RW_HEADS)]
        for h in range(RW_HEADS):
            cols = slice(h * RW_HD, (h + 1) * RW_HD)
            oc_scr[rows, cols] = oc_scr[rows, cols] + outs[h]
            new = states[h] * p_tail[V7X_SUBLANES - 1:V7X_SUBLANES, cols] + upd[h] + n_scr[c * RW_HEADS + h]
            if nstate == 1:
                s_scr[h] = new
            else:
                sout_ref[c, h] = new
        return carry

    lax.fori_loop(0, nchunks, recur, 0)

    o = oc_scr[...]
    mean = _segsum(o, bd) * (1.0 / RW_HD)
    d = o - mean
    var = _segsum(d * d, bd) * (1.0 / RW_HD)
    o = d * lax.rsqrt(var + RW_GN_EPS) * ln_w + ln_b
    o_ref[...] = (o + bonus) * gate

    if nstate == 1:
        @pl.when(t == pl.num_programs(1) - 1)
        def _():
            sout_ref[0] = s_scr[...]


def _rw_mixer(pc, shift0, s0, lw, *, nseq, tt, chunk, valid, nstate):
    rows = pc.shape[0]
    nt = rows // (nseq * tt)
    nmat = (tt // chunk) * RW_HEADS
    vmem = (2 * (tt * (C_COLS + RW_WIDTH) * 4 + RW_WIDTH * RW_WIDTH * 2 + 4 * RW_WIDTH * V7X_LANES * 2)
            + (tt + 8) * C_COLS * 4 + 8 * tt * RW_WIDTH * 4 + (2 * nmat + (6 + 4 * nstate) * RW_HEADS) * RW_HD * V7X_LANES * 4
            + 14 * tt * RW_WIDTH * 4)
    const = lambda shape: pl.BlockSpec(shape, lambda s, t: (0,) * len(shape))
    return pl.pallas_call(
        functools.partial(_rw_kernel, tt=tt, chunk=chunk, valid=valid, nstate=nstate),
        out_shape=(jax.ShapeDtypeStruct((rows, RW_WIDTH), f32),
                   jax.ShapeDtypeStruct((nseq * nstate, RW_HEADS, RW_HD, RW_HD), f32)),
        grid=(nseq, nt),
        in_specs=[pl.BlockSpec((tt, C_COLS), lambda s, t: (s * nt + t, 0)),
                  pl.BlockSpec((1, 1, C_COLS), lambda s, t: (s, 0, 0)),
                  pl.BlockSpec((nstate, RW_HEADS, RW_HD, RW_HD), lambda s, t: (s, 0, 0, 0)),
                  const((1, C_COLS)), const((V7X_SUBLANES, RW_WIDTH)),
                  const((RW_DECAY_LORA, RW_WIDTH)), const((RW_A_LORA, RW_WIDTH)), const((RW_GATE_LORA, RW_WIDTH)),
                  const((RW_WIDTH, RW_WIDTH))],
        out_specs=(pl.BlockSpec((tt, RW_WIDTH), lambda s, t: (s * nt + t, 0)),
                   pl.BlockSpec((nstate, RW_HEADS, RW_HD, RW_HD), lambda s, t: (s, 0, 0, 0))),
        scratch_shapes=[pltpu.VMEM((tt + V7X_SUBLANES, C_COLS), f32),
                        pltpu.VMEM((tt, 2 * RW_WIDTH), f32), pltpu.VMEM((tt, 2 * RW_WIDTH), f32),
                        pltpu.VMEM((tt, RW_WIDTH), f32), pltpu.VMEM((tt, RW_WIDTH), f32),
                        pltpu.VMEM((tt, RW_WIDTH), f32), pltpu.VMEM((tt, RW_WIDTH), f32),
                        pltpu.VMEM((nmat, RW_HD, RW_HD), f32), pltpu.VMEM((nmat, RW_HD, RW_HD), f32),
                        pltpu.VMEM((RW_HEADS, RW_HD, RW_HD), f32)],
        compiler_params=_cparams(("parallel", "arbitrary"), vmem),
        name="rwkv7",
    )(pc, shift0, s0, lw["rw_mu"], lw["rw_vec"], lw["rw_w2"], lw["rw_a2"], lw["rw_g2"], lw["bd64"])


def _merge_kernel(h_ref, o0_ref, o1_ref, o2_ref, l0_ref, l1_ref, l2_ref, ob_ref, oc_ref, pg_ref,
                  wa_ref, wb_ref, wc_ref, wo_ref, out_ref):
    l0, l1, l2 = l0_ref[...], l1_ref[...], l2_ref[...]
    mx = jnp.maximum(jnp.maximum(l0, l1), l2)
    e0, e1, e2 = jnp.exp(l0 - mx), jnp.exp(l1 - mx), jnp.exp(l2 - mx)
    oa = (e0 * o0_ref[...] + e1 * o1_ref[...] + e2 * o2_ref[...]) / (e0 + e1 + e2)
    d = D_MODEL
    merged = (_sigmoid(pg_ref[:, 0:d]) * _dot(_bf(oa), wa_ref[...])
              + _sigmoid(pg_ref[:, d:2 * d]) * _dot(_bf(ob_ref[...]), wb_ref[...])
              + _sigmoid(pg_ref[:, 2 * d:3 * d]) * _dot(_bf(oc_ref[...]), wc_ref[...]))
    out_ref[...] = h_ref[...] + _dot(_bf(merged), wo_ref[...])


def _merge_sample_kernel(h_ref, oa_ref, ob_ref, oc_ref, pg_ref, wa_ref, wb_ref, wc_ref, wo_ref, out_ref):
    d = D_MODEL
    merged = (_sigmoid(pg_ref[:, 0:d]) * _dot(_bf(oa_ref[...]), wa_ref[...])
              + _sigmoid(pg_ref[:, d:2 * d]) * _dot(_bf(ob_ref[...]), wb_ref[...])
              + _sigmoid(pg_ref[:, 2 * d:3 * d]) * _dot(_bf(oc_ref[...]), wc_ref[...]))
    out_ref[...] = h_ref[...] + _dot(_bf(merged), wo_ref[...])


def _merge(h, oas, lses, ob, oc, pg, lw):
    rows, d = h.shape
    tr = min(ROW_TILE, rows)
    row = lambda n: pl.BlockSpec((tr, n), lambda i: (i, 0))
    const = lambda a: pl.BlockSpec(a.shape, lambda i: (0, 0))
    ws = (lw["w_br_a"], lw["w_br_b"], lw["w_br_c"], lw["w_out"])
    if lses is None:
        kern, acts = _merge_sample_kernel, (oas,)
    else:
        kern, acts = _merge_kernel, (*oas, *lses)
    acts = (h, *acts, ob, oc, pg)
    vmem = 2 * sum(tr * a.shape[1] * 4 for a in acts) + 2 * sum(w.size * 2 for w in ws) + 8 * tr * d * 4
    return pl.pallas_call(
        kern,
        out_shape=jax.ShapeDtypeStruct((rows, d), f32),
        grid=(rows // tr,),
        in_specs=[row(a.shape[1]) for a in acts] + [const(w) for w in ws],
        out_specs=row(d),
        compiler_params=_cparams(("parallel",), vmem),
        name="merge",
    )(*acts, *ws)


def _peer_candidate_groups(tv_scr):
    k = PEER_TOPK
    s = V7X_SUBLANES
    a_lo = tv_scr[0:s, :]
    a_hi = tv_scr[s:k, :]
    b = [tv_scr[k + j:k + j + 1, :] for j in range(s)]
    b_hi = tv_scr[k + s:2 * k, :]
    row = lax.broadcasted_iota(jnp.int32, a_lo.shape, 0)
    groups = [a_lo + b[0], a_hi + b[0], a_lo + b[1]]
    for j in range(2, s):
        groups.append(jnp.where(row < k // (j + 1), a_lo + b[j], NEG_INF))
    groups.append(tv_scr[0:1, :] + b_hi)
    return groups


def _peer_kernel(h_ref, nw_ref, wq_ref, wql_ref, sk_ref, skl_ref, u_ref, vt_ref, fw_ref, out_ref,
                 xn_scr, n1_scr, e1_scr, r2_scr, e2_scr, sc_scr, rk_scr, tv_scr, acc_scr, *, tokens, eblk, final_norm):
    e = pl.program_id(1)
    k = PEER_TOPK
    half = PEER_QDIM // 2
    unranked = float(k + 1)

    @pl.when(e == 0)
    def _():
        x = h_ref[...]
        xf = x * lax.rsqrt(jnp.mean(x * x, axis=-1, keepdims=True) + RMS_EPS) * nw_ref[...]
        xn = _bf(xf)
        xn_scr[...] = xn
        xl = _bf(xf - xn.astype(f32))
        qf = _dot(xn, wq_ref[...]) + (_dot(xn, wql_ref[...]) + _dot(xl, wq_ref[...]))
        q = _bf(qf)
        ql = _bf(qf - q.astype(f32))
        lane_tiles = [slice(lt * V7X_LANES, (lt + 1) * V7X_LANES) for lt in range(tokens // V7X_LANES)]
        for h in range(PEER_HEADS):
            for p in range(2):
                hp = 2 * h + p
                qcols = slice(hp * half, (hp + 1) * half)
                s = _dot_nt(sk_ref[hp], q[:, qcols]) + (_dot_nt(sk_ref[hp], ql[:, qcols])
                                                        + _dot_nt(skl_ref[hp], q[:, qcols]))
                if p == 0:
                    e1_scr[h] = s
                else:
                    sc_scr[...] = s
                for lanes in lane_tiles:
                    cur = s[:, lanes]
                    for i in range(k):
                        m = jnp.max(cur, axis=0, keepdims=True)
                        tv_scr[p * k + i:p * k + i + 1, lanes] = m
                        cur = jnp.where(cur >= m, -RANK_CODE * (1.0 + (i + 1) / 64.0), cur)
                    rank = jnp.where(cur <= -RANK_CODE, (cur * (-1.0 / RANK_CODE) - 1.0) * 64.0, unranked)
                    if p == 0:
                        rk_scr[:, lanes] = rank
                    else:
                        r2_scr[h, :, lanes] = _bf(rank)
            cur = _peer_candidate_groups(tv_scr)
            thr = None
            for i in range(k):
                m = cur[0]
                for g in cur[1:]:
                    m = jnp.maximum(m, g)
                thr = jnp.max(m, axis=0, keepdims=True)
                cur = [jnp.where(g >= thr, NEG_INF, g) for g in cur]
            top1 = tv_scr[0:1, :]
            top2 = tv_scr[k:k + 1, :]
            tops1 = tv_scr[0:k, :]
            cnt = jnp.zeros(tops1.shape, f32)
            z = jnp.zeros_like(thr)
            for j in range(k):
                sums = tops1 + tv_scr[k + j:k + j + 1, :]
                ok = sums >= thr
                cnt = cnt + jnp.where(ok, 1.0, 0.0)
                z = z + jnp.sum(jnp.where(ok, jnp.exp(sums - (top1 + top2)), 0.0), axis=0, keepdims=True)
            scale2 = 0.5 / z
            for lanes in lane_tiles:
                rank1 = rk_scr[:, lanes]
                n1 = jnp.zeros(rank1.shape, f32)
                for i in range(k):
                    n1 = jnp.where(rank1 == float(i + 1), cnt[i:i + 1, lanes], n1)
                n1_scr[h, :, lanes] = n1
                e1_scr[h, :, lanes] = jnp.exp(e1_scr[h, :, lanes] - top1[:, lanes])
                e2_scr[h, :, lanes] = _bf(jnp.exp(sc_scr[:, lanes] - top2[:, lanes]) * scale2[:, lanes])
        acc_scr[...] = jnp.zeros_like(acc_scr)

    nb1 = eblk // PEER_KEYS
    nlt = tokens // V7X_LANES
    pack = 2 * V7X_SUBLANES
    xu = _dot_nt(u_ref[0], xn_scr[...])
    act2 = xu + xu * jnp.tanh(xu * (GELU_C1 + GELU_C3 * (xu * xu)))
    zero = jnp.zeros((), bf16)
    parts = []
    for ip in range(nb1 // 2):
        i1s = [e * nb1 + 2 * ip + j for j in range(2)]
        nrows = [[n1_scr[h, pl.ds(i1, 1), :] for h in range(PEER_HEADS)] for i1 in i1s]
        erows = [[e1_scr[h, pl.ds(i1, 1), :] for h in range(PEER_HEADS)] for i1 in i1s]
        tiles = [[None] * nlt for _ in range(2)]
        for lt in range(nlt):
            lanes = slice(lt * V7X_LANES, (lt + 1) * V7X_LANES)
            g = [None, None]
            for h in range(PEER_HEADS):
                r2t = r2_scr[h, :, lanes]
                e2t = e2_scr[h, :, lanes]
                for j in range(2):
                    nb = _bf(jnp.broadcast_to(nrows[j][h][:, lanes], (pack, V7X_LANES)))
                    eb = _bf(jnp.broadcast_to(erows[j][h][:, lanes], (pack, V7X_LANES)))
                    nb = jnp.concatenate([nb] * (PEER_KEYS // pack), axis=0)
                    eb = jnp.concatenate([eb] * (PEER_KEYS // pack), axis=0)
                    hit = jnp.where(r2t <= nb, e2t * eb, zero)
                    g[j] = hit if g[j] is None else g[j] + hit
            for j in range(2):
                tiles[j][lt] = g[j]
        for j in range(2):
            il = 2 * ip + j
            gate = tiles[j][0] if nlt == 1 else jnp.concatenate(tiles[j], axis=1)
            parts.append(gate * _bf(act2[il * PEER_KEYS:(il + 1) * PEER_KEYS, :]))
    acc_scr[...] += _dot(vt_ref[0], jnp.concatenate(parts, axis=0))

    @pl.when(e == pl.num_programs(1) - 1)
    def _():
        y = h_ref[...] + acc_scr[...].T
        if final_norm:
            y = y * lax.rsqrt(jnp.mean(y * y, axis=-1, keepdims=True) + RMS_EPS) * fw_ref[...]
        out_ref[...] = y


def _peer(h, lw, layer, final_w=None):
    rows, d = h.shape
    tokens = min(PEER_TOKENS, rows)
    assert tokens % V7X_LANES == 0 and rows % tokens == 0
    eblk = PEER_EBLK
    n_half = PEER_HEADS * 2
    vmem = (2 * (2 * tokens * d * 4 + 2 * d * d * 2 + 2 * eblk * d * 2)
            + tokens * d * 2 + 4 * PEER_HEADS * PEER_KEYS * tokens * 4 + d * tokens * 4
            + 5 * eblk * tokens * 4)
    return pl.pallas_call(
        functools.partial(_peer_kernel, tokens=tokens, eblk=eblk, final_norm=final_w is not None),
        out_shape=jax.ShapeDtypeStruct((rows, d), f32),
        grid=(rows // tokens, PEER_EXPERTS // eblk),
        in_specs=[pl.BlockSpec((tokens, d), lambda i, e: (i, 0)),
                  pl.BlockSpec((1, d), lambda i, e: (0, 0)),
                  pl.BlockSpec((d, PEER_HEADS * PEER_QDIM), lambda i, e: (0, 0)),
                  pl.BlockSpec((d, PEER_HEADS * PEER_QDIM), lambda i, e: (0, 0)),
                  pl.BlockSpec((n_half, PEER_KEYS, PEER_QDIM // 2), lambda i, e: (0, 0, 0)),
                  pl.BlockSpec((n_half, PEER_KEYS, PEER_QDIM // 2), lambda i, e: (0, 0, 0)),
                  pl.BlockSpec((1, eblk, d), lambda i, e: (layer, e, 0)),
                  pl.BlockSpec((1, d, eblk), lambda i, e: (layer, 0, e)),
                  pl.BlockSpec((1, d), lambda i, e: (0, 0))],
        out_specs=pl.BlockSpec((tokens, d), lambda i, e: (i, 0)),
        scratch_shapes=[pltpu.VMEM((tokens, d), bf16),
                        pltpu.VMEM((PEER_HEADS, PEER_KEYS, tokens), f32),
                        pltpu.VMEM((PEER_HEADS, PEER_KEYS, tokens), f32),
                        pltpu.VMEM((PEER_HEADS, PEER_KEYS, tokens), bf16),
                        pltpu.VMEM((PEER_HEADS, PEER_KEYS, tokens), bf16),
                        pltpu.VMEM((PEER_KEYS, tokens), f32), pltpu.VMEM((PEER_KEYS, tokens), f32),
                        pltpu.VMEM((2 * PEER_TOPK, tokens), f32),
                        pltpu.VMEM((d, tokens), f32)],
        compiler_params=_cparams(("parallel", "arbitrary"), vmem),
        name="peer",
    )(h, lw["norm_ffn"], lw["peer_wq"], lw["peer_wq_lo"], lw["peer_sk"], lw["peer_sk_lo"], lw["peer_u"], lw["peer_vt"],
      (lw["norm_ffn"] if final_w is None else final_w).reshape(1, d))


def _layer_weights(l, p):
    w_in = p["w_in"][l]
    o_b, o_c, o_g = A_COLS, A_COLS + B_COLS, A_COLS + B_COLS + C_COLS
    lane = jnp.arange(V7X_LANES)
    on_alpha = (lane >= DN_HEADS) & (lane < 2 * DN_HEADS)

    def alpha_lanes(vec):
        return jnp.where(on_alpha, jnp.pad(vec, (DN_HEADS, V7X_LANES - 2 * DN_HEADS)), 0.0)

    sk = p["peer_subkeys"][l].reshape(PEER_HEADS * 2, PEER_KEYS, PEER_QDIM // 2)
    rw_vec = jnp.stack([p["rw_w0"][l], p["rw_a0"][l], p["rw_k_k"][l], p["rw_k_a"][l],
                        p["rw_r_k"][l].reshape(RW_WIDTH), p["rw_ln_w"][l], p["rw_ln_b"][l],
                        jnp.zeros((RW_WIDTH,), f32)])
    return {
        "norm_mix": p["norm_mix"][l],
        "w_a": _bf(w_in[:, 0:A_COLS]),
        "w_b": _bf(w_in[:, o_b:o_b + B_MAIN]),
        "w_bg": _bf(jnp.pad(w_in[:, o_b + B_MAIN:o_c], ((0, 0), (0, V7X_LANES - 2 * DN_HEADS)))),
        "w_c": _bf(w_in[:, o_c:o_g]),
        "w_g": _bf(w_in[:, o_g:]),
        "dn_conv_w": p["dn_conv_w"][l],
        "dn_prm": jnp.zeros((V7X_SUBLANES, V7X_LANES), f32)
                  .at[0].set(alpha_lanes(p["dn_a_log"][l])).at[1].set(alpha_lanes(p["dn_dt_bias"][l])),
        "dn_norm_w": jnp.tile(p["dn_norm_w"][l], DN_HEADS).reshape(1, DN_WIDTH),
        "bd128": _block_diag_ones(DN_WIDTH, DN_HD),
        "bd64": _block_diag_ones(RW_WIDTH, RW_HD),
        "rw_mu": p["rw_mu"][l].reshape(1, C_COLS),
        "rw_vec": rw_vec,
        "rw_w2": _bf(p["rw_w2"][l]), "rw_a2": _bf(p["rw_a2"][l]), "rw_g2": _bf(p["rw_g2"][l]),
        "w_br_a": _bf(p["w_br_a"][l]), "w_br_b": _bf(p["w_br_b"][l]), "w_br_c": _bf(p["w_br_c"][l]),
        "w_out": _bf(p["w_out"][l]),
        "norm_ffn": p["norm_ffn"][l].reshape(1, D_MODEL),
        "peer_wq": _bf(p["peer_wq"][l]),
        "peer_wq_lo": _bf(p["peer_wq"][l] - _bf(p["peer_wq"][l]).astype(f32)),
        "peer_sk": _bf(sk),
        "peer_sk_lo": _bf(sk - _bf(sk).astype(f32)),
        "peer_u": p["peer_u_bf"],
        "peer_vt": p["peer_vt_bf"],
    }


def _project(h, lw):
    nm = lw["norm_mix"]
    return {s: _norm_matmul(h, nm, lw["w_" + s]) for s in ("a", "b", "bg", "c", "g")}


def _kv_rows(pa, gi, lo, hi):
    k = pa[lo:hi, ATT_WIDTH + gi * ATT_OUT:ATT_WIDTH + (gi + 1) * ATT_OUT]
    v = pa[lo:hi, 2 * ATT_WIDTH + gi * ATT_OUT:2 * ATT_WIDTH + (gi + 1) * ATT_OUT]
    return jnp.stack([k, v], axis=1).reshape(hi - lo, 2, ATT_HPG, ATT_HD)


def _prompt_layer(h, lw, layer, final_w):
    t = h.shape[0]
    assert t % ATT_TILE == 0 and t % SEQ_TILE == 0
    pr = _project(h, lw)
    groups = [_attn_prompt_group(pr["a"], t, gi) for gi in range(len(ATT_GROUPS))]
    ob, dn_s = _dn_mixer(pr["b"], pr["bg"], jnp.zeros((1, V7X_SUBLANES, B_QKV), f32),
                         jnp.zeros((1, DN_HEADS, DN_HD, DN_HD), f32), lw,
                         nseq=1, tt=SEQ_TILE, chunk=DN_CHUNK, valid=(0, DN_CHUNK), nstate=1)
    oc, rw_s = _rw_mixer(pr["c"], jnp.zeros((1, 1, C_COLS), f32),
                         jnp.zeros((1, RW_HEADS, RW_HD, RW_HD), f32), lw,
                         nseq=1, tt=SEQ_TILE, chunk=RW_CHUNK, valid=(0, RW_CHUNK), nstate=1)
    h = _merge(h, [g[0] for g in groups], [g[1] for g in groups], ob, oc, pr["g"], lw)
    h = _peer(h, lw, layer, final_w)
    kvs = [_kv_rows(pr["a"], gi, t - min(win, t), t)[None] for gi, (win, _) in enumerate(ATT_GROUPS)]
    conv = pr["b"][t - (DN_CONV - 1):t, 0:B_QKV][None]
    shift = pr["c"][t - 1:t]
    return h, kvs, dn_s, conv, rw_s, shift


def _sample_rows(hist, x, nseq, seq):
    n = x.shape[1]
    nh = hist.shape[1]
    rows = jnp.concatenate([hist, x.reshape(nseq, seq, n), jnp.zeros((nseq, SAMPLE_ROWS - nh - seq, n), f32)], axis=1)
    return rows.reshape(nseq * SAMPLE_ROWS, n)


def _new_rows(x, nseq, nh, seq):
    return x.reshape(nseq, SAMPLE_ROWS, x.shape[1])[:, nh:nh + seq].reshape(nseq * seq, x.shape[1])


def _sample_layer(h, lw, layer, final_w, caches, dn_state, dn_conv, rw_state, rw_shift, nseq, seq):
    pr = _project(h, lw)
    oa = _attn_sample(pr["a"], caches, layer, nseq, seq)
    nh_dn, nh_rw = DN_CONV - 1, 1
    assert nseq % SAMPLE_STEP == 0 and nh_dn + seq <= SAMPLE_ROWS
    steps = nseq // SAMPLE_STEP
    tt = SAMPLE_STEP * SAMPLE_ROWS
    conv_hist = jnp.pad(dn_conv, ((0, 0), (0, 0), (0, B_MAIN - B_QKV)))
    ob, dn_s = _dn_mixer(_sample_rows(conv_hist, pr["b"], nseq, seq),
                         _sample_rows(jnp.zeros((nseq, nh_dn, V7X_LANES), f32), pr["bg"], nseq, seq),
                         jnp.zeros((steps, V7X_SUBLANES, B_QKV), f32), dn_state, lw,
                         nseq=steps, tt=tt, chunk=SAMPLE_ROWS, valid=(nh_dn, nh_dn + seq), nstate=SAMPLE_STEP)
    oc, rw_s = _rw_mixer(_sample_rows(rw_shift[:, None, :], pr["c"], nseq, seq),
                         jnp.zeros((steps, 1, C_COLS), f32), rw_state, lw,
                         nseq=steps, tt=tt, chunk=SAMPLE_ROWS, valid=(nh_rw, nh_rw + seq), nstate=SAMPLE_STEP)
    h = _merge(h, oa, None, _new_rows(ob, nseq, nh_dn, seq), _new_rows(oc, nseq, nh_rw, seq), pr["g"], lw)
    h = _peer(h, lw, layer, final_w)
    kvs = [_kv_rows(pr["a"], gi, 0, nseq * seq).reshape(nseq, seq, 2, ATT_HPG, ATT_HD)
           for gi in range(len(ATT_GROUPS))]
    conv = pr["b"][:, 0:B_QKV].reshape(nseq, seq, B_QKV)[:, seq - (DN_CONV - 1):]
    shift = pr["c"].reshape(nseq, seq, C_COLS)[:, seq - 1]
    return h, kvs, dn_s, conv, rw_s, shift


def kernel(x_prompt, x_sample, cache_kv_w128, cache_kv_w512, cache_kv_w2048, state_dn, state_dn_conv, state_rw, state_rw_shift, norm_mix, w_in, dn_conv_w, dn_a_log, dn_dt_bias, dn_norm_w, rw_mu, rw_w0, rw_w2, rw_a0, rw_a2, rw_g2, rw_k_k, rw_k_a, rw_r_k, rw_ln_w, rw_ln_b, w_br_a, w_br_b, w_br_c, w_out, norm_ffn, peer_wq, peer_subkeys, peer_u, peer_v, norm_final):
    p = dict(norm_mix=norm_mix, w_in=w_in, dn_conv_w=dn_conv_w, dn_a_log=dn_a_log, dn_dt_bias=dn_dt_bias,
             dn_norm_w=dn_norm_w, rw_mu=rw_mu, rw_w0=rw_w0, rw_w2=rw_w2, rw_a0=rw_a0, rw_a2=rw_a2, rw_g2=rw_g2,
             rw_k_k=rw_k_k, rw_k_a=rw_k_a, rw_r_k=rw_r_k, rw_ln_w=rw_ln_w, rw_ln_b=rw_ln_b, w_br_a=w_br_a,
             w_br_b=w_br_b, w_br_c=w_br_c, w_out=w_out, norm_ffn=norm_ffn, peer_wq=peer_wq,
             peer_subkeys=peer_subkeys, peer_u=peer_u, peer_v=peer_v)
    depth = w_in.shape[0]
    p["peer_u_bf"] = _bf(peer_u)
    p["peer_vt_bf"] = jnp.swapaxes(_bf(peer_v), 1, 2)
    bp, t, d = x_prompt.shape
    nseq, seq, _ = x_sample.shape
    assert bp == 1 and d == D_MODEL and seq <= SAMPLE_ROWS // 2 and DN_CONV - 1 <= seq
    caches = [c.reshape(depth, nseq, c.shape[2], 2 * ATT_OUT) for c in (cache_kv_w128, cache_kv_w512, cache_kv_w2048)]
    hp = x_prompt.reshape(t, d)
    hs = x_sample.reshape(nseq * seq, d)
    outs_p, outs_s = [], []
    for l in range(depth):
        lw = _layer_weights(l, p)
        final_w = norm_final if l == depth - 1 else None
        hp, *st_p = _prompt_layer(hp, lw, l, final_w)
        hs, *st_s = _sample_layer(hs, lw, l, final_w, caches, state_dn[l], state_dn_conv[l], state_rw[l],
                                  state_rw_shift[l], nseq, seq)
        outs_p.append(st_p)
        outs_s.append(st_s)
    y_p = hp.reshape(bp, t, d)
    y_s = hs.reshape(nseq, seq, d)

    def stack(outs, pick):
        return jnp.stack([pick(o) for o in outs], axis=0)

    res = [y_p, y_s]
    for gi in range(len(ATT_GROUPS)):
        res.append(stack(outs_p, lambda o: o[0][gi]))
        res.append(stack(outs_s, lambda o: o[0][gi]))
    for idx in (1, 2, 3, 4):
        res.append(stack(outs_p, lambda o: o[idx]))
        res.append(stack(outs_s, lambda o: o[idx]))
    return tuple(res)
```

```python
import functools
import math

import jax
import jax.numpy as jnp
import numpy as np
from jax import lax
from jax.experimental import pallas as pl
from jax.experimental.pallas import tpu as pltpu

f32 = jnp.float32
bf16 = jnp.bfloat16

V7X_LANES = 128
V7X_SUBLANES = 8
V7X_VMEM_BYTES = 64 * 1024 * 1024
VMEM_CEILING = 56 * 1024 * 1024

D_MODEL = 1024
ATT_GROUPS = ((128, 1), (512, 4), (2048, 16))
ATT_HPG = 4
ATT_HD = 64
ATT_HEADS = ATT_HPG * len(ATT_GROUPS)
ATT_WIDTH = ATT_HEADS * ATT_HD
ATT_OUT = ATT_HPG * ATT_HD
ATT_STEPS = 128
DN_HEADS = 4
DN_HD = 128
DN_WIDTH = DN_HEADS * DN_HD
DN_CONV = 4
DN_CHUNK = 64
DN_GROUP = 8
RW_HEADS = 8
RW_HD = 64
RW_WIDTH = RW_HEADS * RW_HD
RW_DECAY_LORA = 64
RW_A_LORA = 64
RW_GATE_LORA = 128
RW_GN_EPS = 64e-5
RW_CHUNK = 64
RW_GROUP = 8
PEER_KEYS = 128
PEER_EXPERTS = PEER_KEYS * PEER_KEYS
PEER_HEADS = 8
PEER_QDIM = 128
PEER_TOPK = 16
RMS_EPS = 1e-6
L2_EPS = 1e-6

A_COLS = 3 * ATT_WIDTH
B_QKV = 3 * DN_WIDTH
B_MAIN = 4 * DN_WIDTH
B_COLS = B_MAIN + 2 * DN_HEADS
C_COLS = 3 * RW_WIDTH + RW_DECAY_LORA + RW_A_LORA + RW_GATE_LORA
G_COLS = 3 * D_MODEL

ROW_TILE = 512
PROJ_TILE = 1024
ATT_TILE = 2048
ATT_GROUP = 4
SEQ_TILE = 512
SAMPLE_ROWS = 8
SAMPLE_STEP = 8
PEER_TOKENS = 512
PEER_EBLK = 1024

NEG_INF = float("-inf")
RANK_CODE = 2.0 ** 100
GELU_C1 = math.sqrt(2.0 / math.pi)
GELU_C3 = 0.044715 * GELU_C1


def _alibi_slopes():
    h = np.arange(1, ATT_HEADS + 1, dtype=np.float32)
    return np.power(np.float32(2.0), -8.0 * h / ATT_HEADS).astype(np.float32)


def _cparams(semantics, vmem_bytes):
    return pltpu.CompilerParams(dimension_semantics=semantics,
                                vmem_limit_bytes=int(min(max(vmem_bytes, 16 * 1024 * 1024), VMEM_CEILING)))


def _dot(a, b):
    return jnp.dot(a, b, preferred_element_type=f32)


def _dot_nt(a, b):
    return lax.dot_general(a, b, (((1,), (1,)), ((), ())), preferred_element_type=f32)


def _dot_tn(a, b):
    return lax.dot_general(a, b, (((0,), (0,)), ((), ())), preferred_element_type=f32)


def _bf(x):
    return x.astype(bf16)


def _sigmoid(x):
    return 1.0 / (1.0 + jnp.exp(-x))


def _segsum(x, bd):
    hi = _bf(x)
    lo = _bf(x - hi.astype(f32))
    return _dot(hi, bd) + _dot(lo, bd)


def _seg_cumsum(x, rowc, seg):
    k = 1
    while k < seg:
        x = x + jnp.where(rowc >= k, pltpu.roll(x, k, 0), 0.0)
        k *= 2
    return x


def _block_diag_ones(width, seg):
    i = np.arange(width)
    return jnp.asarray((i[:, None] // seg) == (i[None, :] // seg), dtype=bf16)


def _norm_matmul_kernel(h_ref, nw_ref, w_ref, o_ref):
    x = h_ref[...]
    xn = x * lax.rsqrt(jnp.mean(x * x, axis=-1, keepdims=True) + RMS_EPS) * nw_ref[...]
    o_ref[...] = _dot(_bf(xn), w_ref[...])


def _norm_matmul(h, nw, w):
    rows, d = h.shape
    n = w.shape[1]
    tr = min(PROJ_TILE, rows)
    vmem = 2 * (tr * d * 4 + d * n * 2 + tr * n * 4) + 4 * tr * d * 4
    return pl.pallas_call(
        _norm_matmul_kernel,
        out_shape=jax.ShapeDtypeStruct((rows, n), f32),
        grid=(rows // tr,),
        in_specs=[pl.BlockSpec((tr, d), lambda i: (i, 0)),
                  pl.BlockSpec((1, d), lambda i: (0, 0)),
                  pl.BlockSpec((d, n), lambda i: (0, 0))],
        out_specs=pl.BlockSpec((tr, n), lambda i: (i, 0)),
        compiler_params=_cparams(("parallel",), vmem),
        name="norm_matmul",
    )(h, nw.reshape(1, d), w)


def _attn_prompt_kernel(q_ref, kc_ref, vc_ref, kp_ref, vp_ref, o_ref, lse_ref, kk_scr, vv_scr, *, dil, slopes):
    n = ATT_STEPS
    nblk = ATT_TILE // (dil * n)
    tile = pl.program_id(0)
    pair = pl.program_id(1)
    kk_scr[0:ATT_TILE, :] = kp_ref[...]
    kk_scr[ATT_TILE:2 * ATT_TILE, :] = kc_ref[...]
    vv_scr[0:ATT_TILE, :] = vp_ref[...]
    vv_scr[ATT_TILE:2 * ATT_TILE, :] = vc_ref[...]
    ii = lax.broadcasted_iota(jnp.int32, (n, 2 * n), 0)
    jj = lax.broadcasted_iota(jnp.int32, (n, 2 * n), 1)
    steps = n + ii - jj
    band = (steps >= 0) & (steps <= n)
    dist = (steps * dil).astype(f32)
    biases = []
    for hh in range(2):
        slope = jnp.where(pair == 0, slopes[hh], slopes[2 + hh])
        biases.append(jnp.where(band, -slope * dist, NEG_INF))

    def body(gidx, carry):
        items = []
        for cc in range(ATT_GROUP):
            c = gidx * ATT_GROUP + cc
            r = c % dil
            b = c // dil
            qs = r + dil * n * b
            rows_q = pl.ds(qs, n, stride=dil)
            rows_k = pl.ds(ATT_TILE + qs - dil * n, 2 * n, stride=dil)
            q = q_ref[rows_q, :] * (ATT_HD ** -0.5)
            k = kk_scr[rows_k, :]
            first_key = jnp.where(tile * nblk + b == 0, n, 0)
            ss = [_dot_nt(_bf(q[:, hh * ATT_HD:(hh + 1) * ATT_HD]), _bf(k[:, hh * ATT_HD:(hh + 1) * ATT_HD]))
                  for hh in range(2)]
            items.append(dict(rows_q=rows_q, rows_k=rows_k, first_key=first_key, s=ss))
        for it in items:
            it["p"], it["den"], it["lse"] = [], [], []
            for hh in range(2):
                s = jnp.where(jj >= it["first_key"], it["s"][hh] + biases[hh], NEG_INF)
                mx = jnp.max(s, axis=-1, keepdims=True)
                p = jnp.exp(s - mx)
                den = jnp.sum(p, axis=-1, keepdims=True)
                it["p"].append(_bf(p))
                it["den"].append(den)
                it["lse"].append(jnp.broadcast_to(mx + jnp.log(den), (n, ATT_HD)))
        for it in items:
            v = vv_scr[it["rows_k"], :]
            outs = [_dot(it["p"][hh], _bf(v[:, hh * ATT_HD:(hh + 1) * ATT_HD])) / it["den"][hh] for hh in range(2)]
            o_ref[it["rows_q"], :] = jnp.concatenate(outs, axis=1)
            lse_ref[it["rows_q"], :] = jnp.concatenate(it["lse"], axis=1)
        return carry

    lax.fori_loop(0, dil * nblk // ATT_GROUP, body, 0)


def _attn_prompt_group(pa, seq_len, gi):
    _, dil = ATT_GROUPS[gi]
    slopes = tuple(float(s) for s in _alibi_slopes()[gi * ATT_HPG:(gi + 1) * ATT_HPG])
    w = V7X_LANES
    qb, kb, vb = (gi * 2, 6 + gi * 2, 12 + gi * 2)
    blk = (ATT_TILE, w)
    cur = lambda base: pl.BlockSpec(blk, lambda i, j, base=base: (i, base + j))
    prev = lambda base: pl.BlockSpec(blk, lambda i, j, base=base: (jnp.maximum(i - 1, 0), base + j))
    vmem = 2 * 7 * ATT_TILE * w * 4 + 2 * 2 * ATT_TILE * w * 4 + 8 * 1024 * 1024
    return pl.pallas_call(
        functools.partial(_attn_prompt_kernel, dil=dil, slopes=slopes),
        out_shape=(jax.ShapeDtypeStruct((seq_len, ATT_OUT), f32), jax.ShapeDtypeStruct((seq_len, ATT_OUT), f32)),
        grid=(seq_len // ATT_TILE, 2),
        in_specs=[cur(qb), cur(kb), cur(vb), prev(kb), prev(vb)],
        out_specs=(pl.BlockSpec(blk, lambda i, j: (i, j)), pl.BlockSpec(blk, lambda i, j: (i, j))),
        scratch_shapes=[pltpu.VMEM((2 * ATT_TILE, w), f32), pltpu.VMEM((2 * ATT_TILE, w), f32)],
        compiler_params=_cparams(("parallel", "parallel"), vmem),
        name=f"attn_prompt_g{gi}",
    )(pa, pa, pa, pa, pa)


def _attn_sample_kernel(q_ref, k_ref, v_ref, c0_ref, c1_ref, c2_ref, o_ref, *, slopes, seq):
    rows = 2 * seq
    caches = (c0_ref, c1_ref, c2_ref)
    row = lax.broadcasted_iota(jnp.int32, (rows, 1), 0)
    own0 = row < seq
    qpos = row % seq
    npos = lax.broadcasted_iota(jnp.int32, (rows, rows), 1)
    q_all = q_ref[0] * (ATT_HD ** -0.5)
    k_all = k_ref[0]
    v_all = v_ref[0]
    outs, lses = [], []
    for gi, (win, dil) in enumerate(ATT_GROUPS):
        cpos = lax.broadcasted_iota(jnp.int32, (rows, win), 1)
        dist_c = win + qpos - cpos
        ok_c = (dist_c <= win) & ((dist_c & (dil - 1)) == 0)
        dist_n = row - npos
        ok_n = (dist_n >= 0) & (dist_n <= qpos) & ((dist_n & (dil - 1)) == 0)
        dcf = dist_c.astype(f32)
        dnf = dist_n.astype(f32)
        for h in range(ATT_HPG):
            slope = slopes[gi * ATT_HPG + h]
            cols = slice(gi * ATT_OUT + h * ATT_HD, gi * ATT_OUT + (h + 1) * ATT_HD)
            kcols = slice(h * ATT_HD, (h + 1) * ATT_HD)
            vcols = slice(ATT_OUT + h * ATT_HD, ATT_OUT + (h + 1) * ATT_HD)
            qh = _bf(q_all[:, cols])
            sn = jnp.where(ok_n, _dot_nt(qh, _bf(k_all[:, cols])) - slope * dnf, NEG_INF)
            scs = []
            for b in range(2):
                sc = _dot_nt(qh, _bf(caches[gi][0, b, :, kcols])) - slope * dcf
                scs.append(jnp.where(ok_c, sc, NEG_INF))
            sc = jnp.where(own0, scs[0], scs[1])
            mx = jnp.maximum(jnp.max(sc, axis=-1, keepdims=True), jnp.max(sn, axis=-1, keepdims=True))
            pc = jnp.exp(sc - mx)
            pn = jnp.exp(sn - mx)
            den = jnp.sum(pc, axis=-1, keepdims=True) + jnp.sum(pn, axis=-1, keepdims=True)
            pcb = _bf(pc)
            oc = jnp.where(own0, _dot(pcb, _bf(caches[gi][0, 0, :, vcols])), _dot(pcb, _bf(caches[gi][0, 1, :, vcols])))
            o = (oc + _dot(_bf(pn), _bf(v_all[:, cols]))) / den
            outs.append(o)
            lses.append(mx + jnp.log(den))
    merged = []
    for h in range(ATT_HPG):
        ls = [lses[gi * ATT_HPG + h] for gi in range(len(ATT_GROUPS))]
        mx = jnp.maximum(jnp.maximum(ls[0], ls[1]), ls[2])
        es = [jnp.exp(l - mx) for l in ls]
        den = es[0] + es[1] + es[2]
        acc = es[0] * outs[h] + es[1] * outs[ATT_HPG + h] + es[2] * outs[2 * ATT_HPG + h]
        merged.append(acc / den)
    o_ref[0] = jnp.concatenate(merged, axis=1)


def _attn_sample(pa, caches, layer, nseq, seq):
    rows = 2 * seq
    assert rows == SAMPLE_ROWS and nseq % 2 == 0
    slopes = tuple(float(s) for s in _alibi_slopes())
    pa3 = pa.reshape(nseq // 2, rows, A_COLS)
    in_specs = [pl.BlockSpec((1, rows, ATT_WIDTH), lambda i, c=c: (i, 0, c)) for c in range(3)]
    vmem = 0
    for (win, _), c in zip(ATT_GROUPS, caches):
        assert c.shape[2] == win, "cached window must hold exactly `window` rows"
        in_specs.append(pl.BlockSpec((1, 2, win, 2 * ATT_OUT), lambda i: (layer, i, 0, 0)))
        vmem += 2 * 2 * win * 2 * ATT_OUT * 4
    out = pl.pallas_call(
        functools.partial(_attn_sample_kernel, slopes=slopes, seq=seq),
        out_shape=jax.ShapeDtypeStruct((nseq // 2, rows, ATT_OUT), f32),
        grid=(nseq // 2,),
        in_specs=in_specs,
        out_specs=pl.BlockSpec((1, rows, ATT_OUT), lambda i: (i, 0, 0)),
        compiler_params=_cparams(("parallel",), vmem + 16 * 1024 * 1024),
        name="attn_sample",
    )(pa3, pa3, pa3, *caches)
    return out.reshape(nseq * seq, ATT_OUT)


def _dn_kernel(pb_ref, pbg_ref, conv0_ref, s0_ref, cw_ref, prm_ref, nw_ref, bd_ref,
               o_ref, sout_ref,
               xs_scr, q_scr, k_scr, v_scr, g_scr, b_scr, oc_scr, qp_scr, m_scr, n_scr, s_scr,
               *, tt, chunk, valid, nstate):
    t = pl.program_id(1)

    @pl.when(t == 0)
    def _():
        xs_scr[0:V7X_SUBLANES, :] = conv0_ref[0]
        if nstate == 1:
            s_scr[...] = s0_ref[0]

    x = pb_ref[:, 0:B_QKV]
    xs_scr[pl.ds(V7X_SUBLANES, tt), :] = x
    cw = cw_ref[...]
    y = x * cw[3:4, :]
    for j in range(DN_CONV - 1):
        y = y + xs_scr[pl.ds(V7X_SUBLANES - (DN_CONV - 1) + j, tt), :] * cw[j:j + 1, :]
    xs_scr[0:V7X_SUBLANES, :] = xs_scr[pl.ds(tt, V7X_SUBLANES), :]
    y = y * _sigmoid(y)
    bd = bd_ref[...]
    q = y[:, 0:DN_WIDTH]
    k = y[:, DN_WIDTH:2 * DN_WIDTH]
    q_scr[...] = q * lax.rsqrt(_segsum(q * q, bd) + L2_EPS) * (DN_HD ** -0.5)
    k_scr[...] = k * lax.rsqrt(_segsum(k * k, bd) + L2_EPS)
    v_scr[...] = y[:, 2 * DN_WIDTH:3 * DN_WIDTH]
    pg = pbg_ref[...]
    beta = _sigmoid(pg)
    g = -jnp.exp(prm_ref[0:1, :]) * jax.nn.softplus(pg + prm_ref[1:2, :])
    rown = lax.broadcasted_iota(jnp.int32, (tt, V7X_LANES), 0)
    if valid != (0, chunk):
        rc = rown & (chunk - 1)
        beta = jnp.where(rc >= valid[0], jnp.where(rc < valid[1], beta, 0.0), 0.0)
        g = jnp.where(rc >= valid[0], jnp.where(rc < valid[1], g, 0.0), 0.0)
    g_scr[...] = _seg_cumsum(g, rown & (chunk - 1), chunk)
    b_scr[...] = beta

    ci = lax.broadcasted_iota(jnp.int32, (chunk, chunk), 0)
    cj = lax.broadcasted_iota(jnp.int32, (chunk, chunk), 1)
    incl = ci >= cj
    strict = ci > cj

    nchunks = tt // chunk
    group = min(DN_GROUP, nchunks)

    def precompute(gi, carry):
        items = []
        for cc in range(group):
            cidx = gi * group + cc
            rows = pl.ds(pl.multiple_of(cidx * chunk, chunk), chunk)
            gc = g_scr[rows, :]
            bc = b_scr[rows, :]
            gct = gc.T
            for h in range(DN_HEADS):
                cols = slice(h * DN_HD, (h + 1) * DN_HD)
                qh, kh, vh = q_scr[rows, cols], k_scr[rows, cols], v_scr[rows, cols]
                gcol = gc[:, DN_HEADS + h:DN_HEADS + h + 1]
                grow = gct[DN_HEADS + h:DN_HEADS + h + 1, :]
                bcol = bc[:, h:h + 1]
                glast = gc[chunk - 1:chunk, DN_HEADS + h:DN_HEADS + h + 1]
                decay = jnp.exp(jnp.where(incl, gcol - grow, NEG_INF))
                kb = kh * bcol
                eg = jnp.exp(gcol)
                both = _dot_nt(_bf(jnp.concatenate([kb, qh], axis=0)), _bf(kh))
                items.append(dict(
                    rows=rows, cols=cols, idx=cidx * DN_HEADS + h,
                    p=jnp.where(strict, -both[0:chunk] * decay, 0.0),
                    attn=jnp.where(incl, both[chunk:2 * chunk] * decay, 0.0),
                    sol=jnp.concatenate([vh * bcol, kb * eg], axis=1),
                    qeg=qh * eg, kdec=_bf(kh * jnp.exp(glast - gcol))))
        k = 1
        while k < chunk:
            for it in items:
                it["sol"] = it["sol"] + _dot(_bf(it["p"]), _bf(it["sol"]))
            if 2 * k < chunk:
                for it in items:
                    pb = _bf(it["p"])
                    it["p"] = _dot(pb, pb)
            k *= 2
        for it in items:
            u = _bf(it["sol"][:, 0:DN_HD])
            w = _bf(it["sol"][:, DN_HD:2 * DN_HD])
            attn = _bf(it["attn"])
            qp_scr[it["rows"], it["cols"]] = it["qeg"] - _dot(attn, w)
            oc_scr[it["rows"], it["cols"]] = _dot(attn, u)
            m_scr[it["idx"]] = _dot_tn(it["kdec"], w)
            n_scr[it["idx"]] = _dot_tn(it["kdec"], u)
        return carry

    lax.fori_loop(0, nchunks // group, precompute, 0)

    def recur(c, carry):
        rows = pl.ds(pl.multiple_of(c * chunk, chunk), chunk)
        g_tail = g_scr[pl.ds(pl.multiple_of((c + 1) * chunk - V7X_SUBLANES, V7X_SUBLANES), V7X_SUBLANES), :]
        states = [s_scr[h] if nstate == 1 else s0_ref[c, h] for h in range(DN_HEADS)]
        sb = [_bf(s) for s in states]
        outs = [_dot(_bf(qp_scr[rows, h * DN_HD:(h + 1) * DN_HD]), sb[h]) for h in range(DN_HEADS)]
        upd = [_dot(_bf(m_scr[c * DN_HEADS + h]), sb[h]) for h in range(DN_HEADS)]
        for h in range(DN_HEADS):
            cols = slice(h * DN_HD, (h + 1) * DN_HD)
            oc_scr[rows, cols] = oc_scr[rows, cols] + outs[h]
            g_end = jnp.exp(g_tail[V7X_SUBLANES - 1:V7X_SUBLANES, DN_HEADS + h:DN_HEADS + h + 1])
            new = states[h] * g_end - upd[h] + n_scr[c * DN_HEADS + h]
            if nstate == 1:
                s_scr[h] = new
            else:
                sout_ref[c, h] = new
        return carry

    lax.fori_loop(0, nchunks, recur, 0)

    o = oc_scr[...]
    z = pb_ref[:, B_QKV:B_MAIN]
    o = o * lax.rsqrt(_segsum(o * o, bd) * (1.0 / DN_HD) + RMS_EPS) * nw_ref[...]
    o_ref[...] = o * (z * _sigmoid(z))

    if nstate == 1:
        @pl.when(t == pl.num_programs(1) - 1)
        def _():
            sout_ref[0] = s_scr[...]


def _dn_mixer(pb, pbg, conv0, s0, lw, *, nseq, tt, chunk, valid, nstate):
    rows = pb.shape[0]
    nt = rows // (nseq * tt)
    nmat = (tt // chunk) * DN_HEADS
    vmem = (2 * (tt * (B_MAIN + V7X_LANES + DN_WIDTH) * 4 + DN_WIDTH * DN_WIDTH * 2)
            + (tt + 8) * B_QKV * 4 + 4 * tt * DN_WIDTH * 4 + 2 * tt * V7X_LANES * 4
            + tt * DN_WIDTH * 4 + (2 * nmat + (6 + 4 * nstate) * DN_HEADS) * DN_HD * DN_HD * 4 + 8 * tt * B_QKV * 4)
    return pl.pallas_call(
        functools.partial(_dn_kernel, tt=tt, chunk=chunk, valid=valid, nstate=nstate),
        out_shape=(jax.ShapeDtypeStruct((rows, DN_WIDTH), f32),
                   jax.ShapeDtypeStruct((nseq * nstate, DN_HEADS, DN_HD, DN_HD), f32)),
        grid=(nseq, nt),
        in_specs=[pl.BlockSpec((tt, B_MAIN), lambda s, t: (s * nt + t, 0)),
                  pl.BlockSpec((tt, V7X_LANES), lambda s, t: (s * nt + t, 0)),
                  pl.BlockSpec((1, V7X_SUBLANES, B_QKV), lambda s, t: (s, 0, 0)),
                  pl.BlockSpec((nstate, DN_HEADS, DN_HD, DN_HD), lambda s, t: (s, 0, 0, 0)),
                  pl.BlockSpec((DN_CONV, B_QKV), lambda s, t: (0, 0)),
                  pl.BlockSpec((V7X_SUBLANES, V7X_LANES), lambda s, t: (0, 0)),
                  pl.BlockSpec((1, DN_WIDTH), lambda s, t: (0, 0)),
                  pl.BlockSpec((DN_WIDTH, DN_WIDTH), lambda s, t: (0, 0))],
        out_specs=(pl.BlockSpec((tt, DN_WIDTH), lambda s, t: (s * nt + t, 0)),
                   pl.BlockSpec((nstate, DN_HEADS, DN_HD, DN_HD), lambda s, t: (s, 0, 0, 0))),
        scratch_shapes=[pltpu.VMEM((tt + V7X_SUBLANES, B_QKV), f32),
                        pltpu.VMEM((tt, DN_WIDTH), f32), pltpu.VMEM((tt, DN_WIDTH), f32),
                        pltpu.VMEM((tt, DN_WIDTH), f32),
                        pltpu.VMEM((tt, V7X_LANES), f32), pltpu.VMEM((tt, V7X_LANES), f32),
                        pltpu.VMEM((tt, DN_WIDTH), f32), pltpu.VMEM((tt, DN_WIDTH), f32),
                        pltpu.VMEM((nmat, DN_HD, DN_HD), f32), pltpu.VMEM((nmat, DN_HD, DN_HD), f32),
                        pltpu.VMEM((DN_HEADS, DN_HD, DN_HD), f32)],
        compiler_params=_cparams(("parallel", "arbitrary"), vmem),
        name="deltanet",
    )(pb, pbg, conv0, s0, lw["dn_conv_w"], lw["dn_prm"], lw["dn_norm_w"], lw["bd128"])


def _rw_kernel(pc_ref, shift0_ref, s0_ref, mu_ref, vec_ref, w2_ref, a2_ref, g2_ref, bd_ref,
               o_ref, sout_ref,
               xs_scr, ar_scr, bk_scr, v_scr, pe_scr, oc_scr, rp_scr, m_scr, n_scr, s_scr, *, tt, chunk, valid, nstate):
    t = pl.program_id(1)

    @pl.when(t == 0)
    def _():
        xs_scr[V7X_SUBLANES - 1:V7X_SUBLANES, :] = shift0_ref[0]
        if nstate == 1:
            s_scr[...] = s0_ref[0]

    pc = pc_ref[...]
    xs_scr[pl.ds(V7X_SUBLANES, tt), :] = pc
    prev = xs_scr[pl.ds(V7X_SUBLANES - 1, tt), :]
    xs_scr[0:V7X_SUBLANES, :] = xs_scr[pl.ds(tt, V7X_SUBLANES), :]
    xc = pc + (prev - pc) * mu_ref[...]
    w3 = 3 * RW_WIDTH
    r = xc[:, 0:RW_WIDTH]
    k = xc[:, RW_WIDTH:2 * RW_WIDTH]
    v = xc[:, 2 * RW_WIDTH:w3]
    wd = xc[:, w3:w3 + RW_DECAY_LORA]
    ad = xc[:, w3 + RW_DECAY_LORA:w3 + RW_DECAY_LORA + RW_A_LORA]
    gd = xc[:, w3 + RW_DECAY_LORA + RW_A_LORA:C_COLS]
    w0, a0, k_k, k_a, r_k, ln_w, ln_b = (vec_ref[i:i + 1, :] for i in range(7))
    w_log = -jax.nn.softplus(-(w0 + _dot(_bf(jnp.tanh(wd)), w2_ref[...]))) - 0.5
    lw = -jnp.exp(w_log)
    a = _sigmoid(a0 + _dot(_bf(ad), a2_ref[...]))
    gate = _dot(_bf(_sigmoid(gd)), g2_ref[...])
    bd = bd_ref[...]
    kkr = k * k_k
    kk = kkr * lax.rsqrt(_segsum(kkr * kkr, bd) + L2_EPS)
    k2 = k * (1.0 + (a - 1.0) * k_a)
    bonus = _segsum(r * k2 * r_k, bd) * v
    rown = lax.broadcasted_iota(jnp.int32, (tt, RW_WIDTH), 0)
    if valid != (0, chunk):
        rc = rown & (chunk - 1)
        pad = jnp.where(rc >= valid[0], jnp.where(rc < valid[1], 0, 1), 1) == 1
        lw = jnp.where(pad, 0.0, lw)
        kk = jnp.where(pad, 0.0, kk)
        k2 = jnp.where(pad, 0.0, k2)
    gcum = _seg_cumsum(lw, rown & (chunk - 1), chunk)
    e_pos = jnp.exp(gcum)
    e_neg = jnp.exp(-gcum)
    ar_scr[:, 0:RW_WIDTH] = -kk * jnp.exp(gcum - lw)
    ar_scr[:, RW_WIDTH:2 * RW_WIDTH] = r * e_pos
    bk_scr[:, 0:RW_WIDTH] = kk * a * e_neg
    bk_scr[:, RW_WIDTH:2 * RW_WIDTH] = k2 * e_neg
    v_scr[...] = v
    pe_scr[...] = e_pos

    ci = lax.broadcasted_iota(jnp.int32, (2 * chunk, 2 * chunk), 0)
    cj = lax.broadcasted_iota(jnp.int32, (2 * chunk, 2 * chunk), 1)
    keep = jnp.where(ci < chunk, ci - 1, ci - chunk) >= (cj & (chunk - 1))

    nchunks = tt // chunk
    group = min(RW_GROUP, nchunks)

    def precompute(gi, carry):
        items = []
        for cc in range(group):
            cidx = gi * group + cc
            rows = pl.ds(pl.multiple_of(cidx * chunk, chunk), chunk)
            p_tail = pe_scr[pl.ds(pl.multiple_of((cidx + 1) * chunk - V7X_SUBLANES, V7X_SUBLANES), V7X_SUBLANES), :]
            for h in range(RW_HEADS):
                cols = slice(h * RW_HD, (h + 1) * RW_HD)
                cols2 = slice(RW_WIDTH + h * RW_HD, RW_WIDTH + (h + 1) * RW_HD)
                at, rt = ar_scr[rows, cols], ar_scr[rows, cols2]
                bt, kt = bk_scr[rows, cols], bk_scr[rows, cols2]
                bkb = _bf(jnp.concatenate([bt, kt], axis=0))
                m = jnp.where(keep, _dot_nt(_bf(jnp.concatenate([at, rt], axis=0)), bkb), 0.0)
                items.append(dict(rows=rows, cols=cols, idx=cidx * RW_HEADS + h, at=at, rt=rt, bt=_bf(bt), bkb=bkb,
                                  vh=v_scr[rows, cols], p=m[0:chunk, 0:chunk], aak=_bf(m[0:chunk, chunk:2 * chunk]),
                                  mlow=_bf(m[chunk:2 * chunk, :]), p_end=p_tail[V7X_SUBLANES - 1:V7X_SUBLANES, cols]))
        for it in items:
            it["sol"] = jnp.concatenate([it["at"], _dot(it["aak"], _bf(it["vh"]))], axis=1)
        k = 1
        while k < chunk:
            for it in items:
                it["sol"] = it["sol"] + _dot(_bf(it["p"]), _bf(it["sol"]))
            if 2 * k < chunk:
                for it in items:
                    pb = _bf(it["p"])
                    it["p"] = _dot(pb, pb)
            k *= 2
        for it in items:
            wm = _bf(it["sol"][:, 0:RW_HD])
            uv = _bf(jnp.concatenate([it["sol"][:, RW_HD:2 * RW_HD], it["vh"]], axis=0))
            rp_scr[it["rows"], it["cols"]] = it["rt"] + _dot(it["mlow"][:, 0:chunk], wm)
            oc_scr[it["rows"], it["cols"]] = _dot(it["mlow"], uv)
            m_scr[it["idx"]] = _dot_tn(wm, it["bt"]) * it["p_end"]
            n_scr[it["idx"]] = _dot_tn(uv, it["bkb"]) * it["p_end"]
        return carry

    lax.fori_loop(0, nchunks // group, precompute, 0)

    def recur(c, carry):
        rows = pl.ds(pl.multiple_of(c * chunk, chunk), chunk)
        p_tail = pe_scr[pl.ds(pl.multiple_of((c + 1) * chunk - V7X_SUBLANES, V7X_SUBLANES), V7X_SUBLANES), :]
        states = [s_scr[h] if nstate == 1 else s0_ref[c, h] for h in range(RW_HEADS)]
        sb = [_bf(s) for s in states]
        outs = [_dot_nt(_bf(rp_scr[rows, h * RW_HD:(h + 1) * RW_HD]), sb[h]) for h in range(RW_HEADS)]
        upd = [_dot(sb[h], _bf(m_scr[c * RW_HEADS + h])) for h in range(RW_HEADS)]
        for h in range(RW_HEADS):
            cols = slice(h * RW_HD, (h + 1) * RW_HD)
            oc_scr[rows, cols] = oc_scr[rows, cols] + outs[h]
            new = states[h] * p_tail[V7X_SUBLANES - 1:V7X_SUBLANES, cols] + upd[h] + n_scr[c * RW_HEADS + h]
            if nstate == 1:
                s_scr[h] = new
            else:
                sout_ref[c, h] = new
        return carry

    lax.fori_loop(0, nchunks, recur, 0)

    o = oc_scr[...]
    mean = _segsum(o, bd) * (1.0 / RW_HD)
    d = o - mean
    var = _segsum(d * d, bd) * (1.0 / RW_HD)
    o = d * lax.rsqrt(var + RW_GN_EPS) * ln_w + ln_b
    o_ref[...] = (o + bonus) * gate

    if nstate == 1:
        @pl.when(t == pl.num_programs(1) - 1)
        def _():
            sout_ref[0] = s_scr[...]


def _rw_mixer(pc, shift0, s0, lw, *, nseq, tt, chunk, valid, nstate):
    rows = pc.shape[0]
    nt = rows // (nseq * tt)
    nmat = (tt // chunk) * RW_HEADS
    vmem = (2 * (tt * (C_COLS + RW_WIDTH) * 4 + RW_WIDTH * RW_WIDTH * 2 + 4 * RW_WIDTH * V7X_LANES * 2)
            + (tt + 8) * C_COLS * 4 + 8 * tt * RW_WIDTH * 4 + (2 * nmat + (6 + 4 * nstate) * RW_HEADS) * RW_HD * V7X_LANES * 4
            + 14 * tt * RW_WIDTH * 4)
    const = lambda shape: pl.BlockSpec(shape, lambda s, t: (0,) * len(shape))
    return pl.pallas_call(
        functools.partial(_rw_kernel, tt=tt, chunk=chunk, valid=valid, nstate=nstate),
        out_shape=(jax.ShapeDtypeStruct((rows, RW_WIDTH), f32),
                   jax.ShapeDtypeStruct((nseq * nstate, RW_HEADS, RW_HD, RW_HD), f32)),
        grid=(nseq, nt),
        in_specs=[pl.BlockSpec((tt, C_COLS), lambda s, t: (s * nt + t, 0)),
                  pl.BlockSpec((1, 1, C_COLS), lambda s, t: (s, 0, 0)),
                  pl.BlockSpec((nstate, RW_HEADS, RW_HD, RW_HD), lambda s, t: (s, 0, 0, 0)),
                  const((1, C_COLS)), const((V7X_SUBLANES, RW_WIDTH)),
                  const((RW_DECAY_LORA, RW_WIDTH)), const((RW_A_LORA, RW_WIDTH)), const((RW_GATE_LORA, RW_WIDTH)),
                  const((RW_WIDTH, RW_WIDTH))],
        out_specs=(pl.BlockSpec((tt, RW_WIDTH), lambda s, t: (s * nt + t, 0)),
                   pl.BlockSpec((nstate, RW_HEADS, RW_HD, RW_HD), lambda s, t: (s, 0, 0, 0))),
        scratch_shapes=[pltpu.VMEM((tt + V7X_SUBLANES, C_COLS), f32),
                        pltpu.VMEM((tt, 2 * RW_WIDTH), f32), pltpu.VMEM((tt, 2 * RW_WIDTH), f32),
                        pltpu.VMEM((tt, RW_WIDTH), f32), pltpu.VMEM((tt, RW_WIDTH), f32),
                        pltpu.VMEM((tt, RW_WIDTH), f32), pltpu.VMEM((tt, RW_WIDTH), f32),
                        pltpu.VMEM((nmat, RW_HD, RW_HD), f32), pltpu.VMEM((nmat, RW_HD, RW_HD), f32),
                        pltpu.VMEM((RW_HEADS, RW_HD, RW_HD), f32)],
        compiler_params=_cparams(("parallel", "arbitrary"), vmem),
        name="rwkv7",
    )(pc, shift0, s0, lw["rw_mu"], lw["rw_vec"], lw["rw_w2"], lw["rw_a2"], lw["rw_g2"], lw["bd64"])


def _merge_kernel(h_ref, o0_ref, o1_ref, o2_ref, l0_ref, l1_ref, l2_ref, ob_ref, oc_ref, pg_ref,
                  wa_ref, wb_ref, wc_ref, wo_ref, out_ref):
    l0, l1, l2 = l0_ref[...], l1_ref[...], l2_ref[...]
    mx = jnp.maximum(jnp.maximum(l0, l1), l2)
    e0, e1, e2 = jnp.exp(l0 - mx), jnp.exp(l1 - mx), jnp.exp(l2 - mx)
    oa = (e0 * o0_ref[...] + e1 * o1_ref[...] + e2 * o2_ref[...]) / (e0 + e1 + e2)
    d = D_MODEL
    merged = (_sigmoid(pg_ref[:, 0:d]) * _dot(_bf(oa), wa_ref[...])
              + _sigmoid(pg_ref[:, d:2 * d]) * _dot(_bf(ob_ref[...]), wb_ref[...])
              + _sigmoid(pg_ref[:, 2 * d:3 * d]) * _dot(_bf(oc_ref[...]), wc_ref[...]))
    out_ref[...] = h_ref[...] + _dot(_bf(merged), wo_ref[...])


def _merge_sample_kernel(h_ref, oa_ref, ob_ref, oc_ref, pg_ref, wa_ref, wb_ref, wc_ref, wo_ref, out_ref):
    d = D_MODEL
    merged = (_sigmoid(pg_ref[:, 0:d]) * _dot(_bf(oa_ref[...]), wa_ref[...])
              + _sigmoid(pg_ref[:, d:2 * d]) * _dot(_bf(ob_ref[...]), wb_ref[...])
              + _sigmoid(pg_ref[:, 2 * d:3 * d]) * _dot(_bf(oc_ref[...]), wc_ref[...]))
    out_ref[...] = h_ref[...] + _dot(_bf(merged), wo_ref[...])


def _merge(h, oas, lses, ob, oc, pg, lw):
    rows, d = h.shape
    tr = min(ROW_TILE, rows)
    row = lambda n: pl.BlockSpec((tr, n), lambda i: (i, 0))
    const = lambda a: pl.BlockSpec(a.shape, lambda i: (0, 0))
    ws = (lw["w_br_a"], lw["w_br_b"], lw["w_br_c"], lw["w_out"])
    if lses is None:
        kern, acts = _merge_sample_kernel, (oas,)
    else:
        kern, acts = _merge_kernel, (*oas, *lses)
    acts = (h, *acts, ob, oc, pg)
    vmem = 2 * sum(tr * a.shape[1] * 4 for a in acts) + 2 * sum(w.size * 2 for w in ws) + 8 * tr * d * 4
    return pl.pallas_call(
        kern,
        out_shape=jax.ShapeDtypeStruct((rows, d), f32),
        grid=(rows // tr,),
        in_specs=[row(a.shape[1]) for a in acts] + [const(w) for w in ws],
        out_specs=row(d),
        compiler_params=_cparams(("parallel",), vmem),
        name="merge",
    )(*acts, *ws)


def _peer_candidate_groups(tv_scr):
    k = PEER_TOPK
    s = V7X_SUBLANES
    a_lo = tv_scr[0:s, :]
    a_hi = tv_scr[s:k, :]
    b = [tv_scr[k + j:k + j + 1, :] for j in range(s)]
    b_hi = tv_scr[k + s:2 * k, :]
    row = lax.broadcasted_iota(jnp.int32, a_lo.shape, 0)
    groups = [a_lo + b[0], a_hi + b[0], a_lo + b[1]]
    for j in range(2, s):
        groups.append(jnp.where(row < k // (j + 1), a_lo + b[j], NEG_INF))
    groups.append(tv_scr[0:1, :] + b_hi)
    return groups


def _peer_kernel(h_ref, nw_ref, wq_ref, wql_ref, sk_ref, skl_ref, u_ref, vt_ref, fw_ref, out_ref,
                 xn_scr, n1_scr, e1_scr, r2_scr, e2_scr, sc_scr, rk_scr, tv_scr, acc_scr, *, tokens, eblk, final_norm):
    e = pl.program_id(1)
    k = PEER_TOPK
    half = PEER_QDIM // 2
    unranked = float(k + 1)

    @pl.when(e == 0)
    def _():
        x = h_ref[...]
        xf = x * lax.rsqrt(jnp.mean(x * x, axis=-1, keepdims=True) + RMS_EPS) * nw_ref[...]
        xn = _bf(xf)
        xn_scr[...] = xn
        xl = _bf(xf - xn.astype(f32))
        qf = _dot(xn, wq_ref[...]) + (_dot(xn, wql_ref[...]) + _dot(xl, wq_ref[...]))
        q = _bf(qf)
        ql = _bf(qf - q.astype(f32))
        lane_tiles = [slice(lt * V7X_LANES, (lt + 1) * V7X_LANES) for lt in range(tokens // V7X_LANES)]
        for h in range(PEER_HEADS):
            for p in range(2):
                hp = 2 * h + p
                qcols = slice(hp * half, (hp + 1) * half)
                s = _dot_nt(sk_ref[hp], q[:, qcols]) + (_dot_nt(sk_ref[hp], ql[:, qcols])
                                                        + _dot_nt(skl_ref[hp], q[:, qcols]))
                if p == 0:
                    e1_scr[h] = s
                else:
                    sc_scr[...] = s
                for lanes in lane_tiles:
                    cur = s[:, lanes]
                    for i in range(k):
                        m = jnp.max(cur, axis=0, keepdims=True)
                        tv_scr[p * k + i:p * k + i + 1, lanes] = m
                        cur = jnp.where(cur >= m, -RANK_CODE * (1.0 + (i + 1) / 64.0), cur)
                    rank = jnp.where(cur <= -RANK_CODE, (cur * (-1.0 / RANK_CODE) - 1.0) * 64.0, unranked)
                    if p == 0:
                        rk_scr[:, lanes] = rank
                    else:
                        r2_scr[h, :, lanes] = _bf(rank)
            cur = _peer_candidate_groups(tv_scr)
            thr = None
            for i in range(k):
                m = cur[0]
                for g in cur[1:]:
                    m = jnp.maximum(m, g)
                thr = jnp.max(m, axis=0, keepdims=True)
                cur = [jnp.where(g >= thr, NEG_INF, g) for g in cur]
            top1 = tv_scr[0:1, :]
            top2 = tv_scr[k:k + 1, :]
            tops1 = tv_scr[0:k, :]
            cnt = jnp.zeros(tops1.shape, f32)
            z = jnp.zeros_like(thr)
            for j in range(k):
                sums = tops1 + tv_scr[k + j:k + j + 1, :]
                ok = sums >= thr
                cnt = cnt + jnp.where(ok, 1.0, 0.0)
                z = z + jnp.sum(jnp.where(ok, jnp.exp(sums - (top1 + top2)), 0.0), axis=0, keepdims=True)
            scale2 = 0.5 / z
            for lanes in lane_tiles:
                rank1 = rk_scr[:, lanes]
                n1 = jnp.zeros(rank1.shape, f32)
                for i in range(k):
                    n1 = jnp.where(rank1 == float(i + 1), cnt[i:i + 1, lanes], n1)
                n1_scr[h, :, lanes] = n1
                e1_scr[h, :, lanes] = jnp.exp(e1_scr[h, :, lanes] - top1[:, lanes])
                e2_scr[h, :, lanes] = _bf(jnp.exp(sc_scr[:, lanes] - top2[:, lanes]) * scale2[:, lanes])
        acc_scr[...] = jnp.zeros_like(acc_scr)

    nb1 = eblk // PEER_KEYS
    nlt = tokens // V7X_LANES
    pack = 2 * V7X_SUBLANES
    xu = _dot_nt(u_ref[0], xn_scr[...])
    act2 = xu + xu * jnp.tanh(xu * (GELU_C1 + GELU_C3 * (xu * xu)))
    zero = jnp.zeros((), bf16)
    parts = []
    for ip in range(nb1 // 2):
        i1s = [e * nb1 + 2 * ip + j for j in range(2)]
        nrows = [[n1_scr[h, pl.ds(i1, 1), :] for h in range(PEER_HEADS)] for i1 in i1s]
        erows = [[e1_scr[h, pl.ds(i1, 1), :] for h in range(PEER_HEADS)] for i1 in i1s]
        tiles = [[None] * nlt for _ in range(2)]
        for lt in range(nlt):
            lanes = slice(lt * V7X_LANES, (lt + 1) * V7X_LANES)
            g = [None, None]
            for h in range(PEER_HEADS):
                r2t = r2_scr[h, :, lanes]
                e2t = e2_scr[h, :, lanes]
                for j in range(2):
                    nb = _bf(jnp.broadcast_to(nrows[j][h][:, lanes], (pack, V7X_LANES)))
                    eb = _bf(jnp.broadcast_to(erows[j][h][:, lanes], (pack, V7X_LANES)))
                    nb = jnp.concatenate([nb] * (PEER_KEYS // pack), axis=0)
                    eb = jnp.concatenate([eb] * (PEER_KEYS // pack), axis=0)
                    hit = jnp.where(r2t <= nb, e2t * eb, zero)
                    g[j] = hit if g[j] is None else g[j] + hit
            for j in range(2):
                tiles[j][lt] = g[j]
        for j in range(2):
            il = 2 * ip + j
            gate = tiles[j][0] if nlt == 1 else jnp.concatenate(tiles[j], axis=1)
            parts.append(gate * _bf(act2[il * PEER_KEYS:(il + 1) * PEER_KEYS, :]))
    acc_scr[...] += _dot(vt_ref[0], jnp.concatenate(parts, axis=0))

    @pl.when(e == pl.num_programs(1) - 1)
    def _():
        y = h_ref[...] + acc_scr[...].T
        if final_norm:
            y = y * lax.rsqrt(jnp.mean(y * y, axis=-1, keepdims=True) + RMS_EPS) * fw_ref[...]
        out_ref[...] = y


def _peer(h, lw, layer, final_w=None):
    rows, d = h.shape
    tokens = min(PEER_TOKENS, rows)
    assert tokens % V7X_LANES == 0 and rows % tokens == 0
    eblk = PEER_EBLK
    n_half = PEER_HEADS * 2
    vmem = (2 * (2 * tokens * d * 4 + 2 * d * d * 2 + 2 * eblk * d * 2)
            + tokens * d * 2 + 4 * PEER_HEADS * PEER_KEYS * tokens * 4 + d * tokens * 4
            + 5 * eblk * tokens * 4)
    return pl.pallas_call(
        functools.partial(_peer_kernel, tokens=tokens, eblk=eblk, final_norm=final_w is not None),
        out_shape=jax.ShapeDtypeStruct((rows, d), f32),
        grid=(rows // tokens, PEER_EXPERTS // eblk),
        in_specs=[pl.BlockSpec((tokens, d), lambda i, e: (i, 0)),
                  pl.BlockSpec((1, d), lambda i, e: (0, 0)),
                  pl.BlockSpec((d, PEER_HEADS * PEER_QDIM), lambda i, e: (0, 0)),
                  pl.BlockSpec((d, PEER_HEADS * PEER_QDIM), lambda i, e: (0, 0)),
                  pl.BlockSpec((n_half, PEER_KEYS, PEER_QDIM // 2), lambda i, e: (0, 0, 0)),
                  pl.BlockSpec((n_half, PEER_KEYS, PEER_QDIM // 2), lambda i, e: (0, 0, 0)),
                  pl.BlockSpec((1, eblk, d), lambda i, e: (layer, e, 0)),
                  pl.BlockSpec((1, d, eblk), lambda i, e: (layer, 0, e)),
                  pl.BlockSpec((1, d), lambda i, e: (0, 0))],
        out_specs=pl.BlockSpec((tokens, d), lambda i, e: (i, 0)),
        scratch_shapes=[pltpu.VMEM((tokens, d), bf16),
                        pltpu.VMEM((PEER_HEADS, PEER_KEYS, tokens), f32),
                        pltpu.VMEM((PEER_HEADS, PEER_KEYS, tokens), f32),
                        pltpu.VMEM((PEER_HEADS, PEER_KEYS, tokens), bf16),
                        pltpu.VMEM((PEER_HEADS, PEER_KEYS, tokens), bf16),
                        pltpu.VMEM((PEER_KEYS, tokens), f32), pltpu.VMEM((PEER_KEYS, tokens), f32),
                        pltpu.VMEM((2 * PEER_TOPK, tokens), f32),
                        pltpu.VMEM((d, tokens), f32)],
        compiler_params=_cparams(("parallel", "arbitrary"), vmem),
        name="peer",
    )(h, lw["norm_ffn"], lw["peer_wq"], lw["peer_wq_lo"], lw["peer_sk"], lw["peer_sk_lo"], lw["peer_u"], lw["peer_vt"],
      (lw["norm_ffn"] if final_w is None else final_w).reshape(1, d))


def _layer_weights(l, p):
    w_in = p["w_in"][l]
    o_b, o_c, o_g = A_COLS, A_COLS + B_COLS, A_COLS + B_COLS + C_COLS
    lane = jnp.arange(V7X_LANES)
    on_alpha = (lane >= DN_HEADS) & (lane < 2 * DN_HEADS)

    def alpha_lanes(vec):
        return jnp.where(on_alpha, jnp.pad(vec, (DN_HEADS, V7X_LANES - 2 * DN_HEADS)), 0.0)

    sk = p["peer_subkeys"][l].reshape(PEER_HEADS * 2, PEER_KEYS, PEER_QDIM // 2)
    rw_vec = jnp.stack([p["rw_w0"][l], p["rw_a0"][l], p["rw_k_k"][l], p["rw_k_a"][l],
                        p["rw_r_k"][l].reshape(RW_WIDTH), p["rw_ln_w"][l], p["rw_ln_b"][l],
                        jnp.zeros((RW_WIDTH,), f32)])
    return {
        "norm_mix": p["norm_mix"][l],
        "w_a": _bf(w_in[:, 0:A_COLS]),
        "w_b": _bf(w_in[:, o_b:o_b + B_MAIN]),
        "w_bg": _bf(jnp.pad(w_in[:, o_b + B_MAIN:o_c], ((0, 0), (0, V7X_LANES - 2 * DN_HEADS)))),
        "w_c": _bf(w_in[:, o_c:o_g]),
        "w_g": _bf(w_in[:, o_g:]),
        "dn_conv_w": p["dn_conv_w"][l],
        "dn_prm": jnp.zeros((V7X_SUBLANES, V7X_LANES), f32)
                  .at[0].set(alpha_lanes(p["dn_a_log"][l])).at[1].set(alpha_lanes(p["dn_dt_bias"][l])),
        "dn_norm_w": jnp.tile(p["dn_norm_w"][l], DN_HEADS).reshape(1, DN_WIDTH),
        "bd128": _block_diag_ones(DN_WIDTH, DN_HD),
        "bd64": _block_diag_ones(RW_WIDTH, RW_HD),
        "rw_mu": p["rw_mu"][l].reshape(1, C_COLS),
        "rw_vec": rw_vec,
        "rw_w2": _bf(p["rw_w2"][l]), "rw_a2": _bf(p["rw_a2"][l]), "rw_g2": _bf(p["rw_g2"][l]),
        "w_br_a": _bf(p["w_br_a"][l]), "w_br_b": _bf(p["w_br_b"][l]), "w_br_c": _bf(p["w_br_c"][l]),
        "w_out": _bf(p["w_out"][l]),
        "norm_ffn": p["norm_ffn"][l].reshape(1, D_MODEL),
        "peer_wq": _bf(p["peer_wq"][l]),
        "peer_wq_lo": _bf(p["peer_wq"][l] - _bf(p["peer_wq"][l]).astype(f32)),
        "peer_sk": _bf(sk),
        "peer_sk_lo": _bf(sk - _bf(sk).astype(f32)),
        "peer_u": p["peer_u_bf"],
        "peer_vt": p["peer_vt_bf"],
    }


def _project(h, lw):
    nm = lw["norm_mix"]
    return {s: _norm_matmul(h, nm, lw["w_" + s]) for s in ("a", "b", "bg", "c", "g")}


def _kv_rows(pa, gi, lo, hi):
    k = pa[lo:hi, ATT_WIDTH + gi * ATT_OUT:ATT_WIDTH + (gi + 1) * ATT_OUT]
    v = pa[lo:hi, 2 * ATT_WIDTH + gi * ATT_OUT:2 * ATT_WIDTH + (gi + 1) * ATT_OUT]
    return jnp.stack([k, v], axis=1).reshape(hi - lo, 2, ATT_HPG, ATT_HD)


def _prompt_layer(h, lw, layer, final_w):
    t = h.shape[0]
    assert t % ATT_TILE == 0 and t % SEQ_TILE == 0
    pr = _project(h, lw)
    groups = [_attn_prompt_group(pr["a"], t, gi) for gi in range(len(ATT_GROUPS))]
    ob, dn_s = _dn_mixer(pr["b"], pr["bg"], jnp.zeros((1, V7X_SUBLANES, B_QKV), f32),
                         jnp.zeros((1, DN_HEADS, DN_HD, DN_HD), f32), lw,
                         nseq=1, tt=SEQ_TILE, chunk=DN_CHUNK, valid=(0, DN_CHUNK), nstate=1)
    oc, rw_s = _rw_mixer(pr["c"], jnp.zeros((1, 1, C_COLS), f32),
                         jnp.zeros((1, RW_HEADS, RW_HD, RW_HD), f32), lw,
                         nseq=1, tt=SEQ_TILE, chunk=RW_CHUNK, valid=(0, RW_CHUNK), nstate=1)
    h = _merge(h, [g[0] for g in groups], [g[1] for g in groups], ob, oc, pr["g"], lw)
    h = _peer(h, lw, layer, final_w)
    kvs = [_kv_rows(pr["a"], gi, t - min(win, t), t)[None] for gi, (win, _) in enumerate(ATT_GROUPS)]
    conv = pr["b"][t - (DN_CONV - 1):t, 0:B_QKV][None]
    shift = pr["c"][t - 1:t]
    return h, kvs, dn_s, conv, rw_s, shift


def _sample_rows(hist, x, nseq, seq):
    n = x.shape[1]
    nh = hist.shape[1]
    rows = jnp.concatenate([hist, x.reshape(nseq, seq, n), jnp.zeros((nseq, SAMPLE_ROWS - nh - seq, n), f32)], axis=1)
    return rows.reshape(nseq * SAMPLE_ROWS, n)


def _new_rows(x, nseq, nh, seq):
    return x.reshape(nseq, SAMPLE_ROWS, x.shape[1])[:, nh:nh + seq].reshape(nseq * seq, x.shape[1])


def _sample_layer(h, lw, layer, final_w, caches, dn_state, dn_conv, rw_state, rw_shift, nseq, seq):
    pr = _project(h, lw)
    oa = _attn_sample(pr["a"], caches, layer, nseq, seq)
    nh_dn, nh_rw = DN_CONV - 1, 1
    assert nseq % SAMPLE_STEP == 0 and nh_dn + seq <= SAMPLE_ROWS
    steps = nseq // SAMPLE_STEP
    tt = SAMPLE_STEP * SAMPLE_ROWS
    conv_hist = jnp.pad(dn_conv, ((0, 0), (0, 0), (0, B_MAIN - B_QKV)))
    ob, dn_s = _dn_mixer(_sample_rows(conv_hist, pr["b"], nseq, seq),
                         _sample_rows(jnp.zeros((nseq, nh_dn, V7X_LANES), f32), pr["bg"], nseq, seq),
                         jnp.zeros((steps, V7X_SUBLANES, B_QKV), f32), dn_state, lw,
                         nseq=steps, tt=tt, chunk=SAMPLE_ROWS, valid=(nh_dn, nh_dn + seq), nstate=SAMPLE_STEP)
    oc, rw_s = _rw_mixer(_sample_rows(rw_shift[:, None, :], pr["c"], nseq, seq),
                         jnp.zeros((steps, 1, C_COLS), f32), rw_state, lw,
                         nseq=steps, tt=tt, chunk=SAMPLE_ROWS, valid=(nh_rw, nh_rw + seq), nstate=SAMPLE_STEP)
    h = _merge(h, oa, None, _new_rows(ob, nseq, nh_dn, seq), _new_rows(oc, nseq, nh_rw, seq), pr["g"], lw)
    h = _peer(h, lw, layer, final_w)
    kvs = [_kv_rows(pr["a"], gi, 0, nseq * seq).reshape(nseq, seq, 2, ATT_HPG, ATT_HD)
           for gi in range(len(ATT_GROUPS))]
    conv = pr["b"][:, 0:B_QKV].reshape(nseq, seq, B_QKV)[:, seq - (DN_CONV - 1):]
    shift = pr["c"].reshape(nseq, seq, C_COLS)[:, seq - 1]
    return h, kvs, dn_s, conv, rw_s, shift


def kernel(x_prompt, x_sample, cache_kv_w128, cache_kv_w512, cache_kv_w2048, state_dn, state_dn_conv, state_rw, state_rw_shift, norm_mix, w_in, dn_conv_w, dn_a_log, dn_dt_bias, dn_norm_w, rw_mu, rw_w0, rw_w2, rw_a0, rw_a2, rw_g2, rw_k_k, rw_k_a, rw_r_k, rw_ln_w, rw_ln_b, w_br_a, w_br_b, w_br_c, w_out, norm_ffn, peer_wq, peer_subkeys, peer_u, peer_v, norm_final):
    p = dict(norm_mix=norm_mix, w_in=w_in, dn_conv_w=dn_conv_w, dn_a_log=dn_a_log, dn_dt_bias=dn_dt_bias,
             dn_norm_w=dn_norm_w, rw_mu=rw_mu, rw_w0=rw_w0, rw_w2=rw_w2, rw_a0=rw_a0, rw_a2=rw_a2, rw_g2=rw_g2,
             rw_k_k=rw_k_k, rw_k_a=rw_k_a, rw_r_k=rw_r_k, rw_ln_w=rw_ln_w, rw_ln_b=rw_ln_b, w_br_a=w_br_a,
             w_br_b=w_br_b, w_br_c=w_br_c, w_out=w_out, norm_ffn=norm_ffn, peer_wq=peer_wq,
             peer_subkeys=peer_subkeys, peer_u=peer_u, peer_v=peer_v)
    depth = w_in.shape[0]
    p["peer_u_bf"] = _bf(peer_u)
    p["peer_vt_bf"] = jnp.swapaxes(_bf(peer_v), 1, 2)
    bp, t, d = x_prompt.shape
    nseq, seq, _ = x_sample.shape
    assert bp == 1 and d == D_MODEL and seq <= SAMPLE_ROWS // 2 and DN_CONV - 1 <= seq
    caches = [c.reshape(depth, nseq, c.shape[2], 2 * ATT_OUT) for c in (cache_kv_w128, cache_kv_w512, cache_kv_w2048)]
    hp = x_prompt.reshape(t, d)
    hs = x_sample.reshape(nseq * seq, d)
    outs_p, outs_s = [], []
    for l in range(depth):
        lw = _layer_weights(l, p)
        final_w = norm_final if l == depth - 1 else None
        hp, *st_p = _prompt_layer(hp, lw, l, final_w)
        hs, *st_s = _sample_layer(hs, lw, l, final_w, caches, state_dn[l], state_dn_conv[l], state_rw[l],
                                  state_rw_shift[l], nseq, seq)
        outs_p.append(st_p)
        outs_s.append(st_s)
    y_p = hp.reshape(bp, t, d)
    y_s = hs.reshape(nseq, seq, d)

    def stack(outs, pick):
        return jnp.stack([pick(o) for o in outs], axis=0)

    res = [y_p, y_s]
    for gi in range(len(ATT_GROUPS)):
        res.append(stack(outs_p, lambda o: o[0][gi]))
        res.append(stack(outs_s, lambda o: o[0][gi]))
    for idx in (1, 2, 3, 4):
        res.append(stack(outs_p, lambda o: o[idx]))
        res.append(stack(outs_s, lambda o: o[idx]))
    return tuple(res)
```

```python
import functools
import math

import jax
import jax.numpy as jnp
import numpy as np
from jax import lax
from jax.experimental import pallas as pl
from jax.experimental.pallas import tpu as pltpu

f32 = jnp.float32
bf16 = jnp.bfloat16

V7X_LANES = 128
V7X_SUBLANES = 8
V7X_VMEM_BYTES = 64 * 1024 * 1024
VMEM_CEILING = 56 * 1024 * 1024

D_MODEL = 1024
ATT_GROUPS = ((128, 1), (512, 4), (2048, 16))
ATT_HPG = 4
ATT_HD = 64
ATT_HEADS = ATT_HPG * len(ATT_GROUPS)
ATT_WIDTH = ATT_HEADS * ATT_HD
ATT_OUT = ATT_HPG * ATT_HD
ATT_STEPS = 128
DN_HEADS = 4
DN_HD = 128
DN_WIDTH = DN_HEADS * DN_HD
DN_CONV = 4
DN_CHUNK = 64
DN_GROUP = 8
RW_HEADS = 8
RW_HD = 64
RW_WIDTH = RW_HEADS * RW_HD
RW_DECAY_LORA = 64
RW_A_LORA = 64
RW_GATE_LORA = 128
RW_GN_EPS = 64e-5
RW_CHUNK = 64
RW_GROUP = 8
PEER_KEYS = 128
PEER_EXPERTS = PEER_KEYS * PEER_KEYS
PEER_HEADS = 8
PEER_QDIM = 128
PEER_TOPK = 16
RMS_EPS = 1e-6
L2_EPS = 1e-6

A_COLS = 3 * ATT_WIDTH
B_QKV = 3 * DN_WIDTH
B_MAIN = 4 * DN_WIDTH
B_COLS = B_MAIN + 2 * DN_HEADS
C_COLS = 3 * RW_WIDTH + RW_DECAY_LORA + RW_A_LORA + RW_GATE_LORA
G_COLS = 3 * D_MODEL

ROW_TILE = 512
PROJ_TILE = 1024
ATT_TILE = 2048
ATT_GROUP = 4
SEQ_TILE = 512
SAMPLE_ROWS = 8
SAMPLE_STEP = 8
PEER_TOKENS = 512
PEER_EBLK = 2048

NEG_INF = float("-inf")
RANK_CODE = 2.0 ** 100
GELU_C1 = math.sqrt(2.0 / math.pi)
GELU_C3 = 0.044715 * GELU_C1


def _alibi_slopes():
    h = np.arange(1, ATT_HEADS + 1, dtype=np.float32)
    return np.power(np.float32(2.0), -8.0 * h / ATT_HEADS).astype(np.float32)


def _cparams(semantics, vmem_bytes):
    return pltpu.CompilerParams(dimension_semantics=semantics,
                                vmem_limit_bytes=int(min(max(vmem_bytes, 16 * 1024 * 1024), VMEM_CEILING)))


def _dot(a, b):
    return jnp.dot(a, b, preferred_element_type=f32)


def _dot_nt(a, b):
    return lax.dot_general(a, b, (((1,), (1,)), ((), ())), preferred_element_type=f32)


def _dot_tn(a, b):
    return lax.dot_general(a, b, (((0,), (0,)), ((), ())), preferred_element_type=f32)


def _bf(x):
    return x.astype(bf16)


def _sigmoid(x):
    return 1.0 / (1.0 + jnp.exp(-x))


def _segsum(x, bd):
    hi = _bf(x)
    lo = _bf(x - hi.astype(f32))
    return _dot(hi, bd) + _dot(lo, bd)


def _seg_cumsum(x, rowc, seg):
    k = 1
    while k < seg:
        x = x + jnp.where(rowc >= k, pltpu.roll(x, k, 0), 0.0)
        k *= 2
    return x


def _block_diag_ones(width, seg):
    i = np.arange(width)
    return jnp.asarray((i[:, None] // seg) == (i[None, :] // seg), dtype=bf16)


def _norm_matmul_kernel(h_ref, nw_ref, w_ref, o_ref):
    x = h_ref[...]
    xn = x * lax.rsqrt(jnp.mean(x * x, axis=-1, keepdims=True) + RMS_EPS) * nw_ref[...]
    o_ref[...] = _dot(_bf(xn), w_ref[...])


def _norm_matmul(h, nw, w):
    rows, d = h.shape
    n = w.shape[1]
    tr = min(PROJ_TILE, rows)
    vmem = 2 * (tr * d * 4 + d * n * 2 + tr * n * 4) + 4 * tr * d * 4
    return pl.pallas_call(
        _norm_matmul_kernel,
        out_shape=jax.ShapeDtypeStruct((rows, n), f32),
        grid=(rows // tr,),
        in_specs=[pl.BlockSpec((tr, d), lambda i: (i, 0)),
                  pl.BlockSpec((1, d), lambda i: (0, 0)),
                  pl.BlockSpec((d, n), lambda i: (0, 0))],
        out_specs=pl.BlockSpec((tr, n), lambda i: (i, 0)),
        compiler_params=_cparams(("parallel",), vmem),
        name="norm_matmul",
    )(h, nw.reshape(1, d), w)


def _attn_prompt_kernel(q_ref, kc_ref, vc_ref, kp_ref, vp_ref, o_ref, lse_ref, kk_scr, vv_scr, *, dil, slopes):
    n = ATT_STEPS
    nblk = ATT_TILE // (dil * n)
    tile = pl.program_id(0)
    pair = pl.program_id(1)
    kk_scr[0:ATT_TILE, :] = kp_ref[...]
    kk_scr[ATT_TILE:2 * ATT_TILE, :] = kc_ref[...]
    vv_scr[0:ATT_TILE, :] = vp_ref[...]
    vv_scr[ATT_TILE:2 * ATT_TILE, :] = vc_ref[...]
    ii = lax.broadcasted_iota(jnp.int32, (n, 2 * n), 0)
    jj = lax.broadcasted_iota(jnp.int32, (n, 2 * n), 1)
    steps = n + ii - jj
    band = (steps >= 0) & (steps <= n)
    dist = (steps * dil).astype(f32)
    biases = []
    for hh in range(2):
        slope = jnp.where(pair == 0, slopes[hh], slopes[2 + hh])
        biases.append(jnp.where(band, -slope * dist, NEG_INF))

    def body(gidx, carry):
        items = []
        for cc in range(ATT_GROUP):
            c = gidx * ATT_GROUP + cc
            r = c % dil
            b = c // dil
            qs = r + dil * n * b
            rows_q = pl.ds(qs, n, stride=dil)
            rows_k = pl.ds(ATT_TILE + qs - dil * n, 2 * n, stride=dil)
            q = q_ref[rows_q, :] * (ATT_HD ** -0.5)
            k = kk_scr[rows_k, :]
            first_key = jnp.where(tile * nblk + b == 0, n, 0)
            ss = [_dot_nt(_bf(q[:, hh * ATT_HD:(hh + 1) * ATT_HD]), _bf(k[:, hh * ATT_HD:(hh + 1) * ATT_HD]))
                  for hh in range(2)]
            items.append(dict(rows_q=rows_q, rows_k=rows_k, first_key=first_key, s=ss))
        for it in items:
            it["p"], it["den"], it["lse"] = [], [], []
            for hh in range(2):
                s = jnp.where(jj >= it["first_key"], it["s"][hh] + biases[hh], NEG_INF)
                mx = jnp.max(s, axis=-1, keepdims=True)
                p = jnp.exp(s - mx)
                den = jnp.sum(p, axis=-1, keepdims=True)
                it["p"].append(_bf(p))
                it["den"].append(den)
                it["lse"].append(jnp.broadcast_to(mx + jnp.log(den), (n, ATT_HD)))
        for it in items:
            v = vv_scr[it["rows_k"], :]
            outs = [_dot(it["p"][hh], _bf(v[:, hh * ATT_HD:(hh + 1) * ATT_HD])) / it["den"][hh] for hh in range(2)]
            o_ref[it["rows_q"], :] = jnp.concatenate(outs, axis=1)
            lse_ref[it["rows_q"], :] = jnp.concatenate(it["lse"], axis=1)
        return carry

    lax.fori_loop(0, dil * nblk // ATT_GROUP, body, 0)


def _attn_prompt_group(pa, seq_len, gi):
    _, dil = ATT_GROUPS[gi]
    slopes = tuple(float(s) for s in _alibi_slopes()[gi * ATT_HPG:(gi + 1) * ATT_HPG])
    w = V7X_LANES
    qb, kb, vb = (gi * 2, 6 + gi * 2, 12 + gi * 2)
    blk = (ATT_TILE, w)
    cur = lambda base: pl.BlockSpec(blk, lambda i, j, base=base: (i, base + j))
    prev = lambda base: pl.BlockSpec(blk, lambda i, j, base=base: (jnp.maximum(i - 1, 0), base + j))
    vmem = 2 * 7 * ATT_TILE * w * 4 + 2 * 2 * ATT_TILE * w * 4 + 8 * 1024 * 1024
    return pl.pallas_call(
        functools.partial(_attn_prompt_kernel, dil=dil, slopes=slopes),
        out_shape=(jax.ShapeDtypeStruct((seq_len, ATT_OUT), f32), jax.ShapeDtypeStruct((seq_len, ATT_OUT), f32)),
        grid=(seq_len // ATT_TILE, 2),
        in_specs=[cur(qb), cur(kb), cur(vb), prev(kb), prev(vb)],
        out_specs=(pl.BlockSpec(blk, lambda i, j: (i, j)), pl.BlockSpec(blk, lambda i, j: (i, j))),
        scratch_shapes=[pltpu.VMEM((2 * ATT_TILE, w), f32), pltpu.VMEM((2 * ATT_TILE, w), f32)],
        compiler_params=_cparams(("parallel", "parallel"), vmem),
        name=f"attn_prompt_g{gi}",
    )(pa, pa, pa, pa, pa)


def _attn_sample_kernel(q_ref, k_ref, v_ref, c0_ref, c1_ref, c2_ref, o_ref, *, slopes, seq):
    rows = 2 * seq
    caches = (c0_ref, c1_ref, c2_ref)
    row = lax.broadcasted_iota(jnp.int32, (rows, 1), 0)
    own0 = row < seq
    qpos = row % seq
    npos = lax.broadcasted_iota(jnp.int32, (rows, rows), 1)
    q_all = q_ref[0] * (ATT_HD ** -0.5)
    k_all = k_ref[0]
    v_all = v_ref[0]
    outs, lses = [], []
    for gi, (win, dil) in enumerate(ATT_GROUPS):
        cpos = lax.broadcasted_iota(jnp.int32, (rows, win), 1)
        dist_c = win + qpos - cpos
        ok_c = (dist_c <= win) & ((dist_c & (dil - 1)) == 0)
        dist_n = row - npos
        ok_n = (dist_n >= 0) & (dist_n <= qpos) & ((dist_n & (dil - 1)) == 0)
        dcf = dist_c.astype(f32)
        dnf = dist_n.astype(f32)
        for h in range(ATT_HPG):
            slope = slopes[gi * ATT_HPG + h]
            cols = slice(gi * ATT_OUT + h * ATT_HD, gi * ATT_OUT + (h + 1) * ATT_HD)
            kcols = slice(h * ATT_HD, (h + 1) * ATT_HD)
            vcols = slice(ATT_OUT + h * ATT_HD, ATT_OUT + (h + 1) * ATT_HD)
            qh = _bf(q_all[:, cols])
            sn = jnp.where(ok_n, _dot_nt(qh, _bf(k_all[:, cols])) - slope * dnf, NEG_INF)
            scs = []
            for b in range(2):
                sc = _dot_nt(qh, _bf(caches[gi][0, b, :, kcols])) - slope * dcf
                scs.append(jnp.where(ok_c, sc, NEG_INF))
            sc = jnp.where(own0, scs[0], scs[1])
            mx = jnp.maximum(jnp.max(sc, axis=-1, keepdims=True), jnp.max(sn, axis=-1, keepdims=True))
            pc = jnp.exp(sc - mx)
            pn = jnp.exp(sn - mx)
            den = jnp.sum(pc, axis=-1, keepdims=True) + jnp.sum(pn, axis=-1, keepdims=True)
            pcb = _bf(pc)
            oc = jnp.where(own0, _dot(pcb, _bf(caches[gi][0, 0, :, vcols])), _dot(pcb, _bf(caches[gi][0, 1, :, vcols])))
            o = (oc + _dot(_bf(pn), _bf(v_all[:, cols]))) / den
            outs.append(o)
            lses.append(mx + jnp.log(den))
    merged = []
    for h in range(ATT_HPG):
        ls = [lses[gi * ATT_HPG + h] for gi in range(len(ATT_GROUPS))]
        mx = jnp.maximum(jnp.maximum(ls[0], ls[1]), ls[2])
        es = [jnp.exp(l - mx) for l in ls]
        den = es[0] + es[1] + es[2]
        acc = es[0] * outs[h] + es[1] * outs[ATT_HPG + h] + es[2] * outs[2 * ATT_HPG + h]
        merged.append(acc / den)
    o_ref[0] = jnp.concatenate(merged, axis=1)


def _attn_sample(pa, caches, layer, nseq, seq):
    rows = 2 * seq
    assert rows == SAMPLE_ROWS and nseq % 2 == 0
    slopes = tuple(float(s) for s in _alibi_slopes())
    pa3 = pa.reshape(nseq // 2, rows, A_COLS)
    in_specs = [pl.BlockSpec((1, rows, ATT_WIDTH), lambda i, c=c: (i, 0, c)) for c in range(3)]
    vmem = 0
    for (win, _), c in zip(ATT_GROUPS, caches):
        assert c.shape[2] == win, "cached window must hold exactly `window` rows"
        in_specs.append(pl.BlockSpec((1, 2, win, 2 * ATT_OUT), lambda i: (layer, i, 0, 0)))
        vmem += 2 * 2 * win * 2 * ATT_OUT * 4
    out = pl.pallas_call(
        functools.partial(_attn_sample_kernel, slopes=slopes, seq=seq),
        out_shape=jax.ShapeDtypeStruct((nseq // 2, rows, ATT_OUT), f32),
        grid=(nseq // 2,),
        in_specs=in_specs,
        out_specs=pl.BlockSpec((1, rows, ATT_OUT), lambda i: (i, 0, 0)),
        compiler_params=_cparams(("parallel",), vmem + 16 * 1024 * 1024),
        name="attn_sample",
    )(pa3, pa3, pa3, *caches)
    return out.reshape(nseq * seq, ATT_OUT)


def _dn_kernel(pb_ref, pbg_ref, conv0_ref, s0_ref, cw_ref, prm_ref, nw_ref, bd_ref,
               o_ref, sout_ref,
               xs_scr, q_scr, k_scr, v_scr, g_scr, b_scr, oc_scr, qp_scr, m_scr, n_scr, s_scr,
               *, tt, chunk, valid, nstate):
    t = pl.program_id(1)

    @pl.when(t == 0)
    def _():
        xs_scr[0:V7X_SUBLANES, :] = conv0_ref[0]
        if nstate == 1:
            s_scr[...] = s0_ref[0]

    x = pb_ref[:, 0:B_QKV]
    xs_scr[pl.ds(V7X_SUBLANES, tt), :] = x
    cw = cw_ref[...]
    y = x * cw[3:4, :]
    for j in range(DN_CONV - 1):
        y = y + xs_scr[pl.ds(V7X_SUBLANES - (DN_CONV - 1) + j, tt), :] * cw[j:j + 1, :]
    xs_scr[0:V7X_SUBLANES, :] = xs_scr[pl.ds(tt, V7X_SUBLANES), :]
    y = y * _sigmoid(y)
    bd = bd_ref[...]
    q = y[:, 0:DN_WIDTH]
    k = y[:, DN_WIDTH:2 * DN_WIDTH]
    q_scr[...] = q * lax.rsqrt(_segsum(q * q, bd) + L2_EPS) * (DN_HD ** -0.5)
    k_scr[...] = k * lax.rsqrt(_segsum(k * k, bd) + L2_EPS)
    v_scr[...] = y[:, 2 * DN_WIDTH:3 * DN_WIDTH]
    pg = pbg_ref[...]
    beta = _sigmoid(pg)
    g = -jnp.exp(prm_ref[0:1, :]) * jax.nn.softplus(pg + prm_ref[1:2, :])
    rown = lax.broadcasted_iota(jnp.int32, (tt, V7X_LANES), 0)
    if valid != (0, chunk):
        rc = rown & (chunk - 1)
        beta = jnp.where(rc >= valid[0], jnp.where(rc < valid[1], beta, 0.0), 0.0)
        g = jnp.where(rc >= valid[0], jnp.where(rc < valid[1], g, 0.0), 0.0)
    g_scr[...] = _seg_cumsum(g, rown & (chunk - 1), chunk)
    b_scr[...] = beta

    ci = lax.broadcasted_iota(jnp.int32, (chunk, chunk), 0)
    cj = lax.broadcasted_iota(jnp.int32, (chunk, chunk), 1)
    incl = ci >= cj
    strict = ci > cj

    nchunks = tt // chunk
    group = min(DN_GROUP, nchunks)

    def precompute(gi, carry):
        items = []
        for cc in range(group):
            cidx = gi * group + cc
            rows = pl.ds(pl.multiple_of(cidx * chunk, chunk), chunk)
            gc = g_scr[rows, :]
            bc = b_scr[rows, :]
            gct = gc.T
            for h in range(DN_HEADS):
                cols = slice(h * DN_HD, (h + 1) * DN_HD)
                qh, kh, vh = q_scr[rows, cols], k_scr[rows, cols], v_scr[rows, cols]
                gcol = gc[:, DN_HEADS + h:DN_HEADS + h + 1]
                grow = gct[DN_HEADS + h:DN_HEADS + h + 1, :]
                bcol = bc[:, h:h + 1]
                glast = gc[chunk - 1:chunk, DN_HEADS + h:DN_HEADS + h + 1]
                decay = jnp.exp(jnp.where(incl, gcol - grow, NEG_INF))
                kb = kh * bcol
                eg = jnp.exp(gcol)
                both = _dot_nt(_bf(jnp.concatenate([kb, qh], axis=0)), _bf(kh))
                items.append(dict(
                    rows=rows, cols=cols, idx=cidx * DN_HEADS + h,
                    p=jnp.where(strict, -both[0:chunk] * decay, 0.0),
                    attn=jnp.where(incl, both[chunk:2 * chunk] * decay, 0.0),
                    sol=jnp.concatenate([vh * bcol, kb * eg], axis=1),
                    qeg=qh * eg, kdec=_bf(kh * jnp.exp(glast - gcol))))
        k = 1
        while k < chunk:
            for it in items:
                it["sol"] = it["sol"] + _dot(_bf(it["p"]), _bf(it["sol"]))
            if 2 * k < chunk:
                for it in items:
                    pb = _bf(it["p"])
                    it["p"] = _dot(pb, pb)
            k *= 2
        for it in items:
            u = _bf(it["sol"][:, 0:DN_HD])
            w = _bf(it["sol"][:, DN_HD:2 * DN_HD])
            attn = _bf(it["attn"])
            qp_scr[it["rows"], it["cols"]] = it["qeg"] - _dot(attn, w)
            oc_scr[it["rows"], it["cols"]] = _dot(attn, u)
            m_scr[it["idx"]] = _dot_tn(it["kdec"], w)
            n_scr[it["idx"]] = _dot_tn(it["kdec"], u)
        return carry

    lax.fori_loop(0, nchunks // group, precompute, 0)

    def recur(c, carry):
        rows = pl.ds(pl.multiple_of(c * chunk, chunk), chunk)
        g_tail = g_scr[pl.ds(pl.multiple_of((c + 1) * chunk - V7X_SUBLANES, V7X_SUBLANES), V7X_SUBLANES), :]
        states = [s_scr[h] if nstate == 1 else s0_ref[c, h] for h in range(DN_HEADS)]
        sb = [_bf(s) for s in states]
        outs = [_dot(_bf(qp_scr[rows, h * DN_HD:(h + 1) * DN_HD]), sb[h]) for h in range(DN_HEADS)]
        upd = [_dot(_bf(m_scr[c * DN_HEADS + h]), sb[h]) for h in range(DN_HEADS)]
        for h in range(DN_HEADS):
            cols = slice(h * DN_HD, (h + 1) * DN_HD)
            oc_scr[rows, cols] = oc_scr[rows, cols] + outs[h]
            g_end = jnp.exp(g_tail[V7X_SUBLANES - 1:V7X_SUBLANES, DN_HEADS + h:DN_HEADS + h + 1])
            new = states[h] * g_end - upd[h] + n_scr[c * DN_HEADS + h]
            if nstate == 1:
                s_scr[h] = new
            else:
                sout_ref[c, h] = new
        return carry

    lax.fori_loop(0, nchunks, recur, 0)

    o = oc_scr[...]
    z = pb_ref[:, B_QKV:B_MAIN]
    o = o * lax.rsqrt(_segsum(o * o, bd) * (1.0 / DN_HD) + RMS_EPS) * nw_ref[...]
    o_ref[...] = o * (z * _sigmoid(z))

    if nstate == 1:
        @pl.when(t == pl.num_programs(1) - 1)
        def _():
            sout_ref[0] = s_scr[...]


def _dn_mixer(pb, pbg, conv0, s0, lw, *, nseq, tt, chunk, valid, nstate):
    rows = pb.shape[0]
    nt = rows // (nseq * tt)
    nmat = (tt // chunk) * DN_HEADS
    vmem = (2 * (tt * (B_MAIN + V7X_LANES + DN_WIDTH) * 4 + DN_WIDTH * DN_WIDTH * 2)
            + (tt + 8) * B_QKV * 4 + 4 * tt * DN_WIDTH * 4 + 2 * tt * V7X_LANES * 4
            + tt * DN_WIDTH * 4 + (2 * nmat + (6 + 4 * nstate) * DN_HEADS) * DN_HD * DN_HD * 4 + 8 * tt * B_QKV * 4)
    return pl.pallas_call(
        functools.partial(_dn_kernel, tt=tt, chunk=chunk, valid=valid, nstate=nstate),
        out_shape=(jax.ShapeDtypeStruct((rows, DN_WIDTH), f32),
                   jax.ShapeDtypeStruct((nseq * nstate, DN_HEADS, DN_HD, DN_HD), f32)),
        grid=(nseq, nt),
        in_specs=[pl.BlockSpec((tt, B_MAIN), lambda s, t: (s * nt + t, 0)),
                  pl.BlockSpec((tt, V7X_LANES), lambda s, t: (s * nt + t, 0)),
                  pl.BlockSpec((1, V7X_SUBLANES, B_QKV), lambda s, t: (s, 0, 0)),
                  pl.BlockSpec((nstate, DN_HEADS, DN_HD, DN_HD), lambda s, t: (s, 0, 0, 0)),
                  pl.BlockSpec((DN_CONV, B_QKV), lambda s, t: (0, 0)),
                  pl.BlockSpec((V7X_SUBLANES, V7X_LANES), lambda s, t: (0, 0)),
                  pl.BlockSpec((1, DN_WIDTH), lambda s, t: (0, 0)),
                  pl.BlockSpec((DN_WIDTH, DN_WIDTH), lambda s, t: (0, 0))],
        out_specs=(pl.BlockSpec((tt, DN_WIDTH), lambda s, t: (s * nt + t, 0)),
                   pl.BlockSpec((nstate, DN_HEADS, DN_HD, DN_HD), lambda s, t: (s, 0, 0, 0))),
        scratch_shapes=[pltpu.VMEM((tt + V7X_SUBLANES, B_QKV), f32),
                        pltpu.VMEM((tt, DN_WIDTH), f32), pltpu.VMEM((tt, DN_WIDTH), f32),
                        pltpu.VMEM((tt, DN_WIDTH), f32),
                        pltpu.VMEM((tt, V7X_LANES), f32), pltpu.VMEM((tt, V7X_LANES), f32),
                        pltpu.VMEM((tt, DN_WIDTH), f32), pltpu.VMEM((tt, DN_WIDTH), f32),
                        pltpu.VMEM((nmat, DN_HD, DN_HD), f32), pltpu.VMEM((nmat, DN_HD, DN_HD), f32),
                        pltpu.VMEM((DN_HEADS, DN_HD, DN_HD), f32)],
        compiler_params=_cparams(("parallel", "arbitrary"), vmem),
        name="deltanet",
    )(pb, pbg, conv0, s0, lw["dn_conv_w"], lw["dn_prm"], lw["dn_norm_w"], lw["bd128"])


def _rw_kernel(pc_ref, shift0_ref, s0_ref, mu_ref, vec_ref, w2_ref, a2_ref, g2_ref, bd_ref,
               o_ref, sout_ref,
               xs_scr, ar_scr, bk_scr, v_scr, pe_scr, oc_scr, rp_scr, m_scr, n_scr, s_scr, *, tt, chunk, valid, nstate):
    t = pl.program_id(1)

    @pl.when(t == 0)
    def _():
        xs_scr[V7X_SUBLANES - 1:V7X_SUBLANES, :] = shift0_ref[0]
        if nstate == 1:
            s_scr[...] = s0_ref[0]

    pc = pc_ref[...]
    xs_scr[pl.ds(V7X_SUBLANES, tt), :] = pc
    prev = xs_scr[pl.ds(V7X_SUBLANES - 1, tt), :]
    xs_scr[0:V7X_SUBLANES, :] = xs_scr[pl.ds(tt, V7X_SUBLANES), :]
    xc = pc + (prev - pc) * mu_ref[...]
    w3 = 3 * RW_WIDTH
    r = xc[:, 0:RW_WIDTH]
    k = xc[:, RW_WIDTH:2 * RW_WIDTH]
    v = xc[:, 2 * RW_WIDTH:w3]
    wd = xc[:, w3:w3 + RW_DECAY_LORA]
    ad = xc[:, w3 + RW_DECAY_LORA:w3 + RW_DECAY_LORA + RW_A_LORA]
    gd = xc[:, w3 + RW_DECAY_LORA + RW_A_LORA:C_COLS]
    w0, a0, k_k, k_a, r_k, ln_w, ln_b = (vec_ref[i:i + 1, :] for i in range(7))
    w_log = -jax.nn.softplus(-(w0 + _dot(_bf(jnp.tanh(wd)), w2_ref[...]))) - 0.5
    lw = -jnp.exp(w_log)
    a = _sigmoid(a0 + _dot(_bf(ad), a2_ref[...]))
    gate = _dot(_bf(_sigmoid(gd)), g2_ref[...])
    bd = bd_ref[...]
    kkr = k * k_k
    kk = kkr * lax.rsqrt(_segsum(kkr * kkr, bd) + L2_EPS)
    k2 = k * (1.0 + (a - 1.0) * k_a)
    bonus = _segsum(r * k2 * r_k, bd) * v
    rown = lax.broadcasted_iota(jnp.int32, (tt, RW_WIDTH), 0)
    if valid != (0, chunk):
        rc = rown & (chunk - 1)
        pad = jnp.where(rc >= valid[0], jnp.where(rc < valid[1], 0, 1), 1) == 1
        lw = jnp.where(pad, 0.0, lw)
        kk = jnp.where(pad, 0.0, kk)
        k2 = jnp.where(pad, 0.0, k2)
    gcum = _seg_cumsum(lw, rown & (chunk - 1), chunk)
    e_pos = jnp.exp(gcum)
    e_neg = jnp.exp(-gcum)
    ar_scr[:, 0:RW_WIDTH] = -kk * jnp.exp(gcum - lw)
    ar_scr[:, RW_WIDTH:2 * RW_WIDTH] = r * e_pos
    bk_scr[:, 0:RW_WIDTH] = kk * a * e_neg
    bk_scr[:, RW_WIDTH:2 * RW_WIDTH] = k2 * e_neg
    v_scr[...] = v
    pe_scr[...] = e_pos

    ci = lax.broadcasted_iota(jnp.int32, (2 * chunk, 2 * chunk), 0)
    cj = lax.broadcasted_iota(jnp.int32, (2 * chunk, 2 * chunk), 1)
    keep = jnp.where(ci < chunk, ci - 1, ci - chunk) >= (cj & (chunk - 1))

    nchunks = tt // chunk
    group = min(RW_GROUP, nchunks)

    def precompute(gi, carry):
        items = []
        for cc in range(group):
            cidx = gi * group + cc
            rows = pl.ds(pl.multiple_of(cidx * chunk, chunk), chunk)
            p_tail = pe_scr[pl.ds(pl.multiple_of((cidx + 1) * chunk - V7X_SUBLANES, V7X_SUBLANES), V7X_SUBLANES), :]
            for h in range(RW_HEADS):
                cols = slice(h * RW_HD, (h + 1) * RW_HD)
                cols2 = slice(RW_WIDTH + h * RW_HD, RW_WIDTH + (h + 1) * RW_HD)
                at, rt = ar_scr[rows, cols], ar_scr[rows, cols2]
                bt, kt = bk_scr[rows, cols], bk_scr[rows, cols2]
                bkb = _bf(jnp.concatenate([bt, kt], axis=0))
                m = jnp.where(keep, _dot_nt(_bf(jnp.concatenate([at, rt], axis=0)), bkb), 0.0)
                items.append(dict(rows=rows, cols=cols, idx=cidx * RW_HEADS + h, at=at, rt=rt, bt=_bf(bt), bkb=bkb,
                                  vh=v_scr[rows, cols], p=m[0:chunk, 0:chunk], aak=_bf(m[0:chunk, chunk:2 * chunk]),
                                  mlow=_bf(m[chunk:2 * chunk, :]), p_end=p_tail[V7X_SUBLANES - 1:V7X_SUBLANES, cols]))
        for it in items:
            it["sol"] = jnp.concatenate([it["at"], _dot(it["aak"], _bf(it["vh"]))], axis=1)
        k = 1
        while k < chunk:
            for it in items:
                it["sol"] = it["sol"] + _dot(_bf(it["p"]), _bf(it["sol"]))
            if 2 * k < chunk:
                for it in items:
                    pb = _bf(it["p"])
                    it["p"] = _dot(pb, pb)
            k *= 2
        for it in items:
            wm = _bf(it["sol"][:, 0:RW_HD])
            uv = _bf(jnp.concatenate([it["sol"][:, RW_HD:2 * RW_HD], it["vh"]], axis=0))
            rp_scr[it["rows"], it["cols"]] = it["rt"] + _dot(it["mlow"][:, 0:chunk], wm)
            oc_scr[it["rows"], it["cols"]] = _dot(it["mlow"], uv)
            m_scr[it["idx"]] = _dot_tn(wm, it["bt"]) * it["p_end"]
            n_scr[it["idx"]] = _dot_tn(uv, it["bkb"]) * it["p_end"]
        return carry

    lax.fori_loop(0, nchunks // group, precompute, 0)

    def recur(c, carry):
        rows = pl.ds(pl.multiple_of(c * chunk, chunk), chunk)
        p_tail = pe_scr[pl.ds(pl.multiple_of((c + 1) * chunk - V7X_SUBLANES, V7X_SUBLANES), V7X_SUBLANES), :]
        states = [s_scr[h] if nstate == 1 else s0_ref[c, h] for h in range(RW_HEADS)]
        sb = [_bf(s) for s in states]
        outs = [_dot_nt(_bf(rp_scr[rows, h * RW_HD:(h + 1) * RW_HD]), sb[h]) for h in range(RW_HEADS)]
        upd = [_dot(sb[h], _bf(m_scr[c * RW_HEADS + h])) for h in range(RW_HEADS)]
        for h in range(RW_HEADS):
            cols = slice(h * RW_HD, (h + 1) * RW_HD)
            oc_scr[rows, cols] = oc_scr[rows, cols] + outs[h]
            new = states[h] * p_tail[V7X_SUBLANES - 1:V7X_SUBLANES, cols] + upd[h] + n_scr[c * RW_HEADS + h]
            if nstate == 1:
                s_scr[h] = new
            else:
                sout_ref[c, h] = new
        return carry

    lax.fori_loop(0, nchunks, recur, 0)

    o = oc_scr[...]
    mean = _segsum(o, bd) * (1.0 / RW_HD)
    d = o - mean
    var = _segsum(d * d, bd) * (1.0 / RW_HD)
    o = d * lax.rsqrt(var + RW_GN_EPS) * ln_w + ln_b
    o_ref[...] = (o + bonus) * gate

    if nstate == 1:
        @pl.when(t == pl.num_programs(1) - 1)
        def _():
            sout_ref[0] = s_scr[...]


def _rw_mixer(pc, shift0, s0, lw, *, nseq, tt, chunk, valid, nstate):
    rows = pc.shape[0]
    nt = rows // (nseq * tt)
    nmat = (tt // chunk) * RW_HEADS
    vmem = (2 * (tt * (C_COLS + RW_WIDTH) * 4 + RW_WIDTH * RW_WIDTH * 2 + 4 * RW_WIDTH * V7X_LANES * 2)
            + (tt + 8) * C_COLS * 4 + 8 * tt * RW_WIDTH * 4 + (2 * nmat + (6 + 4 * nstate) * RW_HEADS) * RW_HD * V7X_LANES * 4
            + 14 * tt * RW_WIDTH * 4)
    const = lambda shape: pl.BlockSpec(shape, lambda s, t: (0,) * len(shape))
    return pl.pallas_call(
        functools.partial(_rw_kernel, tt=tt, chunk=chunk, valid=valid, nstate=nstate),
        out_shape=(jax.ShapeDtypeStruct((rows, RW_WIDTH), f32),
                   jax.ShapeDtypeStruct((nseq * nstate, RW_HEADS, RW_HD, RW_HD), f32)),
        grid=(nseq, nt),
        in_specs=[pl.BlockSpec((tt, C_COLS), lambda s, t: (s * nt + t, 0)),
                  pl.BlockSpec((1, 1, C_COLS), lambda s, t: (s, 0, 0)),
                  pl.BlockSpec((nstate, RW_HEADS, RW_HD, RW_HD), lambda s, t: (s, 0, 0, 0)),
                  const((1, C_COLS)), const((V7X_SUBLANES, RW_WIDTH)),
                  const((RW_DECAY_LORA, RW_WIDTH)), const((RW_A_LORA, RW_WIDTH)), const((RW_GATE_LORA, RW_WIDTH)),
                  const((RW_WIDTH, RW_WIDTH))],
        out_specs=(pl.BlockSpec((tt, RW_WIDTH), lambda s, t: (s * nt + t, 0)),
                   pl.BlockSpec((nstate, RW_HEADS, RW_HD, RW_HD), lambda s, t: (s, 0, 0, 0))),
        scratch_shapes=[pltpu.VMEM((tt + V7X_SUBLANES, C_COLS), f32),
                        pltpu.VMEM((tt, 2 * RW_WIDTH), f32), pltpu.VMEM((tt, 2 * RW_WIDTH), f32),
                        pltpu.VMEM((tt, RW_WIDTH), f32), pltpu.VMEM((tt, RW_WIDTH), f32),
                        pltpu.VMEM((tt, RW_WIDTH), f32), pltpu.VMEM((tt, RW_WIDTH), f32),
                        pltpu.VMEM((nmat, RW_HD, RW_HD), f32), pltpu.VMEM((nmat, RW_HD, RW_HD), f32),
                        pltpu.VMEM((RW_HEADS, RW_HD, RW_HD), f32)],
        compiler_params=_cparams(("parallel", "arbitrary"), vmem),
        name="rwkv7",
    )(pc, shift0, s0, lw["rw_mu"], lw["rw_vec"], lw["rw_w2"], lw["rw_a2"], lw["rw_g2"], lw["bd64"])


def _merge_kernel(h_ref, o0_ref, o1_ref, o2_ref, l0_ref, l1_ref, l2_ref, ob_ref, oc_ref, pg_ref,
                  wa_ref, wb_ref, wc_ref, wo_ref, out_ref):
    l0, l1, l2 = l0_ref[...], l1_ref[...], l2_ref[...]
    mx = jnp.maximum(jnp.maximum(l0, l1), l2)
    e0, e1, e2 = jnp.exp(l0 - mx), jnp.exp(l1 - mx), jnp.exp(l2 - mx)
    oa = (e0 * o0_ref[...] + e1 * o1_ref[...] + e2 * o2_ref[...]) / (e0 + e1 + e2)
    d = D_MODEL
    merged = (_sigmoid(pg_ref[:, 0:d]) * _dot(_bf(oa), wa_ref[...])
              + _sigmoid(pg_ref[:, d:2 * d]) * _dot(_bf(ob_ref[...]), wb_ref[...])
              + _sigmoid(pg_ref[:, 2 * d:3 * d]) * _dot(_bf(oc_ref[...]), wc_ref[...]))
    out_ref[...] = h_ref[...] + _dot(_bf(merged), wo_ref[...])


def _merge_sample_kernel(h_ref, oa_ref, ob_ref, oc_ref, pg_ref, wa_ref, wb_ref, wc_ref, wo_ref, out_ref):
    d = D_MODEL
    merged = (_sigmoid(pg_ref[:, 0:d]) * _dot(_bf(oa_ref[...]), wa_ref[...])
              + _sigmoid(pg_ref[:, d:2 * d]) * _dot(_bf(ob_ref[...]), wb_ref[...])
              + _sigmoid(pg_ref[:, 2 * d:3 * d]) * _dot(_bf(oc_ref[...]), wc_ref[...]))
    out_ref[...] = h_ref[...] + _dot(_bf(merged), wo_ref[...])


def _merge(h, oas, lses, ob, oc, pg, lw):
    rows, d = h.shape
    tr = min(ROW_TILE, rows)
    row = lambda n: pl.BlockSpec((tr, n), lambda i: (i, 0))
    const = lambda a: pl.BlockSpec(a.shape, lambda i: (0, 0))
    ws = (lw["w_br_a"], lw["w_br_b"], lw["w_br_c"], lw["w_out"])
    if lses is None:
        kern, acts = _merge_sample_kernel, (oas,)
    else:
        kern, acts = _merge_kernel, (*oas, *lses)
    acts = (h, *acts, ob, oc, pg)
    vmem = 2 * sum(tr * a.shape[1] * 4 for a in acts) + 2 * sum(w.size * 2 for w in ws) + 8 * tr * d * 4
    return pl.pallas_call(
        kern,
        out_shape=jax.ShapeDtypeStruct((rows, d), f32),
        grid=(rows // tr,),
        in_specs=[row(a.shape[1]) for a in acts] + [const(w) for w in ws],
        out_specs=row(d),
        compiler_params=_cparams(("parallel",), vmem),
        name="merge",
    )(*acts, *ws)


def _peer_candidate_groups(tv_scr):
    k = PEER_TOPK
    s = V7X_SUBLANES
    a_lo = tv_scr[0:s, :]
    a_hi = tv_scr[s:k, :]
    b = [tv_scr[k + j:k + j + 1, :] for j in range(s)]
    b_hi = tv_scr[k + s:2 * k, :]
    row = lax.broadcasted_iota(jnp.int32, a_lo.shape, 0)
    groups = [a_lo + b[0], a_hi + b[0], a_lo + b[1]]
    for j in range(2, s):
        groups.append(jnp.where(row < k // (j + 1), a_lo + b[j], NEG_INF))
    groups.append(tv_scr[0:1, :] + b_hi)
    return groups


def _peer_kernel(h_ref, nw_ref, wq_ref, wql_ref, sk_ref, skl_ref, u_ref, vt_ref, fw_ref, out_ref,
                 xn_scr, n1_scr, e1_scr, r2_scr, e2_scr, sc_scr, rk_scr, tv_scr, acc_scr, *, tokens, eblk, final_norm):
    e = pl.program_id(1)
    k = PEER_TOPK
    half = PEER_QDIM // 2
    unranked = float(k + 1)

    @pl.when(e == 0)
    def _():
        x = h_ref[...]
        xf = x * lax.rsqrt(jnp.mean(x * x, axis=-1, keepdims=True) + RMS_EPS) * nw_ref[...]
        xn = _bf(xf)
        xn_scr[...] = xn
        xl = _bf(xf - xn.astype(f32))
        qf = _dot(xn, wq_ref[...]) + (_dot(xn, wql_ref[...]) + _dot(xl, wq_ref[...]))
        q = _bf(qf)
        ql = _bf(qf - q.astype(f32))
        lane_tiles = [slice(lt * V7X_LANES, (lt + 1) * V7X_LANES) for lt in range(tokens // V7X_LANES)]
        for h in range(PEER_HEADS):
            for p in range(2):
                hp = 2 * h + p
                qcols = slice(hp * half, (hp + 1) * half)
                s = _dot_nt(sk_ref[hp], q[:, qcols]) + (_dot_nt(sk_ref[hp], ql[:, qcols])
                                                        + _dot_nt(skl_ref[hp], q[:, qcols]))
                if p == 0:
                    e1_scr[h] = s
                else:
                    sc_scr[...] = s
                for lanes in lane_tiles:
                    cur = s[:, lanes]
                    for i in range(k):
                        m = jnp.max(cur, axis=0, keepdims=True)
                        tv_scr[p * k + i:p * k + i + 1, lanes] = m
                        cur = jnp.where(cur >= m, -RANK_CODE * (1.0 + (i + 1) / 64.0), cur)
                    rank = jnp.where(cur <= -RANK_CODE, (cur * (-1.0 / RANK_CODE) - 1.0) * 64.0, unranked)
                    if p == 0:
                        rk_scr[:, lanes] = rank
                    else:
                        r2_scr[h, :, lanes] = _bf(rank)
            cur = _peer_candidate_groups(tv_scr)
            thr = None
            for i in range(k):
                m = cur[0]
                for g in cur[1:]:
                    m = jnp.maximum(m, g)
                thr = jnp.max(m, axis=0, keepdims=True)
                cur = [jnp.where(g >= thr, NEG_INF, g) for g in cur]
            top1 = tv_scr[0:1, :]
            top2 = tv_scr[k:k + 1, :]
            tops1 = tv_scr[0:k, :]
            cnt = jnp.zeros(tops1.shape, f32)
            z = jnp.zeros_like(thr)
            for j in range(k):
                sums = tops1 + tv_scr[k + j:k + j + 1, :]
                ok = sums >= thr
                cnt = cnt + jnp.where(ok, 1.0, 0.0)
                z = z + jnp.sum(jnp.where(ok, jnp.exp(sums - (top1 + top2)), 0.0), axis=0, keepdims=True)
            scale2 = 0.5 / z
            for lanes in lane_tiles:
                rank1 = rk_scr[:, lanes]
                n1 = jnp.zeros(rank1.shape, f32)
                for i in range(k):
                    n1 = jnp.where(rank1 == float(i + 1), cnt[i:i + 1, lanes], n1)
                n1_scr[h, :, lanes] = n1
                e1_scr[h, :, lanes] = jnp.exp(e1_scr[h, :, lanes] - top1[:, lanes])
                e2_scr[h, :, lanes] = _bf(jnp.exp(sc_scr[:, lanes] - top2[:, lanes]) * scale2[:, lanes])
        acc_scr[...] = jnp.zeros_like(acc_scr)

    nb1 = eblk // PEER_KEYS
    nlt = tokens // V7X_LANES
    pack = 2 * V7X_SUBLANES
    xu = _dot_nt(u_ref[0], xn_scr[...])
    act2 = xu + xu * jnp.tanh(xu * (GELU_C1 + GELU_C3 * (xu * xu)))
    zero = jnp.zeros((), bf16)
    parts = []
    for ip in range(nb1 // 2):
        i1s = [e * nb1 + 2 * ip + j for j in range(2)]
        nrows = [[n1_scr[h, pl.ds(i1, 1), :] for h in range(PEER_HEADS)] for i1 in i1s]
        erows = [[e1_scr[h, pl.ds(i1, 1), :] for h in range(PEER_HEADS)] for i1 in i1s]
        tiles = [[None] * nlt for _ in range(2)]
        for lt in range(nlt):
            lanes = slice(lt * V7X_LANES, (lt + 1) * V7X_LANES)
            g = [None, None]
            for h in range(PEER_HEADS):
                r2t = r2_scr[h, :, lanes]
                e2t = e2_scr[h, :, lanes]
                for j in range(2):
                    nb = _bf(jnp.broadcast_to(nrows[j][h][:, lanes], (pack, V7X_LANES)))
                    eb = _bf(jnp.broadcast_to(erows[j][h][:, lanes], (pack, V7X_LANES)))
                    nb = jnp.concatenate([nb] * (PEER_KEYS // pack), axis=0)
                    eb = jnp.concatenate([eb] * (PEER_KEYS // pack), axis=0)
                    hit = jnp.where(r2t <= nb, e2t * eb, zero)
                    g[j] = hit if g[j] is None else g[j] + hit
            for j in range(2):
                tiles[j][lt] = g[j]
        for j in range(2):
            il = 2 * ip + j
            gate = tiles[j][0] if nlt == 1 else jnp.concatenate(tiles[j], axis=1)
            parts.append(gate * _bf(act2[il * PEER_KEYS:(il + 1) * PEER_KEYS, :]))
    acc_scr[...] += _dot(vt_ref[0], jnp.concatenate(parts, axis=0))

    @pl.when(e == pl.num_programs(1) - 1)
    def _():
        y = h_ref[...] + acc_scr[...].T
        if final_norm:
            y = y * lax.rsqrt(jnp.mean(y * y, axis=-1, keepdims=True) + RMS_EPS) * fw_ref[...]
        out_ref[...] = y


def _peer(h, lw, layer, final_w=None):
    rows, d = h.shape
    tokens = min(PEER_TOKENS, rows)
    assert tokens % V7X_LANES == 0 and rows % tokens == 0
    eblk = PEER_EBLK
    n_half = PEER_HEADS * 2
    vmem = (2 * (2 * tokens * d * 4 + 2 * d * d * 2 + 2 * eblk * d * 2)
            + tokens * d * 2 + 4 * PEER_HEADS * PEER_KEYS * tokens * 4 + d * tokens * 4
            + 5 * eblk * tokens * 4)
    return pl.pallas_call(
        functools.partial(_peer_kernel, tokens=tokens, eblk=eblk, final_norm=final_w is not None),
        out_shape=jax.ShapeDtypeStruct((rows, d), f32),
        grid=(rows // tokens, PEER_EXPERTS // eblk),
        in_specs=[pl.BlockSpec((tokens, d), lambda i, e: (i, 0)),
                  pl.BlockSpec((1, d), lambda i, e: (0, 0)),
                  pl.BlockSpec((d, PEER_HEADS * PEER_QDIM), lambda i, e: (0, 0)),
                  pl.BlockSpec((d, PEER_HEADS * PEER_QDIM), lambda i, e: (0, 0)),
                  pl.BlockSpec((n_half, PEER_KEYS, PEER_QDIM // 2), lambda i, e: (0, 0, 0)),
                  pl.BlockSpec((n_half, PEER_KEYS, PEER_QDIM // 2), lambda i, e: (0, 0, 0)),
                  pl.BlockSpec((1, eblk, d), lambda i, e: (layer, e, 0)),
                  pl.BlockSpec((1, d, eblk), lambda i, e: (layer, 0, e)),
                  pl.BlockSpec((1, d), lambda i, e: (0, 0))],
        out_specs=pl.BlockSpec((tokens, d), lambda i, e: (i, 0)),
        scratch_shapes=[pltpu.VMEM((tokens, d), bf16),
                        pltpu.VMEM((PEER_HEADS, PEER_KEYS, tokens), f32),
                        pltpu.VMEM((PEER_HEADS, PEER_KEYS, tokens), f32),
                        pltpu.VMEM((PEER_HEADS, PEER_KEYS, tokens), bf16),
                        pltpu.VMEM((PEER_HEADS, PEER_KEYS, tokens), bf16),
                        pltpu.VMEM((PEER_KEYS, tokens), f32), pltpu.VMEM((PEER_KEYS, tokens), f32),
                        pltpu.VMEM((2 * PEER_TOPK, tokens), f32),
                        pltpu.VMEM((d, tokens), f32)],
        compiler_params=_cparams(("parallel", "arbitrary"), vmem),
        name="peer",
    )(h, lw["norm_ffn"], lw["peer_wq"], lw["peer_wq_lo"], lw["peer_sk"], lw["peer_sk_lo"], lw["peer_u"], lw["peer_vt"],
      (lw["norm_ffn"] if final_w is None else final_w).reshape(1, d))


def _layer_weights(l, p):
    w_in = p["w_in"][l]
    o_b, o_c, o_g = A_COLS, A_COLS + B_COLS, A_COLS + B_COLS + C_COLS
    lane = jnp.arange(V7X_LANES)
    on_alpha = (lane >= DN_HEADS) & (lane < 2 * DN_HEADS)

    def alpha_lanes(vec):
        return jnp.where(on_alpha, jnp.pad(vec, (DN_HEADS, V7X_LANES - 2 * DN_HEADS)), 0.0)

    sk = p["peer_subkeys"][l].reshape(PEER_HEADS * 2, PEER_KEYS, PEER_QDIM // 2)
    rw_vec = jnp.stack([p["rw_w0"][l], p["rw_a0"][l], p["rw_k_k"][l], p["rw_k_a"][l],
                        p["rw_r_k"][l].reshape(RW_WIDTH), p["rw_ln_w"][l], p["rw_ln_b"][l],
                        jnp.zeros((RW_WIDTH,), f32)])
    return {
        "norm_mix": p["norm_mix"][l],
        "w_a": _bf(w_in[:, 0:A_COLS]),
        "w_b": _bf(w_in[:, o_b:o_b + B_MAIN]),
        "w_bg": _bf(jnp.pad(w_in[:, o_b + B_MAIN:o_c], ((0, 0), (0, V7X_LANES - 2 * DN_HEADS)))),
        "w_c": _bf(w_in[:, o_c:o_g]),
        "w_g": _bf(w_in[:, o_g:]),
        "dn_conv_w": p["dn_conv_w"][l],
        "dn_prm": jnp.zeros((V7X_SUBLANES, V7X_LANES), f32)
                  .at[0].set(alpha_lanes(p["dn_a_log"][l])).at[1].set(alpha_lanes(p["dn_dt_bias"][l])),
        "dn_norm_w": jnp.tile(p["dn_norm_w"][l], DN_HEADS).reshape(1, DN_WIDTH),
        "bd128": _block_diag_ones(DN_WIDTH, DN_HD),
        "bd64": _block_diag_ones(RW_WIDTH, RW_HD),
        "rw_mu": p["rw_mu"][l].reshape(1, C_COLS),
        "rw_vec": rw_vec,
        "rw_w2": _bf(p["rw_w2"][l]), "rw_a2": _bf(p["rw_a2"][l]), "rw_g2": _bf(p["rw_g2"][l]),
        "w_br_a": _bf(p["w_br_a"][l]), "w_br_b": _bf(p["w_br_b"][l]), "w_br_c": _bf(p["w_br_c"][l]),
        "w_out": _bf(p["w_out"][l]),
        "norm_ffn": p["norm_ffn"][l].reshape(1, D_MODEL),
        "peer_wq": _bf(p["peer_wq"][l]),
        "peer_wq_lo": _bf(p["peer_wq"][l] - _bf(p["peer_wq"][l]).astype(f32)),
        "peer_sk": _bf(sk),
        "peer_sk_lo": _bf(sk - _bf(sk).astype(f32)),
        "peer_u": p["peer_u_bf"],
        "peer_vt": p["peer_vt_bf"],
    }


def _project(h, lw):
    nm = lw["norm_mix"]
    return {s: _norm_matmul(h, nm, lw["w_" + s]) for s in ("a", "b", "bg", "c", "g")}


def _kv_rows(pa, gi, lo, hi):
    k = pa[lo:hi, ATT_WIDTH + gi * ATT_OUT:ATT_WIDTH + (gi + 1) * ATT_OUT]
    v = pa[lo:hi, 2 * ATT_WIDTH + gi * ATT_OUT:2 * ATT_WIDTH + (gi + 1) * ATT_OUT]
    return jnp.stack([k, v], axis=1).reshape(hi - lo, 2, ATT_HPG, ATT_HD)


def _prompt_layer(h, lw, layer, final_w):
    t = h.shape[0]
    assert t % ATT_TILE == 0 and t % SEQ_TILE == 0
    pr = _project(h, lw)
    groups = [_attn_prompt_group(pr["a"], t, gi) for gi in range(len(ATT_GROUPS))]
    ob, dn_s = _dn_mixer(pr["b"], pr["bg"], jnp.zeros((1, V7X_SUBLANES, B_QKV), f32),
                         jnp.zeros((1, DN_HEADS, DN_HD, DN_HD), f32), lw,
                         nseq=1, tt=SEQ_TILE, chunk=DN_CHUNK, valid=(0, DN_CHUNK), nstate=1)
    oc, rw_s = _rw_mixer(pr["c"], jnp.zeros((1, 1, C_COLS), f32),
                         jnp.zeros((1, RW_HEADS, RW_HD, RW_HD), f32), lw,
                         nseq=1, tt=SEQ_TILE, chunk=RW_CHUNK, valid=(0, RW_CHUNK), nstate=1)
    h = _merge(h, [g[0] for g in groups], [g[1] for g in groups], ob, oc, pr["g"], lw)
    h = _peer(h, lw, layer, final_w)
    kvs = [_kv_rows(pr["a"], gi, t - min(win, t), t)[None] for gi, (win, _) in enumerate(ATT_GROUPS)]
    conv = pr["b"][t - (DN_CONV - 1):t, 0:B_QKV][None]
    shift = pr["c"][t - 1:t]
    return h, kvs, dn_s, conv, rw_s, shift


def _sample_rows(hist, x, nseq, seq):
    n = x.shape[1]
    nh = hist.shape[1]
    rows = jnp.concatenate([hist, x.reshape(nseq, seq, n), jnp.zeros((nseq, SAMPLE_ROWS - nh - seq, n), f32)], axis=1)
    return rows.reshape(nseq * SAMPLE_ROWS, n)


def _new_rows(x, nseq, nh, seq):
    return x.reshape(nseq, SAMPLE_ROWS, x.shape[1])[:, nh:nh + seq].reshape(nseq * seq, x.shape[1])


def _sample_layer(h, lw, layer, final_w, caches, dn_state, dn_conv, rw_state, rw_shift, nseq, seq):
    pr = _project(h, lw)
    oa = _attn_sample(pr["a"], caches, layer, nseq, seq)
    nh_dn, nh_rw = DN_CONV - 1, 1
    assert nseq % SAMPLE_STEP == 0 and nh_dn + seq <= SAMPLE_ROWS
    steps = nseq // SAMPLE_STEP
    tt = SAMPLE_STEP * SAMPLE_ROWS
    conv_hist = jnp.pad(dn_conv, ((0, 0), (0, 0), (0, B_MAIN - B_QKV)))
    ob, dn_s = _dn_mixer(_sample_rows(conv_hist, pr["b"], nseq, seq),
                         _sample_rows(jnp.zeros((nseq, nh_dn, V7X_LANES), f32), pr["bg"], nseq, seq),
                         jnp.zeros((steps, V7X_SUBLANES, B_QKV), f32), dn_state, lw,
                         nseq=steps, tt=tt, chunk=SAMPLE_ROWS, valid=(nh_dn, nh_dn + seq), nstate=SAMPLE_STEP)
    oc, rw_s = _rw_mixer(_sample_rows(rw_shift[:, None, :], pr["c"], nseq, seq),
                         jnp.zeros((steps, 1, C_COLS), f32), rw_state, lw,
                         nseq=steps, tt=tt, chunk=SAMPLE_ROWS, valid=(nh_rw, nh_rw + seq), nstate=SAMPLE_STEP)
    h = _merge(h, oa, None, _new_rows(ob, nseq, nh_dn, seq), _new_rows(oc, nseq, nh_rw, seq), pr["g"], lw)
    h = _peer(h, lw, layer, final_w)
    kvs = [_kv_rows(pr["a"], gi, 0, nseq * seq).reshape(nseq, seq, 2, ATT_HPG, ATT_HD)
           for gi in range(len(ATT_GROUPS))]
    conv = pr["b"][:, 0:B_QKV].reshape(nseq, seq, B_QKV)[:, seq - (DN_CONV - 1):]
    shift = pr["c"].reshape(nseq, seq, C_COLS)[:, seq - 1]
    return h, kvs, dn_s, conv, rw_s, shift


def kernel(x_prompt, x_sample, cache_kv_w128, cache_kv_w512, cache_kv_w2048, state_dn, state_dn_conv, state_rw, state_rw_shift, norm_mix, w_in, dn_conv_w, dn_a_log, dn_dt_bias, dn_norm_w, rw_mu, rw_w0, rw_w2, rw_a0, rw_a2, rw_g2, rw_k_k, rw_k_a, rw_r_k, rw_ln_w, rw_ln_b, w_br_a, w_br_b, w_br_c, w_out, norm_ffn, peer_wq, peer_subkeys, peer_u, peer_v, norm_final):
    p = dict(norm_mix=norm_mix, w_in=w_in, dn_conv_w=dn_conv_w, dn_a_log=dn_a_log, dn_dt_bias=dn_dt_bias,
             dn_norm_w=dn_norm_w, rw_mu=rw_mu, rw_w0=rw_w0, rw_w2=rw_w2, rw_a0=rw_a0, rw_a2=rw_a2, rw_g2=rw_g2,
             rw_k_k=rw_k_k, rw_k_a=rw_k_a, rw_r_k=rw_r_k, rw_ln_w=rw_ln_w, rw_ln_b=rw_ln_b, w_br_a=w_br_a,
             w_br_b=w_br_b, w_br_c=w_br_c, w_out=w_out, norm_ffn=norm_ffn, peer_wq=peer_wq,
             peer_subkeys=peer_subkeys, peer_u=peer_u, peer_v=peer_v)
    depth = w_in.shape[0]
    p["peer_u_bf"] = _bf(peer_u)
    p["peer_vt_bf"] = jnp.swapaxes(_bf(peer_v), 1, 2)
    bp, t, d = x_prompt.shape
    nseq, seq, _ = x_sample.shape
    assert bp == 1 and d == D_MODEL and seq <= SAMPLE_ROWS // 2 and DN_CONV - 1 <= seq
    caches = [c.reshape(depth, nseq, c.shape[2], 2 * ATT_OUT) for c in (cache_kv_w128, cache_kv_w512, cache_kv_w2048)]
    hp = x_prompt.reshape(t, d)
    hs = x_sample.reshape(nseq * seq, d)
    outs_p, outs_s = [], []
    for l in range(depth):
        lw = _layer_weights(l, p)
        final_w = norm_final if l == depth - 1 else None
        hp, *st_p = _prompt_layer(hp, lw, l, final_w)
        hs, *st_s = _sample_layer(hs, lw, l, final_w, caches, state_dn[l], state_dn_conv[l], state_rw[l],
                                  state_rw_shift[l], nseq, seq)
        outs_p.append(st_p)
        outs_s.append(st_s)
    y_p = hp.reshape(bp, t, d)
    y_s = hs.reshape(nseq, seq, d)

    def stack(outs, pick):
        return jnp.stack([pick(o) for o in outs], axis=0)

    res = [y_p, y_s]
    for gi in range(len(ATT_GROUPS)):
        res.append(stack(outs_p, lambda o: o[0][gi]))
        res.append(stack(outs_s, lambda o: o[0][gi]))
    for idx in (1, 2, 3, 4):
        res.append(stack(outs_p, lambda o: o[idx]))
        res.append(stack(outs_s, lambda o: o[idx]))
    return tuple(res)
```
